```python
import math
import jax, jax.numpy as jnp
from jax import lax
import numpy as np

D_MODEL = 1024
BATCH = 8
SEQ = 8192
DEPTH = 4

GRID_W = 64
CTX_LEN = 256
N_MIXERS = 2
HEAD_DIM = 64
N_HEADS = D_MODEL // HEAD_DIM
N_RWKV = (DEPTH + 1) // 2
N_NA = DEPTH // 2
N_LERP = 6
N_DIRS = 2
D_DECAY_LORA = 64
D_AAA_LORA = 64
D_MV_LORA = 32
LNX_EPS = 64e-5
WIN_H = 8
WIN_W = 16
Q_BLOCK_W = 8
BAND_W = Q_BLOCK_W + WIN_W
N_CBLOCKS = GRID_W // Q_BLOCK_W
RMS_EPS = 1e-6
NEG_INF = -1e30

kernel_name = "hybrid_rwkv7_natten_dit"


def _rmsnorm(x, g):
    xf = x.astype(jnp.float32)
    y = xf * lax.rsqrt(jnp.mean(xf * xf, axis=-1, keepdims=True) + RMS_EPS)
    return (y * g.astype(jnp.float32)).astype(x.dtype)


def _heads(t):
    return t.reshape(*t.shape[:-1], N_HEADS, HEAD_DIM)


def _centred_shift(h):
    pad = jnp.pad(h, ((0, 0), (1, 1), (0, 0)))
    return 0.5 * (pad[:, :-2] + pad[:, 2:]) - h


def _rwkv_project(h, mu, w_rkvg, w0, w1, w2, a0, a1, a2, k_k, k_a, v_first, vres):
    f32 = jnp.float32
    xx = _centred_shift(h)
    xr, xw, xk, xv, xa, xg = [h + xx * mu[n] for n in range(N_LERP)]
    r = xr @ w_rkvg[0]
    k = xk @ w_rkvg[1]
    v = xv @ w_rkvg[2]
    g = jax.nn.silu(xg @ w_rkvg[3])
    if vres is not None:
        v0, v1, v2 = vres
        v = v + (v_first - v) * jax.nn.sigmoid(v0 + (xv @ v1) @ v2)
    lora_w = jnp.einsum('dbtr,drc->dbtc', jnp.tanh(jnp.einsum('btc,dcr->dbtr', xw, w1)), w2)
    w_log = -jax.nn.softplus(-(w0[:, None, None, :] + lora_w).astype(f32)) - 0.5
    decay = jnp.exp(-jnp.exp(w_log))
    lora_a = jnp.einsum('dbtr,drc->dbtc', jnp.einsum('btc,dcr->dbtr', xa, a1), a2)
    a = jax.nn.sigmoid((a0[:, None, None, :] + lora_a).astype(f32))
    kf = k.astype(f32)
    kk = _heads(kf * k_k)
    kk = kk / jnp.maximum(jnp.sqrt(jnp.sum(kk * kk, axis=-1, keepdims=True)), 1e-12)
    k_dir = kf[None] * (1.0 + (a - 1.0) * k_a)
    return (_heads(r.astype(f32)), _heads(k_dir), _heads(v.astype(f32)), kk,
            _heads(a), _heads(decay), g, v)


def _rwkv_scan(r, w, k, v, kk, a, s0, reverse):
    def step(s, inp):
        r_t, w_t, k_t, v_t, kk_t, a_t = inp
        sa = jnp.einsum('bhvk,bhk->bhv', s, -kk_t)
        s = (s * w_t[:, :, None, :]
             + sa[..., None] * (kk_t * a_t)[:, :, None, :]
             + v_t[..., None] * k_t[:, :, None, :])
        y = jnp.einsum('bhvk,bhk->bhv', s, r_t)
        return s, y
    xs = tuple(jnp.moveaxis(t, 1, 0) for t in (r, w, k, v, kk, a))
    s_final, ys = lax.scan(step, s0, xs, reverse=reverse)
    return jnp.moveaxis(ys, 0, 1), s_final


def _rwkv_output(y, r, k_dir, v, g, r_k, lnx_w, lnx_b, w_out):
    B, T = y.shape[:2]
    mu = jnp.mean(y, axis=-1, keepdims=True)
    var = jnp.mean(jnp.square(y - mu), axis=-1, keepdims=True)
    yn = ((y - mu) * lax.rsqrt(var + LNX_EPS)).reshape(B, T, D_MODEL) * lnx_w + lnx_b
    bonus = jnp.sum(jnp.sum(r[None] * k_dir * r_k, axis=-1, keepdims=True), axis=0) * v
    o = (yn + bonus.reshape(B, T, D_MODEL)).astype(g.dtype)
    return (o * g) @ w_out


def _rwkv_mixer(h_lat, h_ctx, mu, w_rkvg, w0, w1, w2, a0, a1, a2, k_k, k_a, r_k,
                lnx_w, lnx_b, w_out, v_first_lat, v_first_ctx, vres, with_ctx_out):
    B = h_lat.shape[0]
    r_l, k_l, v_l, kk_l, a_l, w_l, g_l, vraw_l = _rwkv_project(
        h_lat, mu, w_rkvg, w0, w1, w2, a0, a1, a2, k_k, k_a, v_first_lat, vres)
    r_c, k_c, v_c, kk_c, a_c, w_c, g_c, vraw_c = _rwkv_project(
        h_ctx, mu, w_rkvg, w0, w1, w2, a0, a1, a2, k_k, k_a, v_first_ctx, vres)
    s0 = jnp.zeros((B, N_HEADS, HEAD_DIM, HEAD_DIM), jnp.float32)
    y_lat = jnp.zeros_like(r_l)
    y_ctx = jnp.zeros_like(r_c)
    for d in range(N_DIRS):
        rev = d == 1
        yc, s_ctx = _rwkv_scan(r_c, w_c[d], k_c[d], v_c, kk_c, a_c[d], s0, rev)
        yl, _ = _rwkv_scan(r_l, w_l[d], k_l[d], v_l, kk_l, a_l[d], s_ctx, rev)
        y_lat = y_lat + yl
        y_ctx = y_ctx + yc
    out_lat = _rwkv_output(y_lat, r_l, k_l, v_l, g_l, r_k, lnx_w, lnx_b, w_out)
    out_ctx = _rwkv_output(y_ctx, r_c, k_c, v_c, g_c, r_k, lnx_w, lnx_b, w_out) if with_ctx_out else None
    return out_lat, out_ctx, vraw_l, vraw_c


def _na_column_tables():
    j = np.arange(GRID_W)
    win_start = np.clip(j - WIN_W // 2, 0, GRID_W - WIN_W)
    cb = np.arange(N_CBLOCKS)
    band_start = np.clip(cb * Q_BLOCK_W - WIN_W // 2, 0, GRID_W - BAND_W)
    band_cols = band_start[:, None] + np.arange(BAND_W)[None, :]
    q_cols = cb[:, None] * Q_BLOCK_W + np.arange(Q_BLOCK_W)[None, :]
    key_c = band_cols[:, None, :]
    qs = win_start[q_cols][:, :, None]
    valid = (key_c >= qs) & (key_c < qs + WIN_W)
    dc = np.clip(key_c - q_cols[:, :, None], -(WIN_W - 1), WIN_W - 1) + (WIN_W - 1)
    return band_cols, valid, dc


def _na_mixer(h_lat, h_ctx, w_in, b_in, rpb, w_out, with_ctx_out):
    B, T, D = h_lat.shape
    n_ctx = h_ctx.shape[1]
    rows = T // GRID_W
    kh = min(WIN_H, rows)
    scale = HEAD_DIM ** -0.5
    q, k, v, g = jnp.split(h_lat @ w_in + b_in, 4, axis=-1)
    q = (q * scale).reshape(B, rows, N_CBLOCKS, Q_BLOCK_W, N_HEADS, HEAD_DIM)
    k = k.reshape(B, rows, GRID_W, N_HEADS, HEAD_DIM)
    v = v.reshape(B, rows, GRID_W, N_HEADS, HEAD_DIM)
    k_c, v_c = jnp.split(h_ctx @ w_in[:, D:3 * D] + b_in[D:3 * D], 2, axis=-1)
    k_c, v_c = _heads(k_c), _heads(v_c)
    band_cols, valid_np, dc = _na_column_tables()
    valid = jnp.asarray(valid_np)[:, :, None, :]

    def row_block(args):
        r, q_r = args
        rs = jnp.clip(r - WIN_H // 2, 0, rows - kh)
        k_band = lax.dynamic_slice_in_dim(k, rs, kh, axis=1)[:, :, band_cols]
        v_band = lax.dynamic_slice_in_dim(v, rs, kh, axis=1)[:, :, band_cols]
        dr = rs - r + jnp.arange(kh) + (WIN_H - 1)
        bias = jnp.transpose(rpb[:, dr][:, :, dc], (0, 2, 3, 1, 4))
        s_win = (jnp.einsum('bcqhd,bicwhd->bhcqiw', q_r, k_band).astype(jnp.float32)
                 + bias.astype(jnp.float32)[None])
        s_win = jnp.where(valid, s_win, NEG_INF).reshape(B, N_HEADS, N_CBLOCKS, Q_BLOCK_W, kh * BAND_W)
        s_ctx = jnp.einsum('bcqhd,bshd->bhcqs', q_r, k_c).astype(jnp.float32)
        p = jax.nn.softmax(jnp.concatenate([s_win, s_ctx], axis=-1), axis=-1).astype(v.dtype)
        p_win = p[..., :kh * BAND_W].reshape(B, N_HEADS, N_CBLOCKS, Q_BLOCK_W, kh, BAND_W)
        p_ctx = p[..., kh * BAND_W:]
        o = (jnp.einsum('bhcqiw,bicwhd->bcqhd', p_win, v_band)
             + jnp.einsum('bhcqs,bshd->bcqhd', p_ctx, v_c))
        return o.reshape(B, GRID_W, D)

    o = lax.map(row_block, (jnp.arange(rows), jnp.moveaxis(q, 1, 0)))
    o = jnp.moveaxis(o, 0, 1).reshape(B, T, D)
    out_lat = (o * jax.nn.silu(g)) @ w_out
    out_ctx = None
    if with_ctx_out:
        q_cx = _heads((h_ctx @ w_in[:, :D] + b_in[:D]) * scale)
        g_cx = h_ctx @ w_in[:, 3 * D:] + b_in[3 * D:]
        s = jnp.einsum('bqhd,bshd->bhqs', q_cx, k_c).astype(jnp.float32)
        p = jax.nn.softmax(s, axis=-1).astype(v_c.dtype)
        o_c = jnp.einsum('bhqs,bshd->bqhd', p, v_c).reshape(B, n_ctx, D)
        out_ctx = (o_c * jax.nn.silu(g_cx)) @ w_out
    return out_lat, out_ctx


def setup_inputs(seed: int = 0) -> dict:
    key = jax.random.key(seed)
    ks = iter(jax.random.split(key, 40))
    D = D_MODEL
    f32 = jnp.float32

    def nrm(shape, s):
        return jax.random.normal(next(ks), shape, f32) * s

    def uni(shape, lo, hi):
        return jax.random.uniform(next(ks), shape, f32, lo, hi)

    return {
        "x": nrm((BATCH, SEQ, D), 1.0),
        "c": nrm((BATCH, D), 1.0),
        "ctx": nrm((BATCH, CTX_LEN, D), 1.0),
        "c_ctx": nrm((D,), 1.0),
        "ada_w": nrm((DEPTH, D, 3 * D), 0.5 * D ** -0.5),
        "ada_b": nrm((DEPTH, 3 * D), 0.02),
        "pre_g": 1.0 + nrm((DEPTH, D), 0.05),
        "post_g": 1.0 + nrm((DEPTH, D), 0.05),
        "rw_mu": uni((N_RWKV, N_LERP, D), 0.0, 1.0),
        "rw_w_rkvg": nrm((N_RWKV, 4, D, D), D ** -0.5),
        "rw_w0": uni((N_RWKV, N_DIRS, D), -6.0, -1.0),
        "rw_w1": nrm((N_RWKV, N_DIRS, D, D_DECAY_LORA), D ** -0.5),
        "rw_w2": nrm((N_RWKV, N_DIRS, D_DECAY_LORA, D), 0.5 * D_DECAY_LORA ** -0.5),
        "rw_a0": nrm((N_RWKV, N_DIRS, D), 0.5),
        "rw_a1": nrm((N_RWKV, N_DIRS, D, D_AAA_LORA), D ** -0.5),
        "rw_a2": nrm((N_RWKV, N_DIRS, D_AAA_LORA, D), 0.5 * D_AAA_LORA ** -0.5),
        "rw_v0": nrm((N_RWKV - 1, D), 0.5),
        "rw_v1": nrm((N_RWKV - 1, D, D_MV_LORA), D ** -0.5),
        "rw_v2": nrm((N_RWKV - 1, D_MV_LORA, D), 0.5 * D_MV_LORA ** -0.5),
        "rw_k_k": 0.85 + nrm((N_RWKV, D), 0.05),
        "rw_k_a": 1.0 + nrm((N_RWKV, D), 0.05),
        "rw_r_k": nrm((N_RWKV, N_HEADS, HEAD_DIM), 0.1),
        "rw_lnx_w": 1.0 + nrm((N_RWKV, D), 0.05),
        "rw_lnx_b": nrm((N_RWKV, D), 0.02),
        "rw_w_out": nrm((N_RWKV, D, D), D ** -0.5),
        "na_w_in": nrm((N_NA, D, 4 * D), D ** -0.5),
        "na_b_in": nrm((N_NA, 4 * D), 0.02),
        "na_rpb": nrm((N_NA, N_HEADS, 2 * WIN_H - 1, 2 * WIN_W - 1), 0.5),
        "na_w_out": nrm((N_NA, D, D), D ** -0.5),
    }


def reference(x, c, ctx, c_ctx, ada_w, ada_b, pre_g, post_g, rw_mu, rw_w_rkvg, rw_w0, rw_w1, rw_w2,
              rw_a0, rw_a1, rw_a2, rw_v0, rw_v1, rw_v2, rw_k_k, rw_k_a, rw_r_k, rw_lnx_w, rw_lnx_b,
              rw_w_out, na_w_in, na_b_in, na_rpb, na_w_out):
    silu_c = jax.nn.silu(c)
    silu_cc = jax.nn.silu(c_ctx)
    v_first_lat = None
    v_first_ctx = None
    for i in range(DEPTH):
        last = i == DEPTH - 1
        j = i // N_MIXERS
        shift, scale, gate = jnp.split(silu_c @ ada_w[i] + ada_b[i], 3, axis=-1)
        shift_c, scale_c, gate_c = jnp.split(silu_cc @ ada_w[i] + ada_b[i], 3, axis=-1)
        h = _rmsnorm(x, pre_g[i]) * (1.0 + scale[:, None]) + shift[:, None]
        hc = _rmsnorm(ctx, pre_g[i]) * (1.0 + scale_c) + shift_c
        if i % N_MIXERS == 0:
            vres = None if j == 0 else (rw_v0[j - 1], rw_v1[j - 1], rw_v2[j - 1])
            out, out_c, v_lat, v_ctx = _rwkv_mixer(
                h, hc, rw_mu[j], rw_w_rkvg[j], rw_w0[j], rw_w1[j], rw_w2[j], rw_a0[j], rw_a1[j],
                rw_a2[j], rw_k_k[j], rw_k_a[j], rw_r_k[j], rw_lnx_w[j], rw_lnx_b[j], rw_w_out[j],
                v_first_lat, v_first_ctx, vres, not last)
            if j == 0:
                v_first_lat, v_first_ctx = v_lat, v_ctx
        else:
            out, out_c = _na_mixer(h, hc, na_w_in[j], na_b_in[j], na_rpb[j], na_w_out[j], not last)
        x = x + gate[:, None] * _rmsnorm(out, post_g[i])
        if not last:
            ctx = ctx + gate_c * _rmsnorm(out_c, post_g[i])
    return x
```

```python
import functools
import math

import numpy as np
import jax
import jax.numpy as jnp
from jax import lax
from jax.experimental import pallas as pl
from jax.experimental.pallas import tpu as pltpu

F32 = jnp.float32
BF16 = jnp.bfloat16

D = 1024
HD = 64
NH = D // HD
PW = 2 * HD
NPAIR = D // PW
TM = 256
CH = 64
GRID_W = 64
WIN_H = 8
WIN_W = 16
RMS_EPS = 1e-6
LNX_EPS = 64e-5
NEG_INF = -1e30
EXP_M05 = math.exp(-0.5)
RQ = 4
VMEM_LIMIT = 56 * 1024 * 1024


def _bf(x):
    return x.astype(BF16)


def _dot(a, b):
    return jnp.dot(a, b, preferred_element_type=F32)


def _dot_nt(a, b):
    return lax.dot_general(a, b, (((1,), (1,)), ((), ())), preferred_element_type=F32)


def _dot_tn(a, b):
    return lax.dot_general(a, b, (((0,), (0,)), ((), ())), preferred_element_type=F32)


def _dot_split(x, e):
    hi = _bf(x)
    lo = _bf(x - hi.astype(F32))
    return _dot(hi, e) + _dot(lo, e)


def _seg_sum(x, e, et):
    return _dot_split(_dot_split(x, e), et)


def _sigmoid(x):
    return 1.0 / (1.0 + jnp.exp(-x))


def _prenorm(x, g, scale, shift):
    ms = jnp.mean(x * x, axis=-1, keepdims=True)
    return (x * lax.rsqrt(ms + RMS_EPS) * g) * (1.0 + scale) + shift


def _const_spec(shape):
    nd = len(shape)
    return pl.BlockSpec(shape, lambda *_: (0,) * nd, pipeline_mode=pl.Buffered(1))


def _params(sem):
    return pltpu.CompilerParams(dimension_semantics=sem, vmem_limit_bytes=VMEM_LIMIT)


def _adaln_kernel(s_ref, w_ref, b_ref, o_ref):
    s = s_ref[...]
    w = w_ref[0]
    hi = _bf(s)
    lo = _bf(s - hi.astype(F32))
    whi = _bf(w)
    wlo = _bf(w - whi.astype(F32))
    o_ref[0] = _dot(hi, whi) + _dot(lo, whi) + _dot(hi, wlo) + b_ref[0]


def _adaln(silu_rows, ada_w, ada_b):
    depth = ada_w.shape[0]
    nrow = silu_rows.shape[0]
    return pl.pallas_call(
        _adaln_kernel,
        grid=(depth, 3),
        in_specs=[pl.BlockSpec((nrow, D), lambda i, j: (0, 0)),
                  pl.BlockSpec((1, D, D), lambda i, j: (i, 0, j)),
                  pl.BlockSpec((1, 1, D), lambda i, j: (i, 0, j))],
        out_specs=pl.BlockSpec((1, nrow, D), lambda i, j: (i, 0, j)),
        out_shape=jax.ShapeDtypeStruct((depth, nrow, 3 * D), F32),
        compiler_params=_params(("parallel", "parallel")),
        name="adaln",
    )(silu_rows, ada_w, ada_b.reshape(depth, 1, 3 * D))


_V_PRE_G, _V_MU, _V_KK, _V_KA, _V_RK, _V_W0, _V_A0, _V_V0 = 0, 1, 7, 8, 9, 10, 12, 14


def _rwkv_proj_kernel(has_vres, nt, *refs):
    if has_vres:
        (x_ref, xp_ref, xn_ref, mod_ref, vec_ref, wr_ref, wk_ref, wv_ref, wg_ref, w1_ref, w2_ref,
         a1_ref, a2_ref, e_ref, et_ref, v1_ref, v2_ref, vf_ref,
         r_ref, v_ref, kk_ref, g_ref, bonus_ref, kd0_ref, kd1_ref, b0_ref, b1_ref, lw0_ref, lw1_ref) = refs
    else:
        (x_ref, xp_ref, xn_ref, mod_ref, vec_ref, wr_ref, wk_ref, wv_ref, wg_ref, w1_ref, w2_ref,
         a1_ref, a2_ref, e_ref, et_ref,
         r_ref, v_ref, kk_ref, g_ref, bonus_ref, kd0_ref, kd1_ref, b0_ref, b1_ref, lw0_ref, lw1_ref) = refs
    t = pl.program_id(1)
    shift = mod_ref[0, 0, :, 0:D]
    scale = mod_ref[0, 0, :, D:2 * D]
    g_pre = vec_ref[_V_PRE_G:_V_PRE_G + 1]

    h = _prenorm(x_ref[0], g_pre, scale, shift)
    hp = _prenorm(xp_ref[0], g_pre, scale, shift)[7:8]
    hn = _prenorm(xn_ref[0], g_pre, scale, shift)[0:1]
    hp = jnp.where(t >= 2, hp, 0.0)
    hn = jnp.where(jnp.logical_and(t >= 1, t < nt - 1), hn, 0.0)
    row = lax.broadcasted_iota(jnp.int32, (TM, 1), 0)
    h_m1 = jnp.where(row == 0, hp, pltpu.roll(h, 1, 0))
    h_p1 = jnp.where(row == TM - 1, hn, pltpu.roll(h, TM - 1, 0))
    xx = 0.5 * (h_m1 + h_p1) - h

    def lerp(n):
        return _bf(h + xx * vec_ref[_V_MU + n:_V_MU + n + 1])

    r = _dot(lerp(0), wr_ref[...])
    k = _dot(lerp(2), wk_ref[...])
    xv = lerp(3)
    v = _dot(xv, wv_ref[...])
    g = _dot(lerp(5), wg_ref[...])
    g = g * _sigmoid(g)
    if has_vres:
        lv = _dot(_bf(_dot(xv, v1_ref[...])), v2_ref[...])
        v = v + (vf_ref[0] - v) * _sigmoid(vec_ref[_V_V0:_V_V0 + 1] + lv)
    tw = _bf(jnp.tanh(_dot(lerp(1), w1_ref[...])))
    la = _bf(_dot(lerp(4), a1_ref[...]))

    e = e_ref[...]
    et = et_ref[...]
    kkr = k * vec_ref[_V_KK:_V_KK + 1]
    kk = kkr / jnp.maximum(jnp.sqrt(_seg_sum(kkr * kkr, e, et)), 1e-12)
    k_a = vec_ref[_V_KA:_V_KA + 1]

    ksum = None
    for d, (kd_ref, b_ref, lw_ref) in enumerate(((kd0_ref, b0_ref, lw0_ref), (kd1_ref, b1_ref, lw1_ref))):
        wl = vec_ref[_V_W0 + d:_V_W0 + d + 1] + _dot(tw, w2_ref[d])
        lw_ref[0] = -EXP_M05 * _sigmoid(wl)
        a = _sigmoid(vec_ref[_V_A0 + d:_V_A0 + d + 1] + _dot(la, a2_ref[d]))
        kd = k * (1.0 + (a - 1.0) * k_a)
        kd_ref[0] = kd
        b_ref[0] = kk * a
        ksum = kd if ksum is None else ksum + kd

    r_ref[0] = r
    v_ref[0] = v
    kk_ref[0] = kk
    g_ref[0] = g
    bonus_ref[0] = _seg_sum(r * ksum * vec_ref[_V_RK:_V_RK + 1], e, et) * v


def _rwkv_proj(xc, mod, vec, wr, wk, wv, wg, w1c, w2z, a1c, a2z, e, et, vres):
    B, TT, _ = xc.shape
    nt = TT // TM
    has_vres = vres is not None
    tile = pl.BlockSpec((1, TM, D), lambda b, t: (b, t, 0))
    in_specs = [
        tile,
        pl.BlockSpec((1, 8, D), lambda b, t: (b, jnp.maximum(t * (TM // 8) - 1, 0), 0)),
        pl.BlockSpec((1, 8, D), lambda b, t: (b, jnp.minimum((t + 1) * (TM // 8), TT // 8 - 1), 0)),
        pl.BlockSpec((1, 1, 1, 3 * D), lambda b, t: (b, jnp.minimum(t, 1), 0, 0)),
        _const_spec((16, D)),
        _const_spec((D, D)), _const_spec((D, D)), _const_spec((D, D)), _const_spec((D, D)),
        _const_spec((D, PW)), _const_spec((2, PW, D)), _const_spec((D, PW)), _const_spec((2, PW, D)),
        _const_spec((D, PW)), _const_spec((PW, D)),
    ]
    args = [xc, xc, xc, mod, vec, wr, wk, wv, wg, w1c, w2z, a1c, a2z, e, et]
    if has_vres:
        v1p, v2p, v_first = vres
        in_specs += [_const_spec((D, PW)), _const_spec((PW, D)), tile]
        args += [v1p, v2p, v_first]
    out = jax.ShapeDtypeStruct((B, TT, D), F32)
    return pl.pallas_call(
        functools.partial(_rwkv_proj_kernel, has_vres, nt),
        grid=(B, nt),
        in_specs=in_specs,
        out_specs=[tile] * 11,
        out_shape=[out] * 11,
        compiler_params=_params(("parallel", "parallel")),
        name="rwkv_proj",
    )(*args)


def _cumsum_rows(x, rev):
    row = lax.broadcasted_iota(jnp.int32, (CH, 1), 0)
    s = 1
    while s < CH:
        if rev:
            x = x + jnp.where(row < CH - s, pltpu.roll(x, CH - s, 0), 0.0)
        else:
            x = x + jnp.where(row >= s, pltpu.roll(x, s, 0), 0.0)
        s *= 2
    return x


def _scan_kernel(rev, r_ref, kk_ref, v_ref, k_ref, b_ref, lw_ref, y_ref, st_ref):
    @pl.when(pl.program_id(2) == 0)
    def _():
        st_ref[...] = jnp.zeros_like(st_ref)

    lane = lax.broadcasted_iota(jnp.int32, (1, PW), 1)
    m0 = (lane < HD).astype(F32)
    m1 = 1.0 - m0
    ri = lax.broadcasted_iota(jnp.int32, (PW, PW), 0)
    ci = lax.broadcasted_iota(jnp.int32, (PW, PW), 1)
    same = (ri >> 6) == (ci >> 6)
    rl = ri & (CH - 1)
    cl = ci & (CH - 1)
    if rev:
        strict = jnp.logical_and(same, cl > rl)
        incl = jnp.logical_and(same, cl >= rl)
    else:
        strict = jnp.logical_and(same, cl < rl)
        incl = jnp.logical_and(same, cl <= rl)
    eye = (ri == ci).astype(F32)

    def stack(x):
        return jnp.concatenate([x * m0, x * m1], axis=0)

    nch = TM // CH
    for c in (range(nch - 1, -1, -1) if rev else range(nch)):
        sl = pl.ds(c * CH, CH)
        lw = lw_ref[0, sl, :]
        cum = _cumsum_rows(lw, rev)
        last = cum[0:1] if rev else cum[CH - 1:CH]
        e_c = jnp.exp(cum)
        e_n = jnp.exp(-cum)
        e_h = jnp.exp(last - cum)
        kk = kk_ref[0, sl, :]
        b = b_ref[0, sl, :]
        k = k_ref[0, sl, :]
        a2 = stack(-kk * jnp.exp(cum - lw))
        r2 = stack(r_ref[0, sl, :] * e_c)
        b2 = stack(b * e_n)
        k2 = stack(k * e_n)
        v2 = stack(v_ref[0, sl, :])
        bh2 = stack(b * e_h)
        kh2 = stack(k * e_h)

        m = _dot_nt(_bf(jnp.concatenate([a2, r2], axis=0)), _bf(jnp.concatenate([b2, k2], axis=0)))
        m_ab = jnp.where(strict, m[:PW, :PW], 0.0)
        m_ak = jnp.where(strict, m[:PW, PW:], 0.0)
        m_rb = jnp.where(incl, m[PW:, :PW], 0.0)
        m_rk = jnp.where(incl, m[PW:, PW:], 0.0)

        q = eye + m_ab
        lb = _bf(m_ab)
        l = _dot(lb, lb)
        for _ in range(4):
            lb = _bf(l)
            res = _dot(lb, jnp.concatenate([lb, _bf(q)], axis=1))
            l = res[:, :PW]
            q = q + res[:, PW:]
        tinv = q + _dot(_bf(l), _bf(q))

        mv = _dot(_bf(m_ak), _bf(v2))
        pu = _dot(_bf(tinv), _bf(jnp.concatenate([a2, mv], axis=1)))
        p1 = pu[:, :PW]
        u0 = pu[:, PW:]
        rhs = _bf(jnp.concatenate([jnp.concatenate([u0, p1], axis=1),
                                   jnp.concatenate([v2, jnp.zeros_like(v2)], axis=1)], axis=0))
        yp = _dot(_bf(jnp.concatenate([m_rb, m_rk], axis=1)), rhs)
        gh = _dot_tn(_bf(jnp.concatenate([bh2, kh2], axis=0)), rhs)
        gmat = gh[:, PW:] + eye * jnp.exp(last)
        q2 = r2 + yp[:, PW:]

        st = _bf(st_ref[...])
        y2 = _dot(_bf(q2), st) + yp[:, :PW]
        st_ref[...] = _dot(_bf(gmat), st) + gh[:, :PW]
        y_ref[0, sl, :] = y2[:CH] + y2[CH:]


def _rwkv_scan(rev, r, kk, v, kd, bv, lw):
    B, TT, _ = r.shape
    nt = TT // TM
    if rev:
        idx = lambda b, p, i: (b, jnp.where(i == 0, 0, nt - i), p)
    else:
        idx = lambda b, p, i: (b, i, p)
    tile = pl.BlockSpec((1, TM, PW), idx)
    return pl.pallas_call(
        functools.partial(_scan_kernel, rev),
        grid=(B, NPAIR, nt),
        in_specs=[tile] * 6,
        out_specs=tile,
        out_shape=jax.ShapeDtypeStruct((B, TT, D), F32),
        scratch_shapes=[pltpu.VMEM((PW, PW), F32)],
        compiler_params=_params(("parallel", "parallel", "arbitrary")),
        name="rwkv_scan_rev" if rev else "rwkv_scan_fwd",
    )(r, kk, v, kd, bv, lw)


def _out_tail(o, w_ref, post_g, gate, x):
    out = _dot(_bf(o), w_ref[...])
    ms = jnp.mean(out * out, axis=-1, keepdims=True)
    return x + gate * (out * lax.rsqrt(ms + RMS_EPS) * post_g)


def _rwkv_out_kernel(y0_ref, y1_ref, bonus_ref, g_ref, x_ref, mod_ref, vec_ref, w_ref, e_ref, et_ref, o_ref):
    e = e_ref[...]
    et = et_ref[...]
    y = y0_ref[0] + y1_ref[0]
    mu = _seg_sum(y, e, et) * (1.0 / HD)
    yc = y - mu
    var = _seg_sum(yc * yc, e, et) * (1.0 / HD)
    yn = yc * lax.rsqrt(var + LNX_EPS) * vec_ref[0:1] + vec_ref[1:2]
    o = (yn + bonus_ref[0]) * g_ref[0]
    o_ref[0] = _out_tail(o, w_ref, vec_ref[2:3], mod_ref[0, 0, :, 2 * D:3 * D], x_ref[0])


def _rwkv_out(y0, y1, bonus, g, xc, mod, vec, w_out, e, et):
    B, TT, _ = xc.shape
    nt = TT // TM
    tile = pl.BlockSpec((1, TM, D), lambda b, t: (b, t, 0))
    return pl.pallas_call(
        _rwkv_out_kernel,
        grid=(B, nt),
        in_specs=[tile] * 5 + [
            pl.BlockSpec((1, 1, 1, 3 * D), lambda b, t: (b, jnp.minimum(t, 1), 0, 0)),
            _const_spec((8, D)), _const_spec((D, D)), _const_spec((D, PW)), _const_spec((PW, D))],
        out_specs=tile,
        out_shape=jax.ShapeDtypeStruct((B, TT, D), F32),
        compiler_params=_params(("parallel", "parallel")),
        name="rwkv_out",
    )(y0, y1, bonus, g, xc, mod, vec, w_out, e, et)


def _na_proj_kernel(x_ref, mod_ref, vec_ref, w_ref, bias_ref, q_ref, k_ref, v_ref, g_ref):
    shift = mod_ref[0, 0, :, 0:D]
    scale = mod_ref[0, 0, :, D:2 * D]
    h = _bf(_prenorm(x_ref[0], vec_ref[0:1], scale, shift))
    q = _dot(h, w_ref[:, 0:D]) + bias_ref[:, 0:D]
    q_ref[0] = _bf(q * (HD ** -0.5))
    k_ref[0] = _bf(_dot(h, w_ref[:, D:2 * D]) + bias_ref[:, D:2 * D])
    v_ref[0] = _bf(_dot(h, w_ref[:, 2 * D:3 * D]) + bias_ref[:, 2 * D:3 * D])
    g = _dot(h, w_ref[:, 3 * D:4 * D]) + bias_ref[:, 3 * D:4 * D]
    g_ref[0] = g * _sigmoid(g)


def _na_proj(xc, mod, vec, w_in, b_in):
    B, TT, _ = xc.shape
    nt = TT // TM
    tile = pl.BlockSpec((1, TM, D), lambda b, t: (b, t, 0))
    bf = jax.ShapeDtypeStruct((B, TT, D), BF16)
    return pl.pallas_call(
        _na_proj_kernel,
        grid=(B, nt),
        in_specs=[tile,
                  pl.BlockSpec((1, 1, 1, 3 * D), lambda b, t: (b, jnp.minimum(t, 1), 0, 0)),
                  _const_spec((8, D)), _const_spec((D, 4 * D)), _const_spec((1, 4 * D))],
        out_specs=[tile] * 4,
        out_shape=[bf, bf, bf, jax.ShapeDtypeStruct((B, TT, D), F32)],
        compiler_params=_params(("parallel", "parallel")),
        name="na_proj",
    )(xc, mod, vec, w_in, b_in)


def _softmax_pv(s_list, v_list):
    mx = None
    for s in s_list:
        m = jnp.max(s, axis=-1, keepdims=True)
        mx = m if mx is None else jnp.maximum(mx, m)
    den = None
    acc = None
    for s, v in zip(s_list, v_list):
        p = jnp.exp(s - mx)
        d = jnp.sum(p, axis=-1, keepdims=True)
        o = _dot(_bf(p), v)
        den = d if den is None else den + d
        acc = o if acc is None else acc + o
    return acc / den


def _na_attn_kernel(rows, q_ref, k_ref, v_ref, bias_ref, o_ref):
    lane = lax.broadcasted_iota(jnp.int32, (1, PW), 1)
    m0 = lane < HD
    kc = k_ref[0, 0:TM, :]
    vc = v_ref[0, 0:TM, :]
    step = pl.program_id(2)
    nkeys = WIN_H * GRID_W
    zero = jnp.zeros((), BF16)
    for rr in range(RQ):
        r = step * RQ + rr
        rs = jnp.clip(r - WIN_H // 2, 0, rows - WIN_H)
        q = q_ref[0, rr * GRID_W:(rr + 1) * GRID_W, :]
        q2 = jnp.concatenate([jnp.where(m0, q, zero), jnp.where(m0, zero, q)], axis=0)
        start = pl.multiple_of(TM + rs * GRID_W, GRID_W)
        kw = k_ref[0, pl.ds(start, nkeys), :]
        vw = v_ref[0, pl.ds(start, nkeys), :]
        s_win = _dot_nt(q2, kw) + bias_ref[0, r - rs]
        s_ctx = _dot_nt(q2, kc)
        o2 = _softmax_pv([s_win, s_ctx], [vw, vc])
        o_ref[0, rr * GRID_W:(rr + 1) * GRID_W, :] = _bf(jnp.where(m0, o2[:GRID_W], o2[GRID_W:]))


def _na_attn(q, k, v, bias):
    B, TT, _ = q.shape
    T = TT - TM
    rows = T // GRID_W
    qb = RQ * GRID_W
    kv = pl.BlockSpec((1, TT, PW), lambda b, p, s: (b, 0, p))
    return pl.pallas_call(
        functools.partial(_na_attn_kernel, rows),
        grid=(B, NPAIR, rows // RQ),
        in_specs=[pl.BlockSpec((1, qb, PW), lambda b, p, s: (b, TM // qb + s, p)), kv, kv,
                  pl.BlockSpec((1, WIN_H, PW, WIN_H * GRID_W), lambda b, p, s: (p, 0, 0, 0))],
        out_specs=pl.BlockSpec((1, qb, PW), lambda b, p, s: (b, s, p)),
        out_shape=jax.ShapeDtypeStruct((B, T, D), BF16),
        compiler_params=_params(("parallel", "parallel", "arbitrary")),
        name="na_attn",
    )(q, k, v, bias)


def _ctx_attn_kernel(q_ref, k_ref, v_ref, o_ref):
    lane = lax.broadcasted_iota(jnp.int32, (1, PW), 1)
    m0 = lane < HD
    zero = jnp.zeros((), BF16)
    q = q_ref[0]
    q2 = jnp.concatenate([jnp.where(m0, q, zero), jnp.where(m0, zero, q)], axis=0)
    kc = k_ref[0]
    o2 = _softmax_pv([_dot_nt(q2, kc)], [v_ref[0]])
    o_ref[0] = _bf(jnp.where(m0, o2[:TM], o2[TM:]))


def _ctx_attn(q, k, v):
    B = q.shape[0]
    blk = pl.BlockSpec((1, TM, PW), lambda b, p: (b, 0, p))
    return pl.pallas_call(
        _ctx_attn_kernel,
        grid=(B, NPAIR),
        in_specs=[blk, blk, blk],
        out_specs=blk,
        out_shape=jax.ShapeDtypeStruct((B, TM, D), BF16),
        compiler_params=_params(("parallel", "parallel")),
        name="ctx_attn",
    )(q, k, v)


def _na_out_kernel(with_ctx, *refs):
    if with_ctx:
        ol_ref, oc_ref, g_ref, x_ref, mod_ref, vec_ref, w_ref, o_ref = refs
        o = jnp.where(pl.program_id(1) == 0, oc_ref[0], ol_ref[0])
    else:
        ol_ref, g_ref, x_ref, mod_ref, vec_ref, w_ref, o_ref = refs
        o = ol_ref[0]
    o = o.astype(F32) * g_ref[0]
    o_ref[0] = _out_tail(o, w_ref, vec_ref[1:2], mod_ref[0, 0, :, 2 * D:3 * D], x_ref[0])


def _na_out(o_lat, o_ctx, g, xc, mod, vec, w_out):
    B, TT, _ = xc.shape
    nt = TT // TM
    with_ctx = o_ctx is not None
    consts = [_const_spec((8, D)), _const_spec((D, D))]
    if with_ctx:
        tile = pl.BlockSpec((1, TM, D), lambda b, t: (b, t, 0))
        in_specs = [pl.BlockSpec((1, TM, D), lambda b, t: (b, jnp.maximum(t - 1, 0), 0)),
                    pl.BlockSpec((1, TM, D), lambda b, t: (b, 0, 0)),
                    tile, tile,
                    pl.BlockSpec((1, 1, 1, 3 * D), lambda b, t: (b, jnp.minimum(t, 1), 0, 0))] + consts
        args = (o_lat, o_ctx, g, xc, mod, vec, w_out)
        grid, out_spec, out_rows = (B, nt), tile, TT
    else:
        lat = pl.BlockSpec((1, TM, D), lambda b, t: (b, t + 1, 0))
        out_spec = pl.BlockSpec((1, TM, D), lambda b, t: (b, t, 0))
        in_specs = [out_spec, lat, lat,
                    pl.BlockSpec((1, 1, 1, 3 * D), lambda b, t: (b, 1, 0, 0))] + consts
        args = (o_lat, g, xc, mod, vec, w_out)
        grid, out_rows = (B, nt - 1), TT - TM
    return pl.pallas_call(
        functools.partial(_na_out_kernel, with_ctx),
        grid=grid,
        in_specs=in_specs,
        out_specs=out_spec,
        out_shape=jax.ShapeDtypeStruct((B, out_rows, D), F32),
        compiler_params=_params(("parallel", "parallel")),
        name="na_out",
    )(*args)


def _na_bias_table(rpb):
    j = np.arange(GRID_W)
    win_start = np.clip(j - WIN_W // 2, 0, GRID_W - WIN_W)
    kcol = np.arange(GRID_W)
    valid = (kcol[None, :] >= win_start[:, None]) & (kcol[None, :] < win_start[:, None] + WIN_W)
    dc = np.clip(kcol[None, :] - j[:, None], -(WIN_W - 1), WIN_W - 1) + (WIN_W - 1)
    off = np.arange(WIN_H)
    i = np.arange(WIN_H)
    dr = i[None, :] - off[:, None] + (WIN_H - 1)
    tab = rpb[:, dr][:, :, :, dc]
    tab = jnp.where(jnp.asarray(valid)[None, None, None], tab, NEG_INF)
    tab = jnp.transpose(tab, (0, 1, 3, 2, 4)).reshape(NH, WIN_H, GRID_W, WIN_H * GRID_W)
    tab = tab.reshape(NPAIR, 2, WIN_H, GRID_W, WIN_H * GRID_W)
    return jnp.transpose(tab, (0, 2, 1, 3, 4)).reshape(NPAIR, WIN_H, PW, WIN_H * GRID_W).astype(F32)


def _pad_rows(m, rows):
    return jnp.pad(m, ((0, rows - m.shape[0]), (0, 0)))


def kernel(x, c, ctx, c_ctx, ada_w, ada_b, pre_g, post_g, rw_mu, rw_w_rkvg, rw_w0, rw_w1, rw_w2, rw_a0, rw_a1, rw_a2, rw_v0, rw_v1, rw_v2, rw_k_k, rw_k_a, rw_r_k, rw_lnx_w, rw_lnx_b, rw_w_out, na_w_in, na_b_in, na_rpb, na_w_out):
    B, T, _ = x.shape
    depth = ada_w.shape[0]
    assert ctx.shape[1] == TM and T % (RQ * GRID_W) == 0 and T % TM == 0 and T // GRID_W >= WIN_H

    seg = (np.arange(D)[:, None] // HD) == np.arange(PW)[None, :]
    e = jnp.asarray(seg, BF16)
    et = jnp.asarray(seg.T, BF16)

    cond = jnp.concatenate([c, c_ctx[None, :]], axis=0)
    nrow = -(-(B + 1) // 8) * 8
    cond = _pad_rows(cond * jax.nn.sigmoid(cond), nrow)
    mod_all = _adaln(cond, ada_w, ada_b)
    mod_ctx = jnp.broadcast_to(mod_all[:, B:B + 1], (depth, B, 3 * D))
    mod_all = jnp.stack([mod_ctx, mod_all[:, :B]], axis=2)[:, :, :, None, :]

    xc = jnp.concatenate([ctx, x], axis=1)
    v_first = None
    for i in range(depth):
        last = i == depth - 1
        j = i // 2
        mod = mod_all[i]
        if i % 2 == 0:
            zero = jnp.zeros((D,), F32)
            vec = jnp.stack([pre_g[i], *rw_mu[j], rw_k_k[j], rw_k_a[j], rw_r_k[j].reshape(D),
                             rw_w0[j, 0], rw_w0[j, 1], rw_a0[j, 0], rw_a0[j, 1],
                             rw_v0[j - 1] if j > 0 else zero, zero])
            lora = D // 16
            w1c = _bf(jnp.concatenate([rw_w1[j, 0], rw_w1[j, 1]], axis=1))
            a1c = _bf(jnp.concatenate([rw_a1[j, 0], rw_a1[j, 1]], axis=1))
            zl = jnp.zeros((lora, D), F32)
            w2z = _bf(jnp.stack([jnp.concatenate([rw_w2[j, 0], zl]), jnp.concatenate([zl, rw_w2[j, 1]])]))
            a2z = _bf(jnp.stack([jnp.concatenate([rw_a2[j, 0], zl]), jnp.concatenate([zl, rw_a2[j, 1]])]))
            vres = None
            if j > 0:
                v1p = _bf(jnp.pad(rw_v1[j - 1], ((0, 0), (0, PW - rw_v1.shape[-1]))))
                v2p = _bf(_pad_rows(rw_v2[j - 1], PW))
                vres = (v1p, v2p, v_first)
            wq = _bf(rw_w_rkvg[j])
            r, v, kk, g, bonus, kd0, kd1, b0, b1, lw0, lw1 = _rwkv_proj(
                xc, mod, vec, wq[0], wq[1], wq[2], wq[3], w1c, w2z, a1c, a2z, e, et, vres)
            if j == 0:
                v_first = v
            y0 = _rwkv_scan(False, r, kk, v, kd0, b0, lw0)
            y1 = _rwkv_scan(True, r, kk, v, kd1, b1, lw1)
            vec_o = _pad_rows(jnp.stack([rw_lnx_w[j], rw_lnx_b[j], post_g[i]]), 8)
            xc = _rwkv_out(y0, y1, bonus, g, xc, mod, vec_o, _bf(rw_w_out[j]), e, et)
        else:
            vec = _pad_rows(jnp.stack([pre_g[i], post_g[i]]), 8)
            q, k, v, g = _na_proj(xc, mod, vec, _bf(na_w_in[j]), na_b_in[j][None, :])
            o_lat = _na_attn(q, k, v, _na_bias_table(na_rpb[j]))
            o_ctx = None if last else _ctx_attn(q, k, v)
            xc = _na_out(o_lat, o_ctx, g, xc, mod, vec, _bf(na_w_out[j]))
    return xc if xc.shape[1] == T else xc[:, TM:]
```

```python
import functools
import math

import numpy as np
import jax
import jax.numpy as jnp
from jax import lax
from jax.experimental import pallas as pl
from jax.experimental.pallas import tpu as pltpu

F32 = jnp.float32
BF16 = jnp.bfloat16

D = 1024
HD = 64
NH = D // HD
PW = 2 * HD
NPAIR = D // PW
TM = 256
CH = 64
GRID_W = 64
WIN_H = 8
WIN_W = 16
RMS_EPS = 1e-6
LNX_EPS = 64e-5
NEG_INF = -1e30
EXP_M05 = math.exp(-0.5)
RQ = 4
VMEM_LIMIT = 56 * 1024 * 1024


def _bf(x):
    return x.astype(BF16)


def _dot(a, b):
    return jnp.dot(a, b, preferred_element_type=F32)


def _dot_nt(a, b):
    return lax.dot_general(a, b, (((1,), (1,)), ((), ())), preferred_element_type=F32)


def _dot_tn(a, b):
    return lax.dot_general(a, b, (((0,), (0,)), ((), ())), preferred_element_type=F32)


def _dot_split(x, e):
    hi = _bf(x)
    lo = _bf(x - hi.astype(F32))
    return _dot(hi, e) + _dot(lo, e)


def _seg_sum(x, e, et):
    return _dot_split(_dot_split(x, e), et)


def _sigmoid(x):
    return 1.0 / (1.0 + jnp.exp(-x))


def _prenorm(x, g, scale, shift):
    ms = jnp.mean(x * x, axis=-1, keepdims=True)
    return (x * lax.rsqrt(ms + RMS_EPS) * g) * (1.0 + scale) + shift


def _const_spec(shape):
    nd = len(shape)
    return pl.BlockSpec(shape, lambda *_: (0,) * nd, pipeline_mode=pl.Buffered(1))


def _params(sem):
    return pltpu.CompilerParams(dimension_semantics=sem, vmem_limit_bytes=VMEM_LIMIT)


def _adaln_kernel(s_ref, w_ref, b_ref, o_ref):
    s = s_ref[...]
    w = w_ref[0]
    hi = _bf(s)
    lo = _bf(s - hi.astype(F32))
    whi = _bf(w)
    wlo = _bf(w - whi.astype(F32))
    o_ref[0] = _dot(hi, whi) + _dot(lo, whi) + _dot(hi, wlo) + b_ref[0]


def _adaln(silu_rows, ada_w, ada_b):
    depth = ada_w.shape[0]
    nrow = silu_rows.shape[0]
    return pl.pallas_call(
        _adaln_kernel,
        grid=(depth, 3),
        in_specs=[pl.BlockSpec((nrow, D), lambda i, j: (0, 0)),
                  pl.BlockSpec((1, D, D), lambda i, j: (i, 0, j)),
                  pl.BlockSpec((1, 1, D), lambda i, j: (i, 0, j))],
        out_specs=pl.BlockSpec((1, nrow, D), lambda i, j: (i, 0, j)),
        out_shape=jax.ShapeDtypeStruct((depth, nrow, 3 * D), F32),
        compiler_params=_params(("parallel", "parallel")),
        name="adaln",
    )(silu_rows, ada_w, ada_b.reshape(depth, 1, 3 * D))


_V_PRE_G, _V_MU, _V_KK, _V_KA, _V_RK, _V_W0, _V_A0, _V_V0 = 0, 1, 7, 8, 9, 10, 12, 14


def _rwkv_proj_kernel(has_vres, nt, *refs):
    if has_vres:
        (x_ref, xp_ref, xn_ref, mod_ref, vec_ref, wr_ref, wk_ref, wv_ref, wg_ref, w1_ref, w2_ref,
         a1_ref, a2_ref, e_ref, et_ref, v1_ref, v2_ref, vf_ref,
         r_ref, v_ref, kk_ref, g_ref, bonus_ref, kd0_ref, kd1_ref, b0_ref, b1_ref, lw0_ref, lw1_ref) = refs
    else:
        (x_ref, xp_ref, xn_ref, mod_ref, vec_ref, wr_ref, wk_ref, wv_ref, wg_ref, w1_ref, w2_ref,
         a1_ref, a2_ref, e_ref, et_ref,
         r_ref, v_ref, kk_ref, g_ref, bonus_ref, kd0_ref, kd1_ref, b0_ref, b1_ref, lw0_ref, lw1_ref) = refs
    t = pl.program_id(1)
    shift = mod_ref[0, 0, :, 0:D]
    scale = mod_ref[0, 0, :, D:2 * D]
    g_pre = vec_ref[_V_PRE_G:_V_PRE_G + 1]

    h = _prenorm(x_ref[0], g_pre, scale, shift)
    hp = _prenorm(xp_ref[0], g_pre, scale, shift)[7:8]
    hn = _prenorm(xn_ref[0], g_pre, scale, shift)[0:1]
    hp = jnp.where(t >= 2, hp, 0.0)
    hn = jnp.where(jnp.logical_and(t >= 1, t < nt - 1), hn, 0.0)
    row = lax.broadcasted_iota(jnp.int32, (TM, 1), 0)
    h_m1 = jnp.where(row == 0, hp, pltpu.roll(h, 1, 0))
    h_p1 = jnp.where(row == TM - 1, hn, pltpu.roll(h, TM - 1, 0))
    xx = 0.5 * (h_m1 + h_p1) - h

    def lerp(n):
        return _bf(h + xx * vec_ref[_V_MU + n:_V_MU + n + 1])

    r = _dot(lerp(0), wr_ref[...])
    k = _dot(lerp(2), wk_ref[...])
    xv = lerp(3)
    v = _dot(xv, wv_ref[...])
    g = _dot(lerp(5), wg_ref[...])
    g = g * _sigmoid(g)
    if has_vres:
        lv = _dot(_bf(_dot(xv, v1_ref[...])), v2_ref[...])
        v = v + (vf_ref[0] - v) * _sigmoid(vec_ref[_V_V0:_V_V0 + 1] + lv)
    tw = _bf(jnp.tanh(_dot(lerp(1), w1_ref[...])))
    la = _bf(_dot(lerp(4), a1_ref[...]))

    e = e_ref[...]
    et = et_ref[...]
    kkr = k * vec_ref[_V_KK:_V_KK + 1]
    kk = kkr / jnp.maximum(jnp.sqrt(_seg_sum(kkr * kkr, e, et)), 1e-12)
    k_a = vec_ref[_V_KA:_V_KA + 1]

    ksum = None
    for d, (kd_ref, b_ref, lw_ref) in enumerate(((kd0_ref, b0_ref, lw0_ref), (kd1_ref, b1_ref, lw1_ref))):
        wl = vec_ref[_V_W0 + d:_V_W0 + d + 1] + _dot(tw, w2_ref[d])
        lw_ref[0] = -EXP_M05 * _sigmoid(wl)
        a = _sigmoid(vec_ref[_V_A0 + d:_V_A0 + d + 1] + _dot(la, a2_ref[d]))
        kd = k * (1.0 + (a - 1.0) * k_a)
        kd_ref[0] = kd
        b_ref[0] = kk * a
        ksum = kd if ksum is None else ksum + kd

    r_ref[0] = r
    v_ref[0] = v
    kk_ref[0] = kk
    g_ref[0] = g
    bonus_ref[0] = _seg_sum(r * ksum * vec_ref[_V_RK:_V_RK + 1], e, et) * v


def _rwkv_proj(xc, mod, vec, wr, wk, wv, wg, w1c, w2z, a1c, a2z, e, et, vres):
    B, TT, _ = xc.shape
    nt = TT // TM
    has_vres = vres is not None
    tile = pl.BlockSpec((1, TM, D), lambda b, t: (b, t, 0))
    in_specs = [
        tile,
        pl.BlockSpec((1, 8, D), lambda b, t: (b, jnp.maximum(t * (TM // 8) - 1, 0), 0)),
        pl.BlockSpec((1, 8, D), lambda b, t: (b, jnp.minimum((t + 1) * (TM // 8), TT // 8 - 1), 0)),
        pl.BlockSpec((1, 1, 1, 3 * D), lambda b, t: (b, jnp.minimum(t, 1), 0, 0)),
        _const_spec((16, D)),
        _const_spec((D, D)), _const_spec((D, D)), _const_spec((D, D)), _const_spec((D, D)),
        _const_spec((D, PW)), _const_spec((2, PW, D)), _const_spec((D, PW)), _const_spec((2, PW, D)),
        _const_spec((D, PW)), _const_spec((PW, D)),
    ]
    args = [xc, xc, xc, mod, vec, wr, wk, wv, wg, w1c, w2z, a1c, a2z, e, et]
    if has_vres:
        v1p, v2p, v_first = vres
        in_specs += [_const_spec((D, PW)), _const_spec((PW, D)), tile]
        args += [v1p, v2p, v_first]
    out = jax.ShapeDtypeStruct((B, TT, D), F32)
    return pl.pallas_call(
        functools.partial(_rwkv_proj_kernel, has_vres, nt),
        grid=(B, nt),
        in_specs=in_specs,
        out_specs=[tile] * 11,
        out_shape=[out] * 11,
        compiler_params=_params(("parallel", "parallel")),
        name="rwkv_proj",
    )(*args)


def _cumsum_rows(x, rev):
    row = lax.broadcasted_iota(jnp.int32, (CH, 1), 0)
    s = 1
    while s < CH:
        if rev:
            x = x + jnp.where(row < CH - s, pltpu.roll(x, CH - s, 0), 0.0)
        else:
            x = x + jnp.where(row >= s, pltpu.roll(x, s, 0), 0.0)
        s *= 2
    return x


def _tri_masks(rev):
    ri = lax.broadcasted_iota(jnp.int32, (PW, PW), 0)
    ci = lax.broadcasted_iota(jnp.int32, (PW, PW), 1)
    same = (ri >> 6) == (ci >> 6)
    rl = ri & (CH - 1)
    cl = ci & (CH - 1)
    if rev:
        return jnp.logical_and(same, cl > rl), jnp.logical_and(same, cl >= rl)
    return jnp.logical_and(same, cl < rl), jnp.logical_and(same, cl <= rl)


def _scan_kernel(*refs):
    in_refs = (refs[0:6], refs[6:12])
    y_refs = refs[12:14]
    st_ref = refs[14]

    @pl.when(pl.program_id(2) == 0)
    def _():
        st_ref[...] = jnp.zeros_like(st_ref)

    lane = lax.broadcasted_iota(jnp.int32, (1, PW), 1)
    m0 = (lane < HD).astype(F32)
    m1 = 1.0 - m0
    ri = lax.broadcasted_iota(jnp.int32, (PW, PW), 0)
    ci = lax.broadcasted_iota(jnp.int32, (PW, PW), 1)
    eye = (ri == ci).astype(F32)
    masks = (_tri_masks(False), _tri_masks(True))

    def stack(x):
        return jnp.concatenate([x * m0, x * m1], axis=0)

    nch = TM // CH
    chains = [(d, c) for d in (0, 1) for c in (range(nch - 1, -1, -1) if d else range(nch))]
    n = len(chains)

    a2, r2, b2, k2, v2, bh2, kh2, wend = [], [], [], [], [], [], [], []
    for d, c in chains:
        r_ref, kk_ref, v_ref, k_ref, b_ref, lw_ref = in_refs[d]
        sl = pl.ds(c * CH, CH)
        lw = lw_ref[0, sl, :]
        cum = _cumsum_rows(lw, bool(d))
        last = cum[0:1] if d else cum[CH - 1:CH]
        e_n = jnp.exp(-cum)
        e_h = jnp.exp(last - cum)
        b = b_ref[0, sl, :]
        k = k_ref[0, sl, :]
        a2.append(stack(-kk_ref[0, sl, :] * jnp.exp(cum - lw)))
        r2.append(stack(r_ref[0, sl, :] * jnp.exp(cum)))
        b2.append(stack(b * e_n))
        k2.append(stack(k * e_n))
        v2.append(stack(v_ref[0, sl, :]))
        bh2.append(stack(b * e_h))
        kh2.append(stack(k * e_h))
        wend.append(jnp.exp(last))

    m = [_dot_nt(_bf(jnp.concatenate([a2[i], r2[i]], axis=0)), _bf(jnp.concatenate([b2[i], k2[i]], axis=0)))
         for i in range(n)]
    m_ab = [jnp.where(masks[chains[i][0]][0], m[i][:PW, :PW], 0.0) for i in range(n)]
    m_ak = [_bf(jnp.where(masks[chains[i][0]][0], m[i][:PW, PW:], 0.0)) for i in range(n)]
    m_r = [_bf(jnp.where(jnp.concatenate([masks[chains[i][0]][1]] * 2, axis=1), m[i][PW:, :], 0.0)) for i in range(n)]

    q = [eye + x for x in m_ab]
    lb = [_bf(x) for x in m_ab]
    l = [_dot(x, x) for x in lb]
    for _ in range(4):
        lb = [_bf(x) for x in l]
        res = [_dot(lb[i], jnp.concatenate([lb[i], _bf(q[i])], axis=1)) for i in range(n)]
        l = [x[:, :PW] for x in res]
        q = [q[i] + res[i][:, PW:] for i in range(n)]
    tinv = [q[i] + _dot(_bf(l[i]), _bf(q[i])) for i in range(n)]

    mv = [_dot(m_ak[i], _bf(v2[i])) for i in range(n)]
    pu = [_dot(_bf(tinv[i]), _bf(jnp.concatenate([a2[i], mv[i]], axis=1))) for i in range(n)]
    rhs = [_bf(jnp.concatenate([jnp.concatenate([pu[i][:, PW:], pu[i][:, :PW]], axis=1),
                                jnp.concatenate([v2[i], jnp.zeros_like(v2[i])], axis=1)], axis=0)) for i in range(n)]
    yp = [_dot(m_r[i], rhs[i]) for i in range(n)]
    gh = [_dot_tn(_bf(jnp.concatenate([bh2[i], kh2[i]], axis=0)), rhs[i]) for i in range(n)]
    gmat = [_bf(gh[i][:, PW:] + eye * wend[i]) for i in range(n)]
    q2 = [_bf(r2[i] + yp[i][:, PW:]) for i in range(n)]

    st = [st_ref[0], st_ref[1]]
    for i, (d, c) in enumerate(chains):
        sb = _bf(st[d])
        y2 = _dot(q2[i], sb) + yp[i][:, :PW]
        st[d] = _dot(gmat[i], sb) + gh[i][:, :PW]
        y_refs[d][0, pl.ds(c * CH, CH), :] = y2[:CH] + y2[CH:]
    st_ref[0] = st[0]
    st_ref[1] = st[1]


def _rwkv_scan(r, kk, v, kd0, b0, lw0, kd1, b1, lw1):
    B, TT, _ = r.shape
    nt = TT // TM
    fwd = pl.BlockSpec((1, TM, PW), lambda b, p, i: (b, i, p))
    bwd = pl.BlockSpec((1, TM, PW), lambda b, p, i: (b, jnp.where(i == 0, 0, nt - i), p))
    out = jax.ShapeDtypeStruct((B, TT, D), F32)
    return pl.pallas_call(
        _scan_kernel,
        grid=(B, NPAIR, nt),
        in_specs=[fwd] * 6 + [bwd] * 6,
        out_specs=[fwd, bwd],
        out_shape=[out, out],
        scratch_shapes=[pltpu.VMEM((2, PW, PW), F32)],
        compiler_params=_params(("parallel", "parallel", "arbitrary")),
        name="rwkv_scan",
    )(r, kk, v, kd0, b0, lw0, r, kk, v, kd1, b1, lw1)


def _out_tail(o, w_ref, post_g, gate, x):
    out = _dot(_bf(o), w_ref[...])
    ms = jnp.mean(out * out, axis=-1, keepdims=True)
    return x + gate * (out * lax.rsqrt(ms + RMS_EPS) * post_g)


def _rwkv_out_kernel(y0_ref, y1_ref, bonus_ref, g_ref, x_ref, mod_ref, vec_ref, w_ref, e_ref, et_ref, o_ref):
    e = e_ref[...]
    et = et_ref[...]
    y = y0_ref[0] + y1_ref[0]
    mu = _seg_sum(y, e, et) * (1.0 / HD)
    yc = y - mu
    var = _seg_sum(yc * yc, e, et) * (1.0 / HD)
    yn = yc * lax.rsqrt(var + LNX_EPS) * vec_ref[0:1] + vec_ref[1:2]
    o = (yn + bonus_ref[0]) * g_ref[0]
    o_ref[0] = _out_tail(o, w_ref, vec_ref[2:3], mod_ref[0, 0, :, 2 * D:3 * D], x_ref[0])


def _rwkv_out(y0, y1, bonus, g, xc, mod, vec, w_out, e, et):
    B, TT, _ = xc.shape
    nt = TT // TM
    tile = pl.BlockSpec((1, TM, D), lambda b, t: (b, t, 0))
    return pl.pallas_call(
        _rwkv_out_kernel,
        grid=(B, nt),
        in_specs=[tile] * 5 + [
            pl.BlockSpec((1, 1, 1, 3 * D), lambda b, t: (b, jnp.minimum(t, 1), 0, 0)),
            _const_spec((8, D)), _const_spec((D, D)), _const_spec((D, PW)), _const_spec((PW, D))],
        out_specs=tile,
        out_shape=jax.ShapeDtypeStruct((B, TT, D), F32),
        compiler_params=_params(("parallel", "parallel")),
        name="rwkv_out",
    )(y0, y1, bonus, g, xc, mod, vec, w_out, e, et)


def _na_proj_kernel(x_ref, mod_ref, vec_ref, w_ref, bias_ref, q_ref, k_ref, v_ref, g_ref):
    shift = mod_ref[0, 0, :, 0:D]
    scale = mod_ref[0, 0, :, D:2 * D]
    h = _bf(_prenorm(x_ref[0], vec_ref[0:1], scale, shift))
    q = _dot(h, w_ref[:, 0:D]) + bias_ref[:, 0:D]
    q_ref[0] = _bf(q * (HD ** -0.5))
    k_ref[0] = _bf(_dot(h, w_ref[:, D:2 * D]) + bias_ref[:, D:2 * D])
    v_ref[0] = _bf(_dot(h, w_ref[:, 2 * D:3 * D]) + bias_ref[:, 2 * D:3 * D])
    g = _dot(h, w_ref[:, 3 * D:4 * D]) + bias_ref[:, 3 * D:4 * D]
    g_ref[0] = g * _sigmoid(g)


def _na_proj(xc, mod, vec, w_in, b_in):
    B, TT, _ = xc.shape
    nt = TT // TM
    tile = pl.BlockSpec((1, TM, D), lambda b, t: (b, t, 0))
    bf = jax.ShapeDtypeStruct((B, TT, D), BF16)
    return pl.pallas_call(
        _na_proj_kernel,
        grid=(B, nt),
        in_specs=[tile,
                  pl.BlockSpec((1, 1, 1, 3 * D), lambda b, t: (b, jnp.minimum(t, 1), 0, 0)),
                  _const_spec((8, D)), _const_spec((D, 4 * D)), _const_spec((1, 4 * D))],
        out_specs=[tile] * 4,
        out_shape=[bf, bf, bf, jax.ShapeDtypeStruct((B, TT, D), F32)],
        compiler_params=_params(("parallel", "parallel")),
        name="na_proj",
    )(xc, mod, vec, w_in, b_in)


def _softmax_pv(s_list, v_list):
    mx = None
    for s in s_list:
        m = jnp.max(s, axis=-1, keepdims=True)
        mx = m if mx is None else jnp.maximum(mx, m)
    den = None
    acc = None
    for s, v in zip(s_list, v_list):
        p = jnp.exp(s - mx)
        d = jnp.sum(p, axis=-1, keepdims=True)
        o = _dot(_bf(p), v)
        den = d if den is None else den + d
        acc = o if acc is None else acc + o
    return acc / den


def _na_attn_kernel(rows, q_ref, k_ref, v_ref, bias_ref, o_ref):
    lane = lax.broadcasted_iota(jnp.int32, (1, PW), 1)
    m0 = lane < HD
    kc = k_ref[0, 0:TM, :]
    vc = v_ref[0, 0:TM, :]
    step = pl.program_id(2)
    nkeys = WIN_H * GRID_W
    zero = jnp.zeros((), BF16)
    for rr in range(RQ):
        r = step * RQ + rr
        rs = jnp.clip(r - WIN_H // 2, 0, rows - WIN_H)
        q = q_ref[0, rr * GRID_W:(rr + 1) * GRID_W, :]
        q2 = jnp.concatenate([jnp.where(m0, q, zero), jnp.where(m0, zero, q)], axis=0)
        start = pl.multiple_of(TM + rs * GRID_W, GRID_W)
        kw = k_ref[0, pl.ds(start, nkeys), :]
        vw = v_ref[0, pl.ds(start, nkeys), :]
        s_win = _dot_nt(q2, kw) + bias_ref[0, r - rs]
        s_ctx = _dot_nt(q2, kc)
        o2 = _softmax_pv([s_win, s_ctx], [vw, vc])
        o_ref[0, rr * GRID_W:(rr + 1) * GRID_W, :] = _bf(jnp.where(m0, o2[:GRID_W], o2[GRID_W:]))


def _na_attn(q, k, v, bias):
    B, TT, _ = q.shape
    T = TT - TM
    rows = T // GRID_W
    qb = RQ * GRID_W
    kv = pl.BlockSpec((1, TT, PW), lambda b, p, s: (b, 0, p))
    return pl.pallas_call(
        functools.partial(_na_attn_kernel, rows),
        grid=(B, NPAIR, rows // RQ),
        in_specs=[pl.BlockSpec((1, qb, PW), lambda b, p, s: (b, TM // qb + s, p)), kv, kv,
                  pl.BlockSpec((1, WIN_H, PW, WIN_H * GRID_W), lambda b, p, s: (p, 0, 0, 0))],
        out_specs=pl.BlockSpec((1, qb, PW), lambda b, p, s: (b, s, p)),
        out_shape=jax.ShapeDtypeStruct((B, T, D), BF16),
        compiler_params=_params(("parallel", "parallel", "arbitrary")),
        name="na_attn",
    )(q, k, v, bias)


def _ctx_attn_kernel(q_ref, k_ref, v_ref, o_ref):
    lane = lax.broadcasted_iota(jnp.int32, (1, PW), 1)
    m0 = lane < HD
    zero = jnp.zeros((), BF16)
    q = q_ref[0]
    q2 = jnp.concatenate([jnp.where(m0, q, zero), jnp.where(m0, zero, q)], axis=0)
    kc = k_ref[0]
    o2 = _softmax_pv([_dot_nt(q2, kc)], [v_ref[0]])
    o_ref[0] = _bf(jnp.where(m0, o2[:TM], o2[TM:]))


def _ctx_attn(q, k, v):
    B = q.shape[0]
    blk = pl.BlockSpec((1, TM, PW), lambda b, p: (b, 0, p))
    return pl.pallas_call(
        _ctx_attn_kernel,
        grid=(B, NPAIR),
        in_specs=[blk, blk, blk],
        out_specs=blk,
        out_shape=jax.ShapeDtypeStruct((B, TM, D), BF16),
        compiler_params=_params(("parallel", "parallel")),
        name="ctx_attn",
    )(q, k, v)


def _na_out_kernel(with_ctx, *refs):
    if with_ctx:
        ol_ref, oc_ref, g_ref, x_ref, mod_ref, vec_ref, w_ref, o_ref = refs
        o = jnp.where(pl.program_id(1) == 0, oc_ref[0], ol_ref[0])
    else:
        ol_ref, g_ref, x_ref, mod_ref, vec_ref, w_ref, o_ref = refs
        o = ol_ref[0]
    o = o.astype(F32) * g_ref[0]
    o_ref[0] = _out_tail(o, w_ref, vec_ref[1:2], mod_ref[0, 0, :, 2 * D:3 * D], x_ref[0])


def _na_out(o_lat, o_ctx, g, xc, mod, vec, w_out):
    B, TT, _ = xc.shape
    nt = TT // TM
    with_ctx = o_ctx is not None
    consts = [_const_spec((8, D)), _const_spec((D, D))]
    if with_ctx:
        tile = pl.BlockSpec((1, TM, D), lambda b, t: (b, t, 0))
        in_specs = [pl.BlockSpec((1, TM, D), lambda b, t: (b, jnp.maximum(t - 1, 0), 0)),
                    pl.BlockSpec((1, TM, D), lambda b, t: (b, 0, 0)),
                    tile, tile,
                    pl.BlockSpec((1, 1, 1, 3 * D), lambda b, t: (b, jnp.minimum(t, 1), 0, 0))] + consts
        args = (o_lat, o_ctx, g, xc, mod, vec, w_out)
        grid, out_spec, out_rows = (B, nt), tile, TT
    else:
        lat = pl.BlockSpec((1, TM, D), lambda b, t: (b, t + 1, 0))
        out_spec = pl.BlockSpec((1, TM, D), lambda b, t: (b, t, 0))
        in_specs = [out_spec, lat, lat,
                    pl.BlockSpec((1, 1, 1, 3 * D), lambda b, t: (b, 1, 0, 0))] + consts
        args = (o_lat, g, xc, mod, vec, w_out)
        grid, out_rows = (B, nt - 1), TT - TM
    return pl.pallas_call(
        functools.partial(_na_out_kernel, with_ctx),
        grid=grid,
        in_specs=in_specs,
        out_specs=out_spec,
        out_shape=jax.ShapeDtypeStruct((B, out_rows, D), F32),
        compiler_params=_params(("parallel", "parallel")),
        name="na_out",
    )(*args)


def _na_bias_table(rpb):
    j = np.arange(GRID_W)
    win_start = np.clip(j - WIN_W // 2, 0, GRID_W - WIN_W)
    kcol = np.arange(GRID_W)
    valid = (kcol[None, :] >= win_start[:, None]) & (kcol[None, :] < win_start[:, None] + WIN_W)
    dc = np.clip(kcol[None, :] - j[:, None], -(WIN_W - 1), WIN_W - 1) + (WIN_W - 1)
    off = np.arange(WIN_H)
    i = np.arange(WIN_H)
    dr = i[None, :] - off[:, None] + (WIN_H - 1)
    tab = rpb[:, dr][:, :, :, dc]
    tab = jnp.where(jnp.asarray(valid)[None, None, None], tab, NEG_INF)
    tab = jnp.transpose(tab, (0, 1, 3, 2, 4)).reshape(NH, WIN_H, GRID_W, WIN_H * GRID_W)
    tab = tab.reshape(NPAIR, 2, WIN_H, GRID_W, WIN_H * GRID_W)
    return jnp.transpose(tab, (0, 2, 1, 3, 4)).reshape(NPAIR, WIN_H, PW, WIN_H * GRID_W).astype(F32)


def _pad_rows(m, rows):
    return jnp.pad(m, ((0, rows - m.shape[0]), (0, 0)))


def kernel(x, c, ctx, c_ctx, ada_w, ada_b, pre_g, post_g, rw_mu, rw_w_rkvg, rw_w0, rw_w1, rw_w2, rw_a0, rw_a1, rw_a2, rw_v0, rw_v1, rw_v2, rw_k_k, rw_k_a, rw_r_k, rw_lnx_w, rw_lnx_b, rw_w_out, na_w_in, na_b_in, na_rpb, na_w_out):
    B, T, _ = x.shape
    depth = ada_w.shape[0]
    assert ctx.shape[1] == TM and T % (RQ * GRID_W) == 0 and T % TM == 0 and T // GRID_W >= WIN_H

    seg = (np.arange(D)[:, None] // HD) == np.arange(PW)[None, :]
    e = jnp.asarray(seg, BF16)
    et = jnp.asarray(seg.T, BF16)

    cond = jnp.concatenate([c, c_ctx[None, :]], axis=0)
    nrow = -(-(B + 1) // 8) * 8
    cond = _pad_rows(cond * jax.nn.sigmoid(cond), nrow)
    mod_all = _adaln(cond, ada_w, ada_b)
    mod_ctx = jnp.broadcast_to(mod_all[:, B:B + 1], (depth, B, 3 * D))
    mod_all = jnp.stack([mod_ctx, mod_all[:, :B]], axis=2)[:, :, :, None, :]

    xc = jnp.concatenate([ctx, x], axis=1)
    v_first = None
    for i in range(depth):
        last = i == depth - 1
        j = i // 2
        mod = mod_all[i]
        if i % 2 == 0:
            zero = jnp.zeros((D,), F32)
            vec = jnp.stack([pre_g[i], *rw_mu[j], rw_k_k[j], rw_k_a[j], rw_r_k[j].reshape(D),
                             rw_w0[j, 0], rw_w0[j, 1], rw_a0[j, 0], rw_a0[j, 1],
                             rw_v0[j - 1] if j > 0 else zero, zero])
            lora = D // 16
            w1c = _bf(jnp.concatenate([rw_w1[j, 0], rw_w1[j, 1]], axis=1))
            a1c = _bf(jnp.concatenate([rw_a1[j, 0], rw_a1[j, 1]], axis=1))
            zl = jnp.zeros((lora, D), F32)
            w2z = _bf(jnp.stack([jnp.concatenate([rw_w2[j, 0], zl]), jnp.concatenate([zl, rw_w2[j, 1]])]))
            a2z = _bf(jnp.stack([jnp.concatenate([rw_a2[j, 0], zl]), jnp.concatenate([zl, rw_a2[j, 1]])]))
            vres = None
            if j > 0:
                v1p = _bf(jnp.pad(rw_v1[j - 1], ((0, 0), (0, PW - rw_v1.shape[-1]))))
                v2p = _bf(_pad_rows(rw_v2[j - 1], PW))
                vres = (v1p, v2p, v_first)
            wq = _bf(rw_w_rkvg[j])
            r, v, kk, g, bonus, kd0, kd1, b0, b1, lw0, lw1 = _rwkv_proj(
                xc, mod, vec, wq[0], wq[1], wq[2], wq[3], w1c, w2z, a1c, a2z, e, et, vres)
            if j == 0:
                v_first = v
            y0, y1 = _rwkv_scan(r, kk, v, kd0, b0, lw0, kd1, b1, lw1)
            vec_o = _pad_rows(jnp.stack([rw_lnx_w[j], rw_lnx_b[j], post_g[i]]), 8)
            xc = _rwkv_out(y0, y1, bonus, g, xc, mod, vec_o, _bf(rw_w_out[j]), e, et)
        else:
            vec = _pad_rows(jnp.stack([pre_g[i], post_g[i]]), 8)
            q, k, v, g = _na_proj(xc, mod, vec, _bf(na_w_in[j]), na_b_in[j][None, :])
            o_lat = _na_attn(q, k, v, _na_bias_table(na_rpb[j]))
            o_ctx = None if last else _ctx_attn(q, k, v)
            xc = _na_out(o_lat, o_ctx, g, xc, mod, vec, _bf(na_w_out[j]))
    return xc if xc.shape[1] == T else xc[:, TM:]
```

```python
import functools
import math

import numpy as np
import jax
import jax.numpy as jnp
from jax import lax
from jax.experimental import pallas as pl
from jax.experimental.pallas import tpu as pltpu

F32 = jnp.float32
BF16 = jnp.bfloat16

D = 1024
HD = 64
NH = D // HD
PW = 2 * HD
NPAIR = D // PW
TM = 256
CH = 64
GRID_W = 64
WIN_H = 8
WIN_W = 16
RMS_EPS = 1e-6
LNX_EPS = 64e-5
NEG_INF = -1e30
EXP_M05 = math.exp(-0.5)
RQ = 8
SCAN_LAG = 1
VMEM_LIMIT = 56 * 1024 * 1024


def _bf(x):
    return x.astype(BF16)


def _dot(a, b):
    return jnp.dot(a, b, preferred_element_type=F32)


def _dot_nt(a, b):
    return lax.dot_general(a, b, (((1,), (1,)), ((), ())), preferred_element_type=F32)


def _dot_tn(a, b):
    return lax.dot_general(a, b, (((0,), (0,)), ((), ())), preferred_element_type=F32)


def _dot_split(x, e):
    hi = _bf(x)
    lo = _bf(x - hi.astype(F32))
    return _dot(hi, e) + _dot(lo, e)


def _seg_sum(x, e, et):
    return _dot_split(_dot_split(x, e), et)


def _sigmoid(x):
    return 0.5 * jnp.tanh(0.5 * x) + 0.5


def _prenorm(x, g, scale, shift):
    ms = jnp.mean(x * x, axis=-1, keepdims=True)
    return (x * lax.rsqrt(ms + RMS_EPS) * g) * (1.0 + scale) + shift


def _const_spec(shape):
    nd = len(shape)
    return pl.BlockSpec(shape, lambda *_: (0,) * nd, pipeline_mode=pl.Buffered(1))


def _params(sem):
    return pltpu.CompilerParams(dimension_semantics=sem, vmem_limit_bytes=VMEM_LIMIT)


def _adaln_kernel(s_ref, w_ref, b_ref, o_ref):
    s = s_ref[...]
    w = w_ref[0]
    hi = _bf(s)
    lo = _bf(s - hi.astype(F32))
    whi = _bf(w)
    wlo = _bf(w - whi.astype(F32))
    o_ref[0] = _dot(hi, whi) + _dot(lo, whi) + _dot(hi, wlo) + b_ref[0]


def _adaln(silu_rows, ada_w, ada_b):
    depth = ada_w.shape[0]
    nrow = silu_rows.shape[0]
    return pl.pallas_call(
        _adaln_kernel,
        grid=(depth, 3),
        in_specs=[pl.BlockSpec((nrow, D), lambda i, j: (0, 0)),
                  pl.BlockSpec((1, D, D), lambda i, j: (i, 0, j)),
                  pl.BlockSpec((1, 1, D), lambda i, j: (i, 0, j))],
        out_specs=pl.BlockSpec((1, nrow, D), lambda i, j: (i, 0, j)),
        out_shape=jax.ShapeDtypeStruct((depth, nrow, 3 * D), F32),
        compiler_params=_params(("parallel", "parallel")),
        name="adaln",
    )(silu_rows, ada_w, ada_b.reshape(depth, 1, 3 * D))


_V_PRE_G, _V_MU, _V_KK, _V_KA, _V_RK, _V_W0, _V_A0, _V_V0 = 0, 1, 7, 8, 9, 10, 12, 14


def _rwkv_proj_kernel(has_vres, nt, *refs):
    if has_vres:
        (x_ref, xp_ref, xn_ref, mod_ref, vec_ref, wr_ref, wk_ref, wv_ref, wg_ref, w1_ref, w2_ref,
         a1_ref, a2_ref, e_ref, et_ref, v1_ref, v2_ref, vf_ref,
         r_ref, v_ref, kk_ref, g_ref, bonus_ref, kd0_ref, kd1_ref, b0_ref, b1_ref, lw0_ref, lw1_ref) = refs
    else:
        (x_ref, xp_ref, xn_ref, mod_ref, vec_ref, wr_ref, wk_ref, wv_ref, wg_ref, w1_ref, w2_ref,
         a1_ref, a2_ref, e_ref, et_ref,
         r_ref, v_ref, kk_ref, g_ref, bonus_ref, kd0_ref, kd1_ref, b0_ref, b1_ref, lw0_ref, lw1_ref) = refs
    t = pl.program_id(1)
    shift = mod_ref[0, 0, :, 0:D]
    scale = mod_ref[0, 0, :, D:2 * D]
    g_pre = vec_ref[_V_PRE_G:_V_PRE_G + 1]

    h = _prenorm(x_ref[0], g_pre, scale, shift)
    hp = _prenorm(xp_ref[0], g_pre, scale, shift)[7:8]
    hn = _prenorm(xn_ref[0], g_pre, scale, shift)[0:1]
    hp = jnp.where(t >= 2, hp, 0.0)
    hn = jnp.where(jnp.logical_and(t >= 1, t < nt - 1), hn, 0.0)
    row = lax.broadcasted_iota(jnp.int32, (TM, 1), 0)
    h_m1 = jnp.where(row == 0, hp, pltpu.roll(h, 1, 0))
    h_p1 = jnp.where(row == TM - 1, hn, pltpu.roll(h, TM - 1, 0))
    xx = 0.5 * (h_m1 + h_p1) - h

    def lerp(n):
        return _bf(h + xx * vec_ref[_V_MU + n:_V_MU + n + 1])

    r = _dot(lerp(0), wr_ref[...])
    k = _dot(lerp(2), wk_ref[...])
    xv = lerp(3)
    v = _dot(xv, wv_ref[...])
    g = _dot(lerp(5), wg_ref[...])
    g = g * _sigmoid(g)
    if has_vres:
        lv = _dot(_bf(_dot(xv, v1_ref[...])), v2_ref[...])
        v = v + (vf_ref[0] - v) * _sigmoid(vec_ref[_V_V0:_V_V0 + 1] + lv)
    tw = _bf(jnp.tanh(_dot(lerp(1), w1_ref[...])))
    la = _bf(_dot(lerp(4), a1_ref[...]))

    e = e_ref[...]
    et = et_ref[...]
    kkr = k * vec_ref[_V_KK:_V_KK + 1]
    kk = kkr / jnp.maximum(jnp.sqrt(_seg_sum(kkr * kkr, e, et)), 1e-12)
    k_a = vec_ref[_V_KA:_V_KA + 1]

    ksum = None
    for d, (kd_ref, b_ref, lw_ref) in enumerate(((kd0_ref, b0_ref, lw0_ref), (kd1_ref, b1_ref, lw1_ref))):
        wl = vec_ref[_V_W0 + d:_V_W0 + d + 1] + _dot(tw, w2_ref[d])
        lw_ref[0] = -EXP_M05 * _sigmoid(wl)
        a = _sigmoid(vec_ref[_V_A0 + d:_V_A0 + d + 1] + _dot(la, a2_ref[d]))
        kd = k * (1.0 + (a - 1.0) * k_a)
        kd_ref[0] = kd
        b_ref[0] = kk * a
        ksum = kd if ksum is None else ksum + kd

    r_ref[0] = r
    v_ref[0] = v
    kk_ref[0] = kk
    g_ref[0] = g
    bonus_ref[0] = _seg_sum(r * ksum * vec_ref[_V_RK:_V_RK + 1], e, et) * v


def _rwkv_proj(xc, mod, vec, wr, wk, wv, wg, w1c, w2z, a1c, a2z, e, et, vres):
    B, TT, _ = xc.shape
    nt = TT // TM
    has_vres = vres is not None
    tile = pl.BlockSpec((1, TM, D), lambda b, t: (b, t, 0))
    in_specs = [
        tile,
        pl.BlockSpec((1, 8, D), lambda b, t: (b, jnp.maximum(t * (TM // 8) - 1, 0), 0)),
        pl.BlockSpec((1, 8, D), lambda b, t: (b, jnp.minimum((t + 1) * (TM // 8), TT // 8 - 1), 0)),
        pl.BlockSpec((1, 1, 1, 3 * D), lambda b, t: (b, jnp.minimum(t, 1), 0, 0)),
        _const_spec((16, D)),
        _const_spec((D, D)), _const_spec((D, D)), _const_spec((D, D)), _const_spec((D, D)),
        _const_spec((D, PW)), _const_spec((2, PW, D)), _const_spec((D, PW)), _const_spec((2, PW, D)),
        _const_spec((D, PW)), _const_spec((PW, D)),
    ]
    args = [xc, xc, xc, mod, vec, wr, wk, wv, wg, w1c, w2z, a1c, a2z, e, et]
    if has_vres:
        v1p, v2p, v_first = vres
        in_specs += [_const_spec((D, PW)), _const_spec((PW, D)), tile]
        args += [v1p, v2p, v_first]
    out = jax.ShapeDtypeStruct((B, TT, D), F32)
    return pl.pallas_call(
        functools.partial(_rwkv_proj_kernel, has_vres, nt),
        grid=(B, nt),
        in_specs=in_specs,
        out_specs=[tile] * 11,
        out_shape=[out] * 11,
        compiler_params=_params(("parallel", "parallel")),
        name="rwkv_proj",
    )(*args)


def _cumsum_rows(x, rev):
    row = lax.broadcasted_iota(jnp.int32, (CH, 1), 0)
    s = 1
    while s < CH:
        if rev:
            x = x + jnp.where(row < CH - s, pltpu.roll(x, CH - s, 0), 0.0)
        else:
            x = x + jnp.where(row >= s, pltpu.roll(x, s, 0), 0.0)
        s *= 2
    return x


def _tri_masks(rev):
    ri = lax.broadcasted_iota(jnp.int32, (PW, PW), 0)
    ci = lax.broadcasted_iota(jnp.int32, (PW, PW), 1)
    same = (ri >> 6) == (ci >> 6)
    rl = ri & (CH - 1)
    cl = ci & (CH - 1)
    if rev:
        return jnp.logical_and(same, cl > rl), jnp.logical_and(same, cl >= rl)
    return jnp.logical_and(same, cl < rl), jnp.logical_and(same, cl <= rl)


def _scan_kernel(*refs):
    in_refs = (refs[0:6], refs[6:12])
    y_refs = refs[12:14]
    st_ref, q2_ref, g_ref, y0_ref, h_ref = refs[14:19]

    @pl.when(pl.program_id(2) == 0)
    def _():
        for ref in (st_ref, q2_ref, g_ref, y0_ref, h_ref):
            ref[...] = jnp.zeros_like(ref)

    nch = TM // CH
    chains = [(d, c) for d in (0, 1) for c in (range(nch - 1, -1, -1) if d else range(nch))]
    n = len(chains)

    st = [st_ref[0], st_ref[1]]

    def recurrence_round(k):
        for d in (0, 1):
            i = d * nch + k
            sb = _bf(st[d])
            y2 = _dot(q2_ref[i], sb) + y0_ref[i]
            st[d] = _dot(g_ref[i], sb) + h_ref[i]
            y_refs[d][0, pl.ds(chains[i][1] * CH, CH), :] = y2[:CH] + y2[CH:]

    lane = lax.broadcasted_iota(jnp.int32, (1, PW), 1)
    m0 = (lane < HD).astype(F32)
    m1 = 1.0 - m0
    ri = lax.broadcasted_iota(jnp.int32, (PW, PW), 0)
    ci = lax.broadcasted_iota(jnp.int32, (PW, PW), 1)
    eye = (ri == ci).astype(F32)
    masks = (_tri_masks(False), _tri_masks(True))

    def stack(x):
        return jnp.concatenate([x * m0, x * m1], axis=0)

    def prepare(g):
        g.update(a2=[], r2=[], v2=[], bh2=[], kh2=[], wend=[], m=[])
        for d, c in g["chains"]:
            r_ref, kk_ref, v_ref, k_ref, b_ref, lw_ref = in_refs[d]
            sl = pl.ds(c * CH, CH)
            lw = lw_ref[0, sl, :]
            cum = _cumsum_rows(lw, bool(d))
            last = cum[0:1] if d else cum[CH - 1:CH]
            e_n = jnp.exp(-cum)
            e_h = jnp.exp(last - cum)
            b = b_ref[0, sl, :]
            k = k_ref[0, sl, :]
            a2 = stack(-kk_ref[0, sl, :] * jnp.exp(cum - lw))
            r2 = stack(r_ref[0, sl, :] * jnp.exp(cum))
            g["a2"].append(a2)
            g["r2"].append(r2)
            g["v2"].append(stack(v_ref[0, sl, :]))
            g["bh2"].append(stack(b * e_h))
            g["kh2"].append(stack(k * e_h))
            g["wend"].append(jnp.exp(last))
            g["m"].append(_dot_nt(_bf(jnp.concatenate([a2, r2], axis=0)),
                                  _bf(jnp.concatenate([stack(b * e_n), stack(k * e_n)], axis=0))))

    def mask(g):
        strict, incl = masks[g["chains"][0][0]]
        m_ab = [jnp.where(strict, m[:PW, :PW], 0.0) for m in g["m"]]
        g["m_ak"] = [_bf(jnp.where(strict, m[:PW, PW:], 0.0)) for m in g["m"]]
        g["m_r"] = [_bf(jnp.where(jnp.concatenate([incl, incl], axis=1), m[PW:, :], 0.0)) for m in g["m"]]
        g["q"] = [eye + x for x in m_ab]
        lb = [_bf(x) for x in m_ab]
        g["l"] = [_dot(x, x) for x in lb]

    def double(g):
        lb = [_bf(x) for x in g["l"]]
        res = [_dot(x, jnp.concatenate([x, _bf(q)], axis=1)) for x, q in zip(lb, g["q"])]
        g["l"] = [x[:, :PW] for x in res]
        g["q"] = [q + x[:, PW:] for q, x in zip(g["q"], res)]

    def invert(g):
        g["tinv"] = [q + _dot(_bf(l), _bf(q)) for q, l in zip(g["q"], g["l"])]
        g["mv"] = [_dot(x, _bf(v)) for x, v in zip(g["m_ak"], g["v2"])]

    def solve(g):
        g["pu"] = [_dot(_bf(t), _bf(jnp.concatenate([a, mv], axis=1)))
                   for t, a, mv in zip(g["tinv"], g["a2"], g["mv"])]

    def combine(g):
        rhs = [_bf(jnp.concatenate([jnp.concatenate([pu[:, PW:], pu[:, :PW]], axis=1),
                                    jnp.concatenate([v, jnp.zeros_like(v)], axis=1)], axis=0))
               for pu, v in zip(g["pu"], g["v2"])]
        g["yp"] = [_dot(x, y) for x, y in zip(g["m_r"], rhs)]
        g["gh"] = [_dot_tn(_bf(jnp.concatenate([bh, kh], axis=0)), y)
                   for bh, kh, y in zip(g["bh2"], g["kh2"], rhs)]

    def publish(g):
        for j in range(nch):
            i = g["base"] + j
            g_ref[i] = _bf(g["gh"][j][:, PW:] + eye * g["wend"][j])
            q2_ref[i] = _bf(g["r2"][j] + g["yp"][j][:, PW:])
            y0_ref[i] = g["yp"][j][:, :PW]
            h_ref[i] = g["gh"][j][:, :PW]

    stages = [prepare, mask, double, double, double, double, invert, solve, combine, publish]
    groups = [dict(chains=chains[:nch], base=0), dict(chains=chains[nch:], base=nch)]
    for t in range(len(stages) + SCAN_LAG):
        if t < nch:
            recurrence_round(t)
        if t == nch:
            st_ref[0] = st[0]
            st_ref[1] = st[1]
        if t < len(stages):
            stages[t](groups[0])
        if 0 <= t - SCAN_LAG < len(stages):
            stages[t - SCAN_LAG](groups[1])


def _rwkv_scan(r, kk, v, kd0, b0, lw0, kd1, b1, lw1):
    B, TT, _ = r.shape
    nt = TT // TM

    def bwd_tile(j):
        return jnp.where(j == 0, 0, nt - j)

    fwd_in = pl.BlockSpec((1, TM, PW), lambda b, p, i: (b, jnp.minimum(i, nt - 1), p))
    bwd_in = pl.BlockSpec((1, TM, PW), lambda b, p, i: (b, bwd_tile(jnp.minimum(i, nt - 1)), p))
    fwd_out = pl.BlockSpec((1, TM, PW), lambda b, p, i: (b, jnp.maximum(i - 1, 0), p))
    bwd_out = pl.BlockSpec((1, TM, PW), lambda b, p, i: (b, bwd_tile(jnp.maximum(i - 1, 0)), p))
    out = jax.ShapeDtypeStruct((B, TT, D), F32)
    nchain = 2 * (TM // CH)
    return pl.pallas_call(
        _scan_kernel,
        grid=(B, NPAIR, nt + 1),
        in_specs=[fwd_in] * 6 + [bwd_in] * 6,
        out_specs=[fwd_out, bwd_out],
        out_shape=[out, out],
        scratch_shapes=[pltpu.VMEM((2, PW, PW), F32),
                        pltpu.VMEM((nchain, PW, PW), BF16), pltpu.VMEM((nchain, PW, PW), BF16),
                        pltpu.VMEM((nchain, PW, PW), F32), pltpu.VMEM((nchain, PW, PW), F32)],
        compiler_params=_params(("parallel", "parallel", "arbitrary")),
        name="rwkv_scan",
    )(r, kk, v, kd0, b0, lw0, r, kk, v, kd1, b1, lw1)


def _out_tail(o, w_ref, post_g, gate, x):
    out = _dot(_bf(o), w_ref[...])
    ms = jnp.mean(out * out, axis=-1, keepdims=True)
    return x + gate * (out * lax.rsqrt(ms + RMS_EPS) * post_g)


def _rwkv_out_kernel(y0_ref, y1_ref, bonus_ref, g_ref, x_ref, mod_ref, vec_ref, w_ref, e_ref, et_ref, o_ref):
    e = e_ref[...]
    et = et_ref[...]
    y = y0_ref[0] + y1_ref[0]
    mu = _seg_sum(y, e, et) * (1.0 / HD)
    yc = y - mu
    var = _seg_sum(yc * yc, e, et) * (1.0 / HD)
    yn = yc * lax.rsqrt(var + LNX_EPS) * vec_ref[0:1] + vec_ref[1:2]
    o = (yn + bonus_ref[0]) * g_ref[0]
    o_ref[0] = _out_tail(o, w_ref, vec_ref[2:3], mod_ref[0, 0, :, 2 * D:3 * D], x_ref[0])


def _rwkv_out(y0, y1, bonus, g, xc, mod, vec, w_out, e, et):
    B, TT, _ = xc.shape
    nt = TT // TM
    tile = pl.BlockSpec((1, TM, D), lambda b, t: (b, t, 0))
    return pl.pallas_call(
        _rwkv_out_kernel,
        grid=(B, nt),
        in_specs=[tile] * 5 + [
            pl.BlockSpec((1, 1, 1, 3 * D), lambda b, t: (b, jnp.minimum(t, 1), 0, 0)),
            _const_spec((8, D)), _const_spec((D, D)), _const_spec((D, PW)), _const_spec((PW, D))],
        out_specs=tile,
        out_shape=jax.ShapeDtypeStruct((B, TT, D), F32),
        compiler_params=_params(("parallel", "parallel")),
        name="rwkv_out",
    )(y0, y1, bonus, g, xc, mod, vec, w_out, e, et)


def _na_proj_kernel(x_ref, mod_ref, vec_ref, w_ref, bias_ref, q_ref, k_ref, v_ref, g_ref):
    shift = mod_ref[0, 0, :, 0:D]
    scale = mod_ref[0, 0, :, D:2 * D]
    h = _bf(_prenorm(x_ref[0], vec_ref[0:1], scale, shift))
    q = _dot(h, w_ref[:, 0:D]) + bias_ref[:, 0:D]
    q_ref[0] = _bf(q * (HD ** -0.5))
    k_ref[0] = _bf(_dot(h, w_ref[:, D:2 * D]) + bias_ref[:, D:2 * D])
    v_ref[0] = _bf(_dot(h, w_ref[:, 2 * D:3 * D]) + bias_ref[:, 2 * D:3 * D])
    g = _dot(h, w_ref[:, 3 * D:4 * D]) + bias_ref[:, 3 * D:4 * D]
    g_ref[0] = g * _sigmoid(g)


def _na_proj(xc, mod, vec, w_in, b_in):
    B, TT, _ = xc.shape
    nt = TT // TM
    tile = pl.BlockSpec((1, TM, D), lambda b, t: (b, t, 0))
    bf = jax.ShapeDtypeStruct((B, TT, D), BF16)
    return pl.pallas_call(
        _na_proj_kernel,
        grid=(B, nt),
        in_specs=[tile,
                  pl.BlockSpec((1, 1, 1, 3 * D), lambda b, t: (b, jnp.minimum(t, 1), 0, 0)),
                  _const_spec((8, D)), _const_spec((D, 4 * D)), _const_spec((1, 4 * D))],
        out_specs=[pl.BlockSpec((1, TM, D), lambda b, t: (b, jnp.where(t == 0, nt - 1, t - 1), 0))] + [tile] * 3,
        out_shape=[bf, bf, bf, jax.ShapeDtypeStruct((B, TT, D), F32)],
        compiler_params=_params(("parallel", "parallel")),
        name="na_proj",
    )(xc, mod, vec, w_in, b_in)


def _softmax_pv(s_list, v_list):
    mx = None
    for s in s_list:
        m = jnp.max(s, axis=-1, keepdims=True)
        mx = m if mx is None else jnp.maximum(mx, m)
    den = None
    acc = None
    for s, v in zip(s_list, v_list):
        p = jnp.exp(s - mx)
        d = jnp.sum(p, axis=-1, keepdims=True)
        o = _dot(_bf(p), v)
        den = d if den is None else den + d
        acc = o if acc is None else acc + o
    return acc / den


def _na_attn_kernel(rows, q_ref, k_ref, v_ref, bias_ref, o_ref):
    lane = lax.broadcasted_iota(jnp.int32, (1, PW), 1)
    m0 = lane < HD
    kc = k_ref[0, 0:TM, :]
    vc = v_ref[0, 0:TM, :]
    step = pl.program_id(2)
    nkeys = WIN_H * GRID_W
    zero = jnp.zeros((), BF16)
    q2, kw, vw, bias = [], [], [], []
    for rr in range(RQ):
        r = step * RQ + rr
        rs = jnp.clip(r - WIN_H // 2, 0, rows - WIN_H)
        q = q_ref[0, rr * GRID_W:(rr + 1) * GRID_W, :]
        q2.append(jnp.concatenate([jnp.where(m0, q, zero), jnp.where(m0, zero, q)], axis=0))
        start = pl.multiple_of(TM + rs * GRID_W, GRID_W)
        kw.append(k_ref[0, pl.ds(start, nkeys), :])
        vw.append(v_ref[0, pl.ds(start, nkeys), :])
        bias.append(bias_ref[0, r - rs])
    s_ctx = _dot_nt(jnp.concatenate(q2, axis=0), kc)
    s_win = [_dot_nt(q2[rr], kw[rr]) + bias[rr] for rr in range(RQ)]
    mx_ctx = jnp.max(s_ctx, axis=-1, keepdims=True)
    mx = [jnp.maximum(jnp.max(s_win[rr], axis=-1, keepdims=True), mx_ctx[rr * PW:(rr + 1) * PW])
          for rr in range(RQ)]
    p_win = [jnp.exp(s_win[rr] - mx[rr]) for rr in range(RQ)]
    p_ctx = jnp.exp(s_ctx - jnp.concatenate(mx, axis=0))
    den_ctx = jnp.sum(p_ctx, axis=-1, keepdims=True)
    o_ctx = _dot(_bf(p_ctx), vc)
    for rr in range(RQ):
        den = jnp.sum(p_win[rr], axis=-1, keepdims=True) + den_ctx[rr * PW:(rr + 1) * PW]
        o2 = (_dot(_bf(p_win[rr]), vw[rr]) + o_ctx[rr * PW:(rr + 1) * PW]) * (1.0 / den)
        o_ref[0, rr * GRID_W:(rr + 1) * GRID_W, :] = _bf(jnp.where(m0, o2[:GRID_W], o2[GRID_W:]))


def _na_attn(q, k, v, bias):
    B, TT, _ = q.shape
    T = TT - TM
    rows = T // GRID_W
    qb = RQ * GRID_W
    kv = pl.BlockSpec((1, TT, PW), lambda b, p, s: (b, 0, p))
    return pl.pallas_call(
        functools.partial(_na_attn_kernel, rows),
        grid=(B, NPAIR, rows // RQ),
        in_specs=[pl.BlockSpec((1, qb, PW), lambda b, p, s: (b, s, p)), kv, kv,
                  pl.BlockSpec((1, WIN_H, PW, WIN_H * GRID_W), lambda b, p, s: (p, 0, 0, 0))],
        out_specs=pl.BlockSpec((1, qb, PW), lambda b, p, s: (b, s, p)),
        out_shape=jax.ShapeDtypeStruct((B, T, D), BF16),
        compiler_params=_params(("parallel", "parallel", "arbitrary")),
        name="na_attn",
    )(q, k, v, bias)


def _ctx_attn_kernel(q_ref, k_ref, v_ref, o_ref):
    lane = lax.broadcasted_iota(jnp.int32, (1, PW), 1)
    m0 = lane < HD
    zero = jnp.zeros((), BF16)
    q = q_ref[0]
    q2 = jnp.concatenate([jnp.where(m0, q, zero), jnp.where(m0, zero, q)], axis=0)
    kc = k_ref[0]
    o2 = _softmax_pv([_dot_nt(q2, kc)], [v_ref[0]])
    o_ref[0] = _bf(jnp.where(m0, o2[:TM], o2[TM:]))


def _ctx_attn(q, k, v):
    B, TT, _ = q.shape
    blk = pl.BlockSpec((1, TM, PW), lambda b, p: (b, 0, p))
    q_ctx = pl.BlockSpec((1, TM, PW), lambda b, p: (b, TT // TM - 1, p))
    return pl.pallas_call(
        _ctx_attn_kernel,
        grid=(B, NPAIR),
        in_specs=[q_ctx, blk, blk],
        out_specs=blk,
        out_shape=jax.ShapeDtypeStruct((B, TM, D), BF16),
        compiler_params=_params(("parallel", "parallel")),
        name="ctx_attn",
    )(q, k, v)


def _na_out_kernel(with_ctx, *refs):
    if with_ctx:
        ol_ref, oc_ref, g_ref, x_ref, mod_ref, vec_ref, w_ref, o_ref = refs
        o = jnp.where(pl.program_id(1) == 0, oc_ref[0], ol_ref[0])
    else:
        ol_ref, g_ref, x_ref, mod_ref, vec_ref, w_ref, o_ref = refs
        o = ol_ref[0]
    o = o.astype(F32) * g_ref[0]
    o_ref[0] = _out_tail(o, w_ref, vec_ref[1:2], mod_ref[0, 0, :, 2 * D:3 * D], x_ref[0])


def _na_out(o_lat, o_ctx, g, xc, mod, vec, w_out):
    B, TT, _ = xc.shape
    nt = TT // TM
    with_ctx = o_ctx is not None
    consts = [_const_spec((8, D)), _const_spec((D, D))]
    if with_ctx:
        tile = pl.BlockSpec((1, TM, D), lambda b, t: (b, t, 0))
        in_specs = [pl.BlockSpec((1, TM, D), lambda b, t: (b, jnp.maximum(t - 1, 0), 0)),
                    pl.BlockSpec((1, TM, D), lambda b, t: (b, 0, 0)),
                    tile, tile,
                    pl.BlockSpec((1, 1, 1, 3 * D), lambda b, t: (b, jnp.minimum(t, 1), 0, 0))] + consts
        args = (o_lat, o_ctx, g, xc, mod, vec, w_out)
        grid, out_spec, out_rows = (B, nt), tile, TT
    else:
        lat = pl.BlockSpec((1, TM, D), lambda b, t: (b, t + 1, 0))
        out_spec = pl.BlockSpec((1, TM, D), lambda b, t: (b, t, 0))
        in_specs = [out_spec, lat, lat,
                    pl.BlockSpec((1, 1, 1, 3 * D), lambda b, t: (b, 1, 0, 0))] + consts
        args = (o_lat, g, xc, mod, vec, w_out)
        grid, out_rows = (B, nt - 1), TT - TM
    return pl.pallas_call(
        functools.partial(_na_out_kernel, with_ctx),
        grid=grid,
        in_specs=in_specs,
        out_specs=out_spec,
        out_shape=jax.ShapeDtypeStruct((B, out_rows, D), F32),
        compiler_params=_params(("parallel", "parallel")),
        name="na_out",
    )(*args)


def _na_bias_table(rpb):
    j = np.arange(GRID_W)
    win_start = np.clip(j - WIN_W // 2, 0, GRID_W - WIN_W)
    kcol = np.arange(GRID_W)
    valid = (kcol[None, :] >= win_start[:, None]) & (kcol[None, :] < win_start[:, None] + WIN_W)
    dc = np.clip(kcol[None, :] - j[:, None], -(WIN_W - 1), WIN_W - 1) + (WIN_W - 1)
    off = np.arange(WIN_H)
    i = np.arange(WIN_H)
    dr = i[None, :] - off[:, None] + (WIN_H - 1)
    tab = rpb[:, dr][:, :, :, dc]
    tab = jnp.where(jnp.asarray(valid)[None, None, None], tab, NEG_INF)
    tab = jnp.transpose(tab, (0, 1, 3, 2, 4)).reshape(NH, WIN_H, GRID_W, WIN_H * GRID_W)
    tab = tab.reshape(NPAIR, 2, WIN_H, GRID_W, WIN_H * GRID_W)
    return jnp.transpose(tab, (0, 2, 1, 3, 4)).reshape(NPAIR, WIN_H, PW, WIN_H * GRID_W).astype(F32)


def _pad_rows(m, rows):
    return jnp.pad(m, ((0, rows - m.shape[0]), (0, 0)))


def kernel(x, c, ctx, c_ctx, ada_w, ada_b, pre_g, post_g, rw_mu, rw_w_rkvg, rw_w0, rw_w1, rw_w2, rw_a0, rw_a1, rw_a2, rw_v0, rw_v1, rw_v2, rw_k_k, rw_k_a, rw_r_k, rw_lnx_w, rw_lnx_b, rw_w_out, na_w_in, na_b_in, na_rpb, na_w_out):
    B, T, _ = x.shape
    depth = ada_w.shape[0]
    assert ctx.shape[1] == TM and T % (RQ * GRID_W) == 0 and T % TM == 0 and T // GRID_W >= WIN_H

    seg = (np.arange(D)[:, None] // HD) == np.arange(PW)[None, :]
    e = jnp.asarray(seg, BF16)
    et = jnp.asarray(seg.T, BF16)

    cond = jnp.concatenate([c, c_ctx[None, :]], axis=0)
    nrow = -(-(B + 1) // 8) * 8
    cond = _pad_rows(cond * jax.nn.sigmoid(cond), nrow)
    mod_all = _adaln(cond, ada_w, ada_b)
    mod_ctx = jnp.broadcast_to(mod_all[:, B:B + 1], (depth, B, 3 * D))
    mod_all = jnp.stack([mod_ctx, mod_all[:, :B]], axis=2)[:, :, :, None, :]

    xc = jnp.concatenate([ctx, x], axis=1)
    v_first = None
    for i in range(depth):
        last = i == depth - 1
        j = i // 2
        mod = mod_all[i]
        if i % 2 == 0:
            zero = jnp.zeros((D,), F32)
            vec = jnp.stack([pre_g[i], *rw_mu[j], rw_k_k[j], rw_k_a[j], rw_r_k[j].reshape(D),
                             rw_w0[j, 0], rw_w0[j, 1], rw_a0[j, 0], rw_a0[j, 1],
                             rw_v0[j - 1] if j > 0 else zero, zero])
            lora = D // 16
            w1c = _bf(jnp.concatenate([rw_w1[j, 0], rw_w1[j, 1]], axis=1))
            a1c = _bf(jnp.concatenate([rw_a1[j, 0], rw_a1[j, 1]], axis=1))
            zl = jnp.zeros((lora, D), F32)
            w2z = _bf(jnp.stack([jnp.concatenate([rw_w2[j, 0], zl]), jnp.concatenate([zl, rw_w2[j, 1]])]))
            a2z = _bf(jnp.stack([jnp.concatenate([rw_a2[j, 0], zl]), jnp.concatenate([zl, rw_a2[j, 1]])]))
            vres = None
            if j > 0:
                v1p = _bf(jnp.pad(rw_v1[j - 1], ((0, 0), (0, PW - rw_v1.shape[-1]))))
                v2p = _bf(_pad_rows(rw_v2[j - 1], PW))
                vres = (v1p, v2p, v_first)
            wq = _bf(rw_w_rkvg[j])
            r, v, kk, g, bonus, kd0, kd1, b0, b1, lw0, lw1 = _rwkv_proj(
                xc, mod, vec, wq[0], wq[1], wq[2], wq[3], w1c, w2z, a1c, a2z, e, et, vres)
            if j == 0:
                v_first = v
            y0, y1 = _rwkv_scan(r, kk, v, kd0, b0, lw0, kd1, b1, lw1)
            vec_o = _pad_rows(jnp.stack([rw_lnx_w[j], rw_lnx_b[j], post_g[i]]), 8)
            xc = _rwkv_out(y0, y1, bonus, g, xc, mod, vec_o, _bf(rw_w_out[j]), e, et)
        else:
            vec = _pad_rows(jnp.stack([pre_g[i], post_g[i]]), 8)
            q, k, v, g = _na_proj(xc, mod, vec, _bf(na_w_in[j]), na_b_in[j][None, :])
            o_lat = _na_attn(q, k, v, _na_bias_table(na_rpb[j]))
            o_ctx = None if last else _ctx_attn(q, k, v)
            xc = _na_out(o_lat, o_ctx, g, xc, mod, vec, _bf(na_w_out[j]))
    return xc if xc.shape[1] == T else xc[:, TM:]
```

```python
import functools
import math

import numpy as np
import jax
import jax.numpy as jnp
from jax import lax
from jax.experimental import pallas as pl
from jax.experimental.pallas import tpu as pltpu

F32 = jnp.float32
BF16 = jnp.bfloat16

D = 1024
HD = 64
NH = D // HD
PW = 2 * HD
NPAIR = D // PW
TM = 256
CH = 64
GRID_W = 64
WIN_H = 8
WIN_W = 16
RMS_EPS = 1e-6
LNX_EPS = 64e-5
NEG_INF = -1e30
EXP_M05 = math.exp(-0.5)
RQ = 8
SCAN_PAIRS = 4
VMEM_LIMIT = 56 * 1024 * 1024


def _bf(x):
    return x.astype(BF16)


def _dot(a, b):
    return jnp.dot(a, b, preferred_element_type=F32)


def _dot_nt(a, b):
    return lax.dot_general(a, b, (((1,), (1,)), ((), ())), preferred_element_type=F32)


def _dot_tn(a, b):
    return lax.dot_general(a, b, (((0,), (0,)), ((), ())), preferred_element_type=F32)


def _dot_split(x, e):
    hi = _bf(x)
    lo = _bf(x - hi.astype(F32))
    return _dot(hi, e) + _dot(lo, e)


def _seg_reduce(x, e):
    return _dot(_bf(x), e)


def _seg_expand(c, et):
    return _dot_split(c, et)


def _sigmoid(x):
    return 0.5 * jnp.tanh(0.5 * x) + 0.5


def _prenorm(x, g, scale, shift):
    ms = jnp.mean(x * x, axis=-1, keepdims=True)
    return (x * lax.rsqrt(ms + RMS_EPS) * g) * (1.0 + scale) + shift


def _const_spec(shape):
    nd = len(shape)
    return pl.BlockSpec(shape, lambda *_: (0,) * nd, pipeline_mode=pl.Buffered(1))


def _params(sem):
    return pltpu.CompilerParams(dimension_semantics=sem, vmem_limit_bytes=VMEM_LIMIT)


def _adaln_kernel(s_ref, w_ref, b_ref, o_ref):
    s = s_ref[...]
    w = w_ref[0]
    hi = _bf(s)
    lo = _bf(s - hi.astype(F32))
    whi = _bf(w)
    wlo = _bf(w - whi.astype(F32))
    o_ref[0] = _dot(hi, whi) + _dot(lo, whi) + _dot(hi, wlo) + b_ref[0]


def _adaln(silu_rows, ada_w, ada_b):
    depth = ada_w.shape[0]
    nrow = silu_rows.shape[0]
    return pl.pallas_call(
        _adaln_kernel,
        grid=(depth, 3),
        in_specs=[pl.BlockSpec((nrow, D), lambda i, j: (0, 0)),
                  pl.BlockSpec((1, D, D), lambda i, j: (i, 0, j)),
                  pl.BlockSpec((1, 1, D), lambda i, j: (i, 0, j))],
        out_specs=pl.BlockSpec((1, nrow, D), lambda i, j: (i, 0, j)),
        out_shape=jax.ShapeDtypeStruct((depth, nrow, 3 * D), F32),
        compiler_params=_params(("parallel", "parallel")),
        name="adaln",
    )(silu_rows, ada_w, ada_b.reshape(depth, 1, 3 * D))


_V_PRE_G, _V_MU, _V_KK, _V_KA, _V_RK, _V_W0, _V_A0, _V_V0 = 0, 1, 7, 8, 9, 10, 12, 14


def _rwkv_proj_kernel(has_vres, nt, *refs):
    if has_vres:
        (x_ref, xp_ref, xn_ref, mod_ref, vec_ref, wr_ref, wk_ref, wv_ref, wg_ref, w1_ref, w2_ref,
         a1_ref, a2_ref, e_ref, et_ref, v1_ref, v2_ref, vf_ref,
         r_ref, v_ref, kk_ref, g_ref, bonus_ref, kd0_ref, kd1_ref, b0_ref, b1_ref, lw0_ref, lw1_ref) = refs
    else:
        (x_ref, xp_ref, xn_ref, mod_ref, vec_ref, wr_ref, wk_ref, wv_ref, wg_ref, w1_ref, w2_ref,
         a1_ref, a2_ref, e_ref, et_ref,
         r_ref, v_ref, kk_ref, g_ref, bonus_ref, kd0_ref, kd1_ref, b0_ref, b1_ref, lw0_ref, lw1_ref) = refs
    t = pl.program_id(1)
    shift = mod_ref[0, 0, :, 0:D]
    scale = mod_ref[0, 0, :, D:2 * D]
    g_pre = vec_ref[_V_PRE_G:_V_PRE_G + 1]

    h = _prenorm(x_ref[0], g_pre, scale, shift)
    hp = _prenorm(xp_ref[0], g_pre, scale, shift)[7:8]
    hn = _prenorm(xn_ref[0], g_pre, scale, shift)[0:1]
    hp = jnp.where(t >= 2, hp, 0.0)
    hn = jnp.where(jnp.logical_and(t >= 1, t < nt - 1), hn, 0.0)
    row = lax.broadcasted_iota(jnp.int32, (TM, 1), 0)
    h_m1 = jnp.where(row == 0, hp, pltpu.roll(h, 1, 0))
    h_p1 = jnp.where(row == TM - 1, hn, pltpu.roll(h, TM - 1, 0))
    xx = 0.5 * (h_m1 + h_p1) - h

    def lerp(n):
        return _bf(h + xx * vec_ref[_V_MU + n:_V_MU + n + 1])

    r = _dot(lerp(0), wr_ref[...])
    k = _dot(lerp(2), wk_ref[...])
    xv = lerp(3)
    v = _dot(xv, wv_ref[...])
    g = _dot(lerp(5), wg_ref[...])
    g = g * _sigmoid(g)
    if has_vres:
        lv = _dot(_bf(_dot(xv, v1_ref[...])), v2_ref[...])
        v = v + (vf_ref[0] - v) * _sigmoid(vec_ref[_V_V0:_V_V0 + 1] + lv)
    tw = _bf(jnp.tanh(_dot(lerp(1), w1_ref[...])))
    la = _bf(_dot(lerp(4), a1_ref[...]))

    e = e_ref[...]
    et = et_ref[...]
    kkr = k * vec_ref[_V_KK:_V_KK + 1]
    kk = kkr * _seg_expand(jnp.minimum(lax.rsqrt(_seg_reduce(kkr * kkr, e)), 1e12), et)
    k_a = vec_ref[_V_KA:_V_KA + 1]

    ksum = None
    for d, (kd_ref, b_ref, lw_ref) in enumerate(((kd0_ref, b0_ref, lw0_ref), (kd1_ref, b1_ref, lw1_ref))):
        wl = vec_ref[_V_W0 + d:_V_W0 + d + 1] + _dot(tw, w2_ref[d])
        lw_ref[0] = -EXP_M05 * _sigmoid(wl)
        a = _sigmoid(vec_ref[_V_A0 + d:_V_A0 + d + 1] + _dot(la, a2_ref[d]))
        kd = k * (1.0 + (a - 1.0) * k_a)
        kd_ref[0] = kd
        b_ref[0] = kk * a
        ksum = kd if ksum is None else ksum + kd

    r_ref[0] = r
    v_ref[0] = v
    kk_ref[0] = kk
    g_ref[0] = _bf(g)
    bonus_ref[0] = _bf(_seg_expand(_seg_reduce(r * ksum * vec_ref[_V_RK:_V_RK + 1], e), et) * v)


def _rwkv_proj(xc, mod, vec, wr, wk, wv, wg, w1c, w2z, a1c, a2z, e, et, vres):
    B, TT, _ = xc.shape
    nt = TT // TM
    has_vres = vres is not None
    tile = pl.BlockSpec((1, TM, D), lambda b, t: (b, t, 0))
    in_specs = [
        tile,
        pl.BlockSpec((1, 8, D), lambda b, t: (b, jnp.maximum(t * (TM // 8) - 1, 0), 0)),
        pl.BlockSpec((1, 8, D), lambda b, t: (b, jnp.minimum((t + 1) * (TM // 8), TT // 8 - 1), 0)),
        pl.BlockSpec((1, 1, 1, 3 * D), lambda b, t: (b, jnp.minimum(t, 1), 0, 0)),
        _const_spec((16, D)),
        _const_spec((D, D)), _const_spec((D, D)), _const_spec((D, D)), _const_spec((D, D)),
        _const_spec((D, PW)), _const_spec((2, PW, D)), _const_spec((D, PW)), _const_spec((2, PW, D)),
        _const_spec((D, PW)), _const_spec((PW, D)),
    ]
    args = [xc, xc, xc, mod, vec, wr, wk, wv, wg, w1c, w2z, a1c, a2z, e, et]
    if has_vres:
        v1p, v2p, v_first = vres
        in_specs += [_const_spec((D, PW)), _const_spec((PW, D)), tile]
        args += [v1p, v2p, v_first]
    out = jax.ShapeDtypeStruct((B, TT, D), F32)
    half = jax.ShapeDtypeStruct((B, TT, D), BF16)
    return pl.pallas_call(
        functools.partial(_rwkv_proj_kernel, has_vres, nt),
        grid=(B, nt),
        in_specs=in_specs,
        out_specs=[tile] * 11,
        out_shape=[out] * 3 + [half] * 2 + [out] * 6,
        compiler_params=_params(("parallel", "parallel")),
        name="rwkv_proj",
    )(*args)


def _cumsum_rows(x, rev):
    row = lax.broadcasted_iota(jnp.int32, (CH, 1), 0)
    s = 1
    while s < CH:
        if rev:
            x = x + jnp.where(row < CH - s, pltpu.roll(x, CH - s, 0), 0.0)
        else:
            x = x + jnp.where(row >= s, pltpu.roll(x, s, 0), 0.0)
        s *= 2
    return x


def _tri_masks(rev):
    ri = lax.broadcasted_iota(jnp.int32, (PW, PW), 0)
    ci = lax.broadcasted_iota(jnp.int32, (PW, PW), 1)
    same = (ri >> 6) == (ci >> 6)
    rl = ri & (CH - 1)
    cl = ci & (CH - 1)
    if rev:
        return jnp.logical_and(same, cl > rl), jnp.logical_and(same, cl >= rl)
    return jnp.logical_and(same, cl < rl), jnp.logical_and(same, cl <= rl)


def _scan_kernel(*refs):
    in_refs = (refs[0:6], refs[6:12])
    y_refs = refs[12:14]
    st_ref, q2_ref, g_ref, y0_ref, h_ref = refs[14:19]

    @pl.when(pl.program_id(2) == 0)
    def _():
        for ref in (st_ref, q2_ref, g_ref, y0_ref, h_ref):
            ref[...] = jnp.zeros_like(ref)

    nch = TM // CH
    chains = [(d, p, c) for d in (0, 1) for p in range(SCAN_PAIRS)
              for c in (range(nch - 1, -1, -1) if d else range(nch))]
    n = len(chains)
    nst = 2 * SCAN_PAIRS
    st = [st_ref[j] for j in range(nst)]

    def recurrence_round(k):
        for j in range(nst):
            i = j * nch + k
            d, p, c = chains[i]
            sb = _bf(st[j])
            y2 = _dot(q2_ref[i], sb) + y0_ref[i]
            st[j] = _dot(g_ref[i], sb) + h_ref[i]
            y_refs[d][0, pl.ds(c * CH, CH), p * PW:(p + 1) * PW] = _bf(y2[:CH] + y2[CH:])

    lane = lax.broadcasted_iota(jnp.int32, (1, PW), 1)
    m0 = (lane < HD).astype(F32)
    m1 = 1.0 - m0
    ri = lax.broadcasted_iota(jnp.int32, (PW, PW), 0)
    ci = lax.broadcasted_iota(jnp.int32, (PW, PW), 1)
    eye = (ri == ci).astype(F32)
    masks = (_tri_masks(False), _tri_masks(True))

    def stack(x):
        return jnp.concatenate([x * m0, x * m1], axis=0)

    g = {}

    def prepare(g):
        g.update(a2=[], r2=[], v2=[], bh2=[], kh2=[], wend=[], m=[])
        for d, p, c in chains:
            r_ref, kk_ref, v_ref, k_ref, b_ref, lw_ref = in_refs[d]
            sl = (0, pl.ds(c * CH, CH), slice(p * PW, (p + 1) * PW))
            lw = lw_ref[sl]
            cum = _cumsum_rows(lw, bool(d))
            last = cum[0:1] if d else cum[CH - 1:CH]
            e_n = jnp.exp(-cum)
            e_h = jnp.exp(last - cum)
            b = b_ref[sl]
            k = k_ref[sl]
            a2 = stack(-kk_ref[sl] * jnp.exp(cum - lw))
            r2 = stack(r_ref[sl] * jnp.exp(cum))
            g["a2"].append(a2)
            g["r2"].append(r2)
            g["v2"].append(stack(v_ref[sl]))
            g["bh2"].append(stack(b * e_h))
            g["kh2"].append(stack(k * e_h))
            g["wend"].append(jnp.exp(last))
            g["m"].append(_dot_nt(_bf(jnp.concatenate([a2, r2], axis=0)),
                                  _bf(jnp.concatenate([stack(b * e_n), stack(k * e_n)], axis=0))))

    def mask(g):
        strict = [masks[d][0] for d, _, _ in chains]
        incl = [jnp.concatenate([masks[d][1]] * 2, axis=1) for d, _, _ in chains]
        m_ab = [jnp.where(s, m[:PW, :PW], 0.0) for s, m in zip(strict, g["m"])]
        g["m_ak"] = [_bf(jnp.where(s, m[:PW, PW:], 0.0)) for s, m in zip(strict, g["m"])]
        g["m_r"] = [_bf(jnp.where(s, m[PW:, :], 0.0)) for s, m in zip(incl, g["m"])]
        g["q"] = [eye + x for x in m_ab]
        lb = [_bf(x) for x in m_ab]
        g["l"] = [_dot(x, x) for x in lb]

    def double(g):
        lb = [_bf(x) for x in g["l"]]
        res = [_dot(x, jnp.concatenate([x, _bf(q)], axis=1)) for x, q in zip(lb, g["q"])]
        g["l"] = [x[:, :PW] for x in res]
        g["q"] = [q + x[:, PW:] for q, x in zip(g["q"], res)]

    def invert(g):
        g["tinv"] = [q + _dot(_bf(l), _bf(q)) for q, l in zip(g["q"], g["l"])]
        g["mv"] = [_dot(x, _bf(v)) for x, v in zip(g["m_ak"], g["v2"])]

    def solve(g):
        g["pu"] = [_dot(_bf(t), _bf(jnp.concatenate([a, mv], axis=1)))
                   for t, a, mv in zip(g["tinv"], g["a2"], g["mv"])]

    def combine(g):
        rhs = [_bf(jnp.concatenate([jnp.concatenate([pu[:, PW:], pu[:, :PW]], axis=1),
                                    jnp.concatenate([v, jnp.zeros_like(v)], axis=1)], axis=0))
               for pu, v in zip(g["pu"], g["v2"])]
        g["yp"] = [_dot(x, y) for x, y in zip(g["m_r"], rhs)]
        g["gh"] = [_dot_tn(_bf(jnp.concatenate([bh, kh], axis=0)), y)
                   for bh, kh, y in zip(g["bh2"], g["kh2"], rhs)]

    def publish(g):
        for i in range(n):
            g_ref[i] = _bf(g["gh"][i][:, PW:] + eye * g["wend"][i])
            q2_ref[i] = _bf(g["r2"][i] + g["yp"][i][:, PW:])
            y0_ref[i] = g["yp"][i][:, :PW]
            h_ref[i] = g["gh"][i][:, :PW]

    stages = [prepare, mask, double, double, double, double, invert, solve, combine, publish]
    for t, stage in enumerate(stages):
        if t < nch:
            recurrence_round(t)
        if t == nch:
            for j in range(nst):
                st_ref[j] = st[j]
        stage(g)


def _rwkv_scan(r, kk, v, kd0, b0, lw0, kd1, b1, lw1):
    B, TT, _ = r.shape
    nt = TT // TM
    width = SCAN_PAIRS * PW

    def bwd_tile(j):
        return jnp.where(j == 0, 0, nt - j)

    fwd_in = pl.BlockSpec((1, TM, width), lambda b, p, i: (b, jnp.minimum(i, nt - 1), p))
    bwd_in = pl.BlockSpec((1, TM, width), lambda b, p, i: (b, bwd_tile(jnp.minimum(i, nt - 1)), p))
    fwd_out = pl.BlockSpec((1, TM, width), lambda b, p, i: (b, jnp.maximum(i - 1, 0), p))
    bwd_out = pl.BlockSpec((1, TM, width), lambda b, p, i: (b, bwd_tile(jnp.maximum(i - 1, 0)), p))
    out = jax.ShapeDtypeStruct((B, TT, D), BF16)
    n = 2 * SCAN_PAIRS * (TM // CH)
    return pl.pallas_call(
        _scan_kernel,
        grid=(B, NPAIR // SCAN_PAIRS, nt + 1),
        in_specs=[fwd_in] * 6 + [bwd_in] * 6,
        out_specs=[fwd_out, bwd_out],
        out_shape=[out, out],
        scratch_shapes=[pltpu.VMEM((2 * SCAN_PAIRS, PW, PW), F32),
                        pltpu.VMEM((n, PW, PW), BF16), pltpu.VMEM((n, PW, PW), BF16),
                        pltpu.VMEM((n, PW, PW), F32), pltpu.VMEM((n, PW, PW), F32)],
        compiler_params=_params(("parallel", "parallel", "arbitrary")),
        name="rwkv_scan",
    )(r, kk, v, kd0, b0, lw0, r, kk, v, kd1, b1, lw1)


def _out_tail(o, w_ref, post_g, gate, x):
    out = _dot(_bf(o), w_ref[...])
    ms = jnp.mean(out * out, axis=-1, keepdims=True)
    return x + gate * (out * lax.rsqrt(ms + RMS_EPS) * post_g)


def _rwkv_out_kernel(y0_ref, y1_ref, bonus_ref, g_ref, x_ref, mod_ref, vec_ref, w_ref, e_ref, et_ref, o_ref):
    e = e_ref[...]
    et = et_ref[...]
    y = y0_ref[0].astype(F32) + y1_ref[0].astype(F32)
    yc = y - _seg_expand(_seg_reduce(y, e) * (1.0 / HD), et)
    var = _seg_reduce(yc * yc, e) * (1.0 / HD)
    yn = yc * _seg_expand(lax.rsqrt(var + LNX_EPS), et) * vec_ref[0:1] + vec_ref[1:2]
    o = (yn + bonus_ref[0].astype(F32)) * g_ref[0].astype(F32)
    o_ref[0] = _out_tail(o, w_ref, vec_ref[2:3], mod_ref[0, 0, :, 2 * D:3 * D], x_ref[0])


def _rwkv_out(y0, y1, bonus, g, xc, mod, vec, w_out, e, et):
    B, TT, _ = xc.shape
    nt = TT // TM
    tile = pl.BlockSpec((1, TM, D), lambda b, t: (b, t, 0))
    return pl.pallas_call(
        _rwkv_out_kernel,
        grid=(B, nt),
        in_specs=[tile] * 5 + [
            pl.BlockSpec((1, 1, 1, 3 * D), lambda b, t: (b, jnp.minimum(t, 1), 0, 0)),
            _const_spec((8, D)), _const_spec((D, D)), _const_spec((D, PW)), _const_spec((PW, D))],
        out_specs=tile,
        out_shape=jax.ShapeDtypeStruct((B, TT, D), F32),
        compiler_params=_params(("parallel", "parallel")),
        name="rwkv_out",
    )(y0, y1, bonus, g, xc, mod, vec, w_out, e, et)


def _na_proj_kernel(x_ref, mod_ref, vec_ref, w_ref, bias_ref, q_ref, k_ref, v_ref, g_ref):
    shift = mod_ref[0, 0, :, 0:D]
    scale = mod_ref[0, 0, :, D:2 * D]
    h = _bf(_prenorm(x_ref[0], vec_ref[0:1], scale, shift))
    q = _dot(h, w_ref[:, 0:D]) + bias_ref[:, 0:D]
    q_ref[0] = _bf(q * (HD ** -0.5))
    k_ref[0] = _bf(_dot(h, w_ref[:, D:2 * D]) + bias_ref[:, D:2 * D])
    v_ref[0] = _bf(_dot(h, w_ref[:, 2 * D:3 * D]) + bias_ref[:, 2 * D:3 * D])
    g = _dot(h, w_ref[:, 3 * D:4 * D]) + bias_ref[:, 3 * D:4 * D]
    g_ref[0] = _bf(g * _sigmoid(g))


def _na_proj(xc, mod, vec, w_in, b_in):
    B, TT, _ = xc.shape
    nt = TT // TM
    tile = pl.BlockSpec((1, TM, D), lambda b, t: (b, t, 0))
    bf = jax.ShapeDtypeStruct((B, TT, D), BF16)
    return pl.pallas_call(
        _na_proj_kernel,
        grid=(B, nt),
        in_specs=[tile,
                  pl.BlockSpec((1, 1, 1, 3 * D), lambda b, t: (b, jnp.minimum(t, 1), 0, 0)),
                  _const_spec((8, D)), _const_spec((D, 4 * D)), _const_spec((1, 4 * D))],
        out_specs=[pl.BlockSpec((1, TM, D), lambda b, t: (b, jnp.where(t == 0, nt - 1, t - 1), 0))] + [tile] * 3,
        out_shape=[bf] * 4,
        compiler_params=_params(("parallel", "parallel")),
        name="na_proj",
    )(xc, mod, vec, w_in, b_in)


def _softmax_pv(s_list, v_list):
    mx = None
    for s in s_list:
        m = jnp.max(s, axis=-1, keepdims=True)
        mx = m if mx is None else jnp.maximum(mx, m)
    den = None
    acc = None
    for s, v in zip(s_list, v_list):
        p = jnp.exp(s - mx)
        d = jnp.sum(p, axis=-1, keepdims=True)
        o = _dot(_bf(p), v)
        den = d if den is None else den + d
        acc = o if acc is None else acc + o
    return acc / den


def _na_attn_kernel(rows, q_ref, k_ref, v_ref, bias_ref, o_ref):
    lane = lax.broadcasted_iota(jnp.int32, (1, PW), 1)
    m0 = lane < HD
    kc = k_ref[0, 0:TM, :]
    vc = v_ref[0, 0:TM, :]
    step = pl.program_id(2)
    nkeys = WIN_H * GRID_W
    zero = jnp.zeros((), BF16)
    q2, kw, vw, bias = [], [], [], []
    for rr in range(RQ):
        r = step * RQ + rr
        rs = jnp.clip(r - WIN_H // 2, 0, rows - WIN_H)
        q = q_ref[0, rr * GRID_W:(rr + 1) * GRID_W, :]
        q2.append(jnp.concatenate([jnp.where(m0, q, zero), jnp.where(m0, zero, q)], axis=0))
        start = pl.multiple_of(TM + rs * GRID_W, GRID_W)
        kw.append(k_ref[0, pl.ds(start, nkeys), :])
        vw.append(v_ref[0, pl.ds(start, nkeys), :])
        bias.append(bias_ref[0, r - rs])
    s_ctx = _dot_nt(jnp.concatenate(q2, axis=0), kc)
    s_win = [_dot_nt(q2[rr], kw[rr]) + bias[rr] for rr in range(RQ)]
    mx_ctx = jnp.max(s_ctx, axis=-1, keepdims=True)
    mx = [jnp.maximum(jnp.max(s_win[rr], axis=-1, keepdims=True), mx_ctx[rr * PW:(rr + 1) * PW])
          for rr in range(RQ)]
    p_win = [jnp.exp(s_win[rr] - mx[rr]) for rr in range(RQ)]
    p_ctx = jnp.exp(s_ctx - jnp.concatenate(mx, axis=0))
    den_ctx = jnp.sum(p_ctx, axis=-1, keepdims=True)
    o_ctx = _dot(_bf(p_ctx), vc)
    for rr in range(RQ):
        den = jnp.sum(p_win[rr], axis=-1, keepdims=True) + den_ctx[rr * PW:(rr + 1) * PW]
        o2 = (_dot(_bf(p_win[rr]), vw[rr]) + o_ctx[rr * PW:(rr + 1) * PW]) * (1.0 / den)
        o_ref[0, rr * GRID_W:(rr + 1) * GRID_W, :] = _bf(jnp.where(m0, o2[:GRID_W], o2[GRID_W:]))


def _na_attn(q, k, v, bias):
    B, TT, _ = q.shape
    T = TT - TM
    rows = T // GRID_W
    qb = RQ * GRID_W
    kv = pl.BlockSpec((1, TT, PW), lambda b, p, s: (b, 0, p))
    return pl.pallas_call(
        functools.partial(_na_attn_kernel, rows),
        grid=(B, NPAIR, rows // RQ),
        in_specs=[pl.BlockSpec((1, qb, PW), lambda b, p, s: (b, s, p)), kv, kv,
                  pl.BlockSpec((1, WIN_H, PW, WIN_H * GRID_W), lambda b, p, s: (p, 0, 0, 0))],
        out_specs=pl.BlockSpec((1, qb, PW), lambda b, p, s: (b, s, p)),
        out_shape=jax.ShapeDtypeStruct((B, T, D), BF16),
        compiler_params=_params(("parallel", "parallel", "arbitrary")),
        name="na_attn",
    )(q, k, v, bias)


def _ctx_attn_kernel(q_ref, k_ref, v_ref, o_ref):
    lane = lax.broadcasted_iota(jnp.int32, (1, PW), 1)
    m0 = lane < HD
    zero = jnp.zeros((), BF16)
    q = q_ref[0]
    q2 = jnp.concatenate([jnp.where(m0, q, zero), jnp.where(m0, zero, q)], axis=0)
    kc = k_ref[0]
    o2 = _softmax_pv([_dot_nt(q2, kc)], [v_ref[0]])
    o_ref[0] = _bf(jnp.where(m0, o2[:TM], o2[TM:]))


def _ctx_attn(q, k, v):
    B, TT, _ = q.shape
    blk = pl.BlockSpec((1, TM, PW), lambda b, p: (b, 0, p))
    q_ctx = pl.BlockSpec((1, TM, PW), lambda b, p: (b, TT // TM - 1, p))
    return pl.pallas_call(
        _ctx_attn_kernel,
        grid=(B, NPAIR),
        in_specs=[q_ctx, blk, blk],
        out_specs=blk,
        out_shape=jax.ShapeDtypeStruct((B, TM, D), BF16),
        compiler_params=_params(("parallel", "parallel")),
        name="ctx_attn",
    )(q, k, v)


def _na_out_kernel(with_ctx, *refs):
    if with_ctx:
        ol_ref, oc_ref, g_ref, x_ref, mod_ref, vec_ref, w_ref, o_ref = refs
        o = jnp.where(pl.program_id(1) == 0, oc_ref[0], ol_ref[0])
    else:
        ol_ref, g_ref, x_ref, mod_ref, vec_ref, w_ref, o_ref = refs
        o = ol_ref[0]
    o = o.astype(F32) * g_ref[0].astype(F32)
    o_ref[0] = _out_tail(o, w_ref, vec_ref[1:2], mod_ref[0, 0, :, 2 * D:3 * D], x_ref[0])


def _na_out(o_lat, o_ctx, g, xc, mod, vec, w_out):
    B, TT, _ = xc.shape
    nt = TT // TM
    with_ctx = o_ctx is not None
    consts = [_const_spec((8, D)), _const_spec((D, D))]
    if with_ctx:
        tile = pl.BlockSpec((1, TM, D), lambda b, t: (b, t, 0))
        in_specs = [pl.BlockSpec((1, TM, D), lambda b, t: (b, jnp.maximum(t - 1, 0), 0)),
                    pl.BlockSpec((1, TM, D), lambda b, t: (b, 0, 0)),
                    tile, tile,
                    pl.BlockSpec((1, 1, 1, 3 * D), lambda b, t: (b, jnp.minimum(t, 1), 0, 0))] + consts
        args = (o_lat, o_ctx, g, xc, mod, vec, w_out)
        grid, out_spec, out_rows = (B, nt), tile, TT
    else:
        lat = pl.BlockSpec((1, TM, D), lambda b, t: (b, t + 1, 0))
        out_spec = pl.BlockSpec((1, TM, D), lambda b, t: (b, t, 0))
        in_specs = [out_spec, lat, lat,
                    pl.BlockSpec((1, 1, 1, 3 * D), lambda b, t: (b, 1, 0, 0))] + consts
        args = (o_lat, g, xc, mod, vec, w_out)
        grid, out_rows = (B, nt - 1), TT - TM
    return pl.pallas_call(
        functools.partial(_na_out_kernel, with_ctx),
        grid=grid,
        in_specs=in_specs,
        out_specs=out_spec,
        out_shape=jax.ShapeDtypeStruct((B, out_rows, D), F32),
        compiler_params=_params(("parallel", "parallel")),
        name="na_out",
    )(*args)


def _na_bias_table(rpb):
    j = np.arange(GRID_W)
    win_start = np.clip(j - WIN_W // 2, 0, GRID_W - WIN_W)
    kcol = np.arange(GRID_W)
    valid = (kcol[None, :] >= win_start[:, None]) & (kcol[None, :] < win_start[:, None] + WIN_W)
    dc = np.clip(kcol[None, :] - j[:, None], -(WIN_W - 1), WIN_W - 1) + (WIN_W - 1)
    off = np.arange(WIN_H)
    i = np.arange(WIN_H)
    dr = i[None, :] - off[:, None] + (WIN_H - 1)
    tab = rpb[:, dr][:, :, :, dc]
    tab = jnp.where(jnp.asarray(valid)[None, None, None], tab, NEG_INF)
    tab = jnp.transpose(tab, (0, 1, 3, 2, 4)).reshape(NH, WIN_H, GRID_W, WIN_H * GRID_W)
    tab = tab.reshape(NPAIR, 2, WIN_H, GRID_W, WIN_H * GRID_W)
    return jnp.transpose(tab, (0, 2, 1, 3, 4)).reshape(NPAIR, WIN_H, PW, WIN_H * GRID_W).astype(F32)


def _pad_rows(m, rows):
    return jnp.pad(m, ((0, rows - m.shape[0]), (0, 0)))


def kernel(x, c, ctx, c_ctx, ada_w, ada_b, pre_g, post_g, rw_mu, rw_w_rkvg, rw_w0, rw_w1, rw_w2, rw_a0, rw_a1, rw_a2, rw_v0, rw_v1, rw_v2, rw_k_k, rw_k_a, rw_r_k, rw_lnx_w, rw_lnx_b, rw_w_out, na_w_in, na_b_in, na_rpb, na_w_out):
    B, T, _ = x.shape
    depth = ada_w.shape[0]
    assert ctx.shape[1] == TM and T % (RQ * GRID_W) == 0 and T % TM == 0 and T // GRID_W >= WIN_H

    seg = (np.arange(D)[:, None] // HD) == np.arange(PW)[None, :]
    e = jnp.asarray(seg, BF16)
    et = jnp.asarray(seg.T, BF16)

    cond = jnp.concatenate([c, c_ctx[None, :]], axis=0)
    nrow = -(-(B + 1) // 8) * 8
    cond = _pad_rows(cond * jax.nn.sigmoid(cond), nrow)
    mod_all = _adaln(cond, ada_w, ada_b)
    mod_ctx = jnp.broadcast_to(mod_all[:, B:B + 1], (depth, B, 3 * D))
    mod_all = jnp.stack([mod_ctx, mod_all[:, :B]], axis=2)[:, :, :, None, :]

    xc = jnp.concatenate([ctx, x], axis=1)
    v_first = None
    for i in range(depth):
        last = i == depth - 1
        j = i // 2
        mod = mod_all[i]
        if i % 2 == 0:
            zero = jnp.zeros((D,), F32)
            vec = jnp.stack([pre_g[i], *rw_mu[j], rw_k_k[j], rw_k_a[j], rw_r_k[j].reshape(D),
                             rw_w0[j, 0], rw_w0[j, 1], rw_a0[j, 0], rw_a0[j, 1],
                             rw_v0[j - 1] if j > 0 else zero, zero])
            lora = D // 16
            w1c = _bf(jnp.concatenate([rw_w1[j, 0], rw_w1[j, 1]], axis=1))
            a1c = _bf(jnp.concatenate([rw_a1[j, 0], rw_a1[j, 1]], axis=1))
            zl = jnp.zeros((lora, D), F32)
            w2z = _bf(jnp.stack([jnp.concatenate([rw_w2[j, 0], zl]), jnp.concatenate([zl, rw_w2[j, 1]])]))
            a2z = _bf(jnp.stack([jnp.concatenate([rw_a2[j, 0], zl]), jnp.concatenate([zl, rw_a2[j, 1]])]))
            vres = None
            if j > 0:
                v1p = _bf(jnp.pad(rw_v1[j - 1], ((0, 0), (0, PW - rw_v1.shape[-1]))))
                v2p = _bf(_pad_rows(rw_v2[j - 1], PW))
                vres = (v1p, v2p, v_first)
            wq = _bf(rw_w_rkvg[j])
            r, v, kk, g, bonus, kd0, kd1, b0, b1, lw0, lw1 = _rwkv_proj(
                xc, mod, vec, wq[0], wq[1], wq[2], wq[3], w1c, w2z, a1c, a2z, e, et, vres)
            if j == 0:
                v_first = v
            y0, y1 = _rwkv_scan(r, kk, v, kd0, b0, lw0, kd1, b1, lw1)
            vec_o = _pad_rows(jnp.stack([rw_lnx_w[j], rw_lnx_b[j], post_g[i]]), 8)
            xc = _rwkv_out(y0, y1, bonus, g, xc, mod, vec_o, _bf(rw_w_out[j]), e, et)
        else:
            vec = _pad_rows(jnp.stack([pre_g[i], post_g[i]]), 8)
            q, k, v, g = _na_proj(xc, mod, vec, _bf(na_w_in[j]), na_b_in[j][None, :])
            o_lat = _na_attn(q, k, v, _na_bias_table(na_rpb[j]))
            o_ctx = None if last else _ctx_attn(q, k, v)
            xc = _na_out(o_lat, o_ctx, g, xc, mod, vec, _bf(na_w_out[j]))
    return xc if xc.shape[1] == T else xc[:, TM:]
```

```python
import functools
import math

import numpy as np
import jax
import jax.numpy as jnp
from jax import lax
from jax.experimental import pallas as pl
from jax.experimental.pallas import tpu as pltpu

F32 = jnp.float32
BF16 = jnp.bfloat16

D = 1024
HD = 64
NH = D // HD
PW = 2 * HD
NPAIR = D // PW
TM = 256
CH = 64
GRID_W = 64
WIN_H = 8
WIN_W = 16
RMS_EPS = 1e-6
LNX_EPS = 64e-5
NEG_INF = -1e30
EXP_M05 = math.exp(-0.5)
RQ = 8
SCAN_PAIRS = 4
ATT_PAIRS = 2
VMEM_LIMIT = 56 * 1024 * 1024


def _bf(x):
    return x.astype(BF16)


def _dot(a, b):
    return jnp.dot(a, b, preferred_element_type=F32)


def _dot_nt(a, b):
    return lax.dot_general(a, b, (((1,), (1,)), ((), ())), preferred_element_type=F32)


def _dot_tn(a, b):
    return lax.dot_general(a, b, (((0,), (0,)), ((), ())), preferred_element_type=F32)


def _dot_split(x, e):
    hi = _bf(x)
    lo = _bf(x - hi.astype(F32))
    return _dot(hi, e) + _dot(lo, e)


def _seg_reduce(x, e, two_pass=False):
    return _dot_split(x, e) if two_pass else _dot(_bf(x), e)


def _seg_expand(c, et):
    return _dot_split(c, et)


def _sigmoid(x):
    return 0.5 * jnp.tanh(0.5 * x) + 0.5


def _prenorm(x, g, scale, shift):
    ms = jnp.mean(x * x, axis=-1, keepdims=True)
    return (x * lax.rsqrt(ms + RMS_EPS) * g) * (1.0 + scale) + shift


def _const_spec(shape):
    nd = len(shape)
    return pl.BlockSpec(shape, lambda *_: (0,) * nd, pipeline_mode=pl.Buffered(1))


def _params(sem):
    return pltpu.CompilerParams(dimension_semantics=sem, vmem_limit_bytes=VMEM_LIMIT)


def _adaln_kernel(s_ref, w_ref, b_ref, o_ref):
    s = s_ref[...]
    w = w_ref[0]
    hi = _bf(s)
    lo = _bf(s - hi.astype(F32))
    whi = _bf(w)
    wlo = _bf(w - whi.astype(F32))
    o_ref[0] = _dot(hi, whi) + _dot(lo, whi) + _dot(hi, wlo) + b_ref[0]


def _adaln(silu_rows, ada_w, ada_b):
    depth = ada_w.shape[0]
    nrow = silu_rows.shape[0]
    return pl.pallas_call(
        _adaln_kernel,
        grid=(depth, 3),
        in_specs=[pl.BlockSpec((nrow, D), lambda i, j: (0, 0)),
                  pl.BlockSpec((1, D, D), lambda i, j: (i, 0, j)),
                  pl.BlockSpec((1, 1, D), lambda i, j: (i, 0, j))],
        out_specs=pl.BlockSpec((1, nrow, D), lambda i, j: (i, 0, j)),
        out_shape=jax.ShapeDtypeStruct((depth, nrow, 3 * D), F32),
        compiler_params=_params(("parallel", "parallel")),
        name="adaln",
    )(silu_rows, ada_w, ada_b.reshape(depth, 1, 3 * D))


_V_PRE_G, _V_MU, _V_KK, _V_KA, _V_RK, _V_W0, _V_A0, _V_V0 = 0, 1, 7, 8, 9, 10, 12, 14


def _rwkv_proj_kernel(has_vres, nt, *refs):
    if has_vres:
        (x_ref, xp_ref, xn_ref, mod_ref, vec_ref, wr_ref, wk_ref, wv_ref, wg_ref, w1_ref, w2_ref,
         a1_ref, a2_ref, e_ref, et_ref, v1_ref, v2_ref, vf_ref,
         r_ref, v_ref, kk_ref, g_ref, bonus_ref, kd0_ref, kd1_ref, b0_ref, b1_ref, lw0_ref, lw1_ref) = refs
    else:
        (x_ref, xp_ref, xn_ref, mod_ref, vec_ref, wr_ref, wk_ref, wv_ref, wg_ref, w1_ref, w2_ref,
         a1_ref, a2_ref, e_ref, et_ref,
         r_ref, v_ref, kk_ref, g_ref, bonus_ref, kd0_ref, kd1_ref, b0_ref, b1_ref, lw0_ref, lw1_ref) = refs
    t = pl.program_id(1)
    shift = mod_ref[0, 0, :, 0:D]
    scale = mod_ref[0, 0, :, D:2 * D]
    g_pre = vec_ref[_V_PRE_G:_V_PRE_G + 1]

    h = _prenorm(x_ref[0], g_pre, scale, shift)
    hp = _prenorm(xp_ref[0], g_pre, scale, shift)[7:8]
    hn = _prenorm(xn_ref[0], g_pre, scale, shift)[0:1]
    hp = jnp.where(t >= 2, hp, 0.0)
    hn = jnp.where(jnp.logical_and(t >= 1, t < nt - 1), hn, 0.0)
    row = lax.broadcasted_iota(jnp.int32, (TM, 1), 0)
    h_m1 = jnp.where(row == 0, hp, pltpu.roll(h, 1, 0))
    h_p1 = jnp.where(row == TM - 1, hn, pltpu.roll(h, TM - 1, 0))
    xx = 0.5 * (h_m1 + h_p1) - h

    def lerp(n):
        return _bf(h + xx * vec_ref[_V_MU + n:_V_MU + n + 1])

    r = _dot(lerp(0), wr_ref[...])
    k = _dot(lerp(2), wk_ref[...])
    xv = lerp(3)
    v = _dot(xv, wv_ref[...])
    g = _dot(lerp(5), wg_ref[...])
    g = g * _sigmoid(g)
    if has_vres:
        lv = _dot(_bf(_dot(xv, v1_ref[...])), v2_ref[...])
        v = v + (vf_ref[0] - v) * _sigmoid(vec_ref[_V_V0:_V_V0 + 1] + lv)
    tw = _bf(jnp.tanh(_dot(lerp(1), w1_ref[...])))
    la = _bf(_dot(lerp(4), a1_ref[...]))

    e = e_ref[...]
    et = et_ref[...]
    kkr = k * vec_ref[_V_KK:_V_KK + 1]
    kk = kkr * _seg_expand(jnp.minimum(lax.rsqrt(_seg_reduce(kkr * kkr, e)), 1e12), et)
    k_a = vec_ref[_V_KA:_V_KA + 1]

    ksum = None
    for d, (kd_ref, b_ref, lw_ref) in enumerate(((kd0_ref, b0_ref, lw0_ref), (kd1_ref, b1_ref, lw1_ref))):
        wl = vec_ref[_V_W0 + d:_V_W0 + d + 1] + _dot(tw, w2_ref[d])
        lw_ref[0] = -EXP_M05 * _sigmoid(wl)
        a = _sigmoid(vec_ref[_V_A0 + d:_V_A0 + d + 1] + _dot(la, a2_ref[d]))
        kd = k * (1.0 + (a - 1.0) * k_a)
        kd_ref[0] = kd
        b_ref[0] = kk * a
        ksum = kd if ksum is None else ksum + kd

    r_ref[0] = r
    v_ref[0] = v
    kk_ref[0] = kk
    g_ref[0] = _bf(g)
    bonus_ref[0] = _bf(_seg_expand(_seg_reduce(r * ksum * vec_ref[_V_RK:_V_RK + 1], e), et) * v)


def _rwkv_proj(xc, mod, vec, wr, wk, wv, wg, w1c, w2z, a1c, a2z, e, et, vres):
    B, TT, _ = xc.shape
    nt = TT // TM
    has_vres = vres is not None
    tile = pl.BlockSpec((1, TM, D), lambda b, t: (b, t, 0))
    in_specs = [
        tile,
        pl.BlockSpec((1, 8, D), lambda b, t: (b, jnp.maximum(t * (TM // 8) - 1, 0), 0)),
        pl.BlockSpec((1, 8, D), lambda b, t: (b, jnp.minimum((t + 1) * (TM // 8), TT // 8 - 1), 0)),
        pl.BlockSpec((1, 1, 1, 3 * D), lambda b, t: (b, jnp.minimum(t, 1), 0, 0)),
        _const_spec((16, D)),
        _const_spec((D, D)), _const_spec((D, D)), _const_spec((D, D)), _const_spec((D, D)),
        _const_spec((D, PW)), _const_spec((2, PW, D)), _const_spec((D, PW)), _const_spec((2, PW, D)),
        _const_spec((D, PW)), _const_spec((PW, D)),
    ]
    args = [xc, xc, xc, mod, vec, wr, wk, wv, wg, w1c, w2z, a1c, a2z, e, et]
    if has_vres:
        v1p, v2p, v_first = vres
        in_specs += [_const_spec((D, PW)), _const_spec((PW, D)), tile]
        args += [v1p, v2p, v_first]
    out = jax.ShapeDtypeStruct((B, TT, D), F32)
    half = jax.ShapeDtypeStruct((B, TT, D), BF16)
    return pl.pallas_call(
        functools.partial(_rwkv_proj_kernel, has_vres, nt),
        grid=(B, nt),
        in_specs=in_specs,
        out_specs=[tile] * 11,
        out_shape=[out] * 3 + [half] * 2 + [out] * 6,
        compiler_params=_params(("parallel", "parallel")),
        name="rwkv_proj",
    )(*args)


def _cumsum_rows(x, rev):
    row = lax.broadcasted_iota(jnp.int32, (CH, 1), 0)
    s = 1
    while s < CH:
        if rev:
            x = x + jnp.where(row < CH - s, pltpu.roll(x, CH - s, 0), 0.0)
        else:
            x = x + jnp.where(row >= s, pltpu.roll(x, s, 0), 0.0)
        s *= 2
    return x


def _tri_masks(rev):
    ri = lax.broadcasted_iota(jnp.int32, (PW, PW), 0)
    ci = lax.broadcasted_iota(jnp.int32, (PW, PW), 1)
    same = (ri >> 6) == (ci >> 6)
    rl = ri & (CH - 1)
    cl = ci & (CH - 1)
    if rev:
        return jnp.logical_and(same, cl > rl), jnp.logical_and(same, cl >= rl)
    return jnp.logical_and(same, cl < rl), jnp.logical_and(same, cl <= rl)


def _scan_kernel(*refs):
    in_refs = (refs[0:6], refs[6:12])
    y_refs = refs[12:14]
    st_ref, q2_ref, g_ref, y0_ref, h_ref = refs[14:19]

    @pl.when(pl.program_id(2) == 0)
    def _():
        for ref in (st_ref, q2_ref, g_ref, y0_ref, h_ref):
            ref[...] = jnp.zeros_like(ref)

    nch = TM // CH
    chains = [(d, p, c) for d in (0, 1) for p in range(SCAN_PAIRS)
              for c in (range(nch - 1, -1, -1) if d else range(nch))]
    n = len(chains)
    nst = 2 * SCAN_PAIRS
    st = [st_ref[j] for j in range(nst)]

    def recurrence_round(k):
        for j in range(nst):
            i = j * nch + k
            d, p, c = chains[i]
            sb = _bf(st[j])
            y2 = _dot(q2_ref[i], sb) + y0_ref[i]
            st[j] = _dot(g_ref[i], sb) + h_ref[i]
            y_refs[d][0, pl.ds(c * CH, CH), p * PW:(p + 1) * PW] = _bf(y2[:CH] + y2[CH:])

    lane = lax.broadcasted_iota(jnp.int32, (1, PW), 1)
    m0 = (lane < HD).astype(F32)
    m1 = 1.0 - m0
    ri = lax.broadcasted_iota(jnp.int32, (PW, PW), 0)
    ci = lax.broadcasted_iota(jnp.int32, (PW, PW), 1)
    eye = (ri == ci).astype(F32)
    masks = (_tri_masks(False), _tri_masks(True))

    def stack(x):
        return jnp.concatenate([x * m0, x * m1], axis=0)

    g = {}

    def prepare(g):
        g.update(a2=[], r2=[], v2=[], bh2=[], kh2=[], wend=[], m=[])
        for d, p, c in chains:
            r_ref, kk_ref, v_ref, k_ref, b_ref, lw_ref = in_refs[d]
            sl = (0, pl.ds(c * CH, CH), slice(p * PW, (p + 1) * PW))
            lw = lw_ref[sl]
            cum = _cumsum_rows(lw, bool(d))
            last = cum[0:1] if d else cum[CH - 1:CH]
            e_n = jnp.exp(-cum)
            e_h = jnp.exp(last - cum)
            b = b_ref[sl]
            k = k_ref[sl]
            a2 = stack(-kk_ref[sl] * jnp.exp(cum - lw))
            r2 = stack(r_ref[sl] * jnp.exp(cum))
            g["a2"].append(a2)
            g["r2"].append(r2)
            g["v2"].append(stack(v_ref[sl]))
            g["bh2"].append(stack(b * e_h))
            g["kh2"].append(stack(k * e_h))
            g["wend"].append(jnp.exp(last))
            g["m"].append(_dot_nt(_bf(jnp.concatenate([a2, r2], axis=0)),
                                  _bf(jnp.concatenate([stack(b * e_n), stack(k * e_n)], axis=0))))

    def mask(g):
        strict = [masks[d][0] for d, _, _ in chains]
        incl = [jnp.concatenate([masks[d][1]] * 2, axis=1) for d, _, _ in chains]
        m_ab = [jnp.where(s, m[:PW, :PW], 0.0) for s, m in zip(strict, g["m"])]
        g["m_ak"] = [_bf(jnp.where(s, m[:PW, PW:], 0.0)) for s, m in zip(strict, g["m"])]
        g["m_r"] = [_bf(jnp.where(s, m[PW:, :], 0.0)) for s, m in zip(incl, g["m"])]
        g["q"] = [eye + x for x in m_ab]
        lb = [_bf(x) for x in m_ab]
        g["l"] = [_dot(x, x) for x in lb]

    def double(g):
        lb = [_bf(x) for x in g["l"]]
        res = [_dot(x, jnp.concatenate([x, _bf(q)], axis=1)) for x, q in zip(lb, g["q"])]
        g["l"] = [x[:, :PW] for x in res]
        g["q"] = [q + x[:, PW:] for q, x in zip(g["q"], res)]

    def invert(g):
        g["tinv"] = [q + _dot(_bf(l), _bf(q)) for q, l in zip(g["q"], g["l"])]
        g["mv"] = [_dot(x, _bf(v)) for x, v in zip(g["m_ak"], g["v2"])]

    def solve(g):
        g["pu"] = [_dot(_bf(t), _bf(jnp.concatenate([a, mv], axis=1)))
                   for t, a, mv in zip(g["tinv"], g["a2"], g["mv"])]

    def combine(g):
        rhs = [_bf(jnp.concatenate([jnp.concatenate([pu[:, PW:], pu[:, :PW]], axis=1),
                                    jnp.concatenate([v, jnp.zeros_like(v)], axis=1)], axis=0))
               for pu, v in zip(g["pu"], g["v2"])]
        g["yp"] = [_dot(x, y) for x, y in zip(g["m_r"], rhs)]
        g["gh"] = [_dot_tn(_bf(jnp.concatenate([bh, kh], axis=0)), y)
                   for bh, kh, y in zip(g["bh2"], g["kh2"], rhs)]

    def publish(g):
        for i in range(n):
            g_ref[i] = _bf(g["gh"][i][:, PW:] + eye * g["wend"][i])
            q2_ref[i] = _bf(g["r2"][i] + g["yp"][i][:, PW:])
            y0_ref[i] = g["yp"][i][:, :PW]
            h_ref[i] = g["gh"][i][:, :PW]

    stages = [prepare, mask, double, double, double, double, invert, solve, combine, publish]
    for t, stage in enumerate(stages):
        if t < nch:
            recurrence_round(t)
        if t == nch:
            for j in range(nst):
                st_ref[j] = st[j]
        stage(g)


def _rwkv_scan(r, kk, v, kd0, b0, lw0, kd1, b1, lw1):
    B, TT, _ = r.shape
    nt = TT // TM
    width = SCAN_PAIRS * PW

    def bwd_tile(j):
        return jnp.where(j == 0, 0, nt - j)

    fwd_in = pl.BlockSpec((1, TM, width), lambda b, p, i: (b, jnp.minimum(i, nt - 1), p))
    bwd_in = pl.BlockSpec((1, TM, width), lambda b, p, i: (b, bwd_tile(jnp.minimum(i, nt - 1)), p))
    fwd_out = pl.BlockSpec((1, TM, width), lambda b, p, i: (b, jnp.maximum(i - 1, 0), p))
    bwd_out = pl.BlockSpec((1, TM, width), lambda b, p, i: (b, bwd_tile(jnp.maximum(i - 1, 0)), p))
    out = jax.ShapeDtypeStruct((B, TT, D), BF16)
    n = 2 * SCAN_PAIRS * (TM // CH)
    return pl.pallas_call(
        _scan_kernel,
        grid=(B, NPAIR // SCAN_PAIRS, nt + 1),
        in_specs=[fwd_in] * 6 + [bwd_in] * 6,
        out_specs=[fwd_out, bwd_out],
        out_shape=[out, out],
        scratch_shapes=[pltpu.VMEM((2 * SCAN_PAIRS, PW, PW), F32),
                        pltpu.VMEM((n, PW, PW), BF16), pltpu.VMEM((n, PW, PW), BF16),
                        pltpu.VMEM((n, PW, PW), F32), pltpu.VMEM((n, PW, PW), F32)],
        compiler_params=_params(("parallel", "parallel", "arbitrary")),
        name="rwkv_scan",
    )(r, kk, v, kd0, b0, lw0, r, kk, v, kd1, b1, lw1)


def _out_tail(o, w_ref, post_g, gate, x):
    out = _dot(_bf(o), w_ref[...])
    ms = jnp.mean(out * out, axis=-1, keepdims=True)
    return x + gate * (out * lax.rsqrt(ms + RMS_EPS) * post_g)


def _rwkv_out_kernel(y0_ref, y1_ref, bonus_ref, g_ref, x_ref, mod_ref, vec_ref, w_ref, e_ref, et_ref, o_ref):
    e = e_ref[...]
    et = et_ref[...]
    y = y0_ref[0].astype(F32) + y1_ref[0].astype(F32)
    yc = y - _seg_expand(_seg_reduce(y, e, two_pass=True) * (1.0 / HD), et)
    var = _seg_reduce(yc * yc, e) * (1.0 / HD)
    yn = yc * _seg_expand(lax.rsqrt(var + LNX_EPS), et) * vec_ref[0:1] + vec_ref[1:2]
    o = (yn + bonus_ref[0].astype(F32)) * g_ref[0].astype(F32)
    o_ref[0] = _out_tail(o, w_ref, vec_ref[2:3], mod_ref[0, 0, :, 2 * D:3 * D], x_ref[0])


def _rwkv_out(y0, y1, bonus, g, xc, mod, vec, w_out, e, et):
    B, TT, _ = xc.shape
    nt = TT // TM
    tile = pl.BlockSpec((1, TM, D), lambda b, t: (b, t, 0))
    return pl.pallas_call(
        _rwkv_out_kernel,
        grid=(B, nt),
        in_specs=[tile] * 5 + [
            pl.BlockSpec((1, 1, 1, 3 * D), lambda b, t: (b, jnp.minimum(t, 1), 0, 0)),
            _const_spec((8, D)), _const_spec((D, D)), _const_spec((D, PW)), _const_spec((PW, D))],
        out_specs=tile,
        out_shape=jax.ShapeDtypeStruct((B, TT, D), F32),
        compiler_params=_params(("parallel", "parallel")),
        name="rwkv_out",
    )(y0, y1, bonus, g, xc, mod, vec, w_out, e, et)


def _na_proj_kernel(x_ref, mod_ref, vec_ref, w_ref, bias_ref, q_ref, k_ref, v_ref, g_ref):
    shift = mod_ref[0, 0, :, 0:D]
    scale = mod_ref[0, 0, :, D:2 * D]
    h = _bf(_prenorm(x_ref[0], vec_ref[0:1], scale, shift))
    q = _dot(h, w_ref[:, 0:D]) + bias_ref[:, 0:D]
    q_ref[0] = _bf(q * (HD ** -0.5))
    k_ref[0] = _bf(_dot(h, w_ref[:, D:2 * D]) + bias_ref[:, D:2 * D])
    v_ref[0] = _bf(_dot(h, w_ref[:, 2 * D:3 * D]) + bias_ref[:, 2 * D:3 * D])
    g = _dot(h, w_ref[:, 3 * D:4 * D]) + bias_ref[:, 3 * D:4 * D]
    g_ref[0] = _bf(g * _sigmoid(g))


def _na_proj(xc, mod, vec, w_in, b_in):
    B, TT, _ = xc.shape
    nt = TT // TM
    tile = pl.BlockSpec((1, TM, D), lambda b, t: (b, t, 0))
    bf = jax.ShapeDtypeStruct((B, TT, D), BF16)
    return pl.pallas_call(
        _na_proj_kernel,
        grid=(B, nt),
        in_specs=[tile,
                  pl.BlockSpec((1, 1, 1, 3 * D), lambda b, t: (b, jnp.minimum(t, 1), 0, 0)),
                  _const_spec((8, D)), _const_spec((D, 4 * D)), _const_spec((1, 4 * D))],
        out_specs=[pl.BlockSpec((1, TM, D), lambda b, t: (b, jnp.where(t == 0, nt - 1, t - 1), 0))] + [tile] * 3,
        out_shape=[bf] * 4,
        compiler_params=_params(("parallel", "parallel")),
        name="na_proj",
    )(xc, mod, vec, w_in, b_in)


def _softmax_pv(s_list, v_list):
    mx = None
    for s in s_list:
        m = jnp.max(s, axis=-1, keepdims=True)
        mx = m if mx is None else jnp.maximum(mx, m)
    den = None
    acc = None
    for s, v in zip(s_list, v_list):
        p = jnp.exp(s - mx)
        d = jnp.sum(p, axis=-1, keepdims=True)
        o = _dot(_bf(p), v)
        den = d if den is None else den + d
        acc = o if acc is None else acc + o
    return acc / den


def _na_attn_kernel(rows, q_ref, k_ref, v_ref, bias_ref, o_ref):
    lane = lax.broadcasted_iota(jnp.int32, (1, PW), 1)
    m0 = lane < HD
    step = pl.program_id(2)
    nkeys = WIN_H * GRID_W
    zero = jnp.zeros((), BF16)
    off, start = [], []
    for rr in range(RQ):
        r = step * RQ + rr
        rs = jnp.clip(r - WIN_H // 2, 0, rows - WIN_H)
        off.append(r - rs)
        start.append(pl.multiple_of(TM + rs * GRID_W, GRID_W))
    o_parts = []
    for pp in range(ATT_PAIRS):
        lanes = slice(pp * PW, (pp + 1) * PW)
        q2, kw, vw, bias = [], [], [], []
        for rr in range(RQ):
            q = q_ref[0, rr * GRID_W:(rr + 1) * GRID_W, lanes]
            q2.append(jnp.concatenate([jnp.where(m0, q, zero), jnp.where(m0, zero, q)], axis=0))
            kw.append(k_ref[0, pl.ds(start[rr], nkeys), lanes])
            vw.append(v_ref[0, pl.ds(start[rr], nkeys), lanes])
            bias.append(bias_ref[pp, off[rr]])
        o_parts.append(dict(q2=q2, kw=kw, vw=vw, bias=bias, lanes=lanes))
    for g in o_parts:
        g["s_ctx"] = _dot_nt(jnp.concatenate(g["q2"], axis=0), k_ref[0, 0:TM, g["lanes"]])
        g["s_win"] = [_dot_nt(g["q2"][rr], g["kw"][rr]) + g["bias"][rr] for rr in range(RQ)]
    for g in o_parts:
        mx_ctx = jnp.max(g["s_ctx"], axis=-1, keepdims=True)
        mx = [jnp.maximum(jnp.max(g["s_win"][rr], axis=-1, keepdims=True), mx_ctx[rr * PW:(rr + 1) * PW])
              for rr in range(RQ)]
        g["p_win"] = [jnp.exp(g["s_win"][rr] - mx[rr]) for rr in range(RQ)]
        g["p_ctx"] = jnp.exp(g["s_ctx"] - jnp.concatenate(mx, axis=0))
    for g in o_parts:
        den_ctx = jnp.sum(g["p_ctx"], axis=-1, keepdims=True)
        o_ctx = _dot(_bf(g["p_ctx"]), v_ref[0, 0:TM, g["lanes"]])
        for rr in range(RQ):
            den = jnp.sum(g["p_win"][rr], axis=-1, keepdims=True) + den_ctx[rr * PW:(rr + 1) * PW]
            o2 = (_dot(_bf(g["p_win"][rr]), g["vw"][rr]) + o_ctx[rr * PW:(rr + 1) * PW]) * (1.0 / den)
            o_ref[0, rr * GRID_W:(rr + 1) * GRID_W, g["lanes"]] = _bf(jnp.where(m0, o2[:GRID_W], o2[GRID_W:]))


def _na_attn(q, k, v, bias):
    B, TT, _ = q.shape
    T = TT - TM
    rows = T // GRID_W
    qb = RQ * GRID_W
    width = ATT_PAIRS * PW
    kv = pl.BlockSpec((1, TT, width), lambda b, p, s: (b, 0, p))
    return pl.pallas_call(
        functools.partial(_na_attn_kernel, rows),
        grid=(B, NPAIR // ATT_PAIRS, rows // RQ),
        in_specs=[pl.BlockSpec((1, qb, width), lambda b, p, s: (b, s, p)), kv, kv,
                  pl.BlockSpec((ATT_PAIRS, WIN_H, PW, WIN_H * GRID_W), lambda b, p, s: (p, 0, 0, 0))],
        out_specs=pl.BlockSpec((1, qb, width), lambda b, p, s: (b, s, p)),
        out_shape=jax.ShapeDtypeStruct((B, T, D), BF16),
        compiler_params=_params(("parallel", "parallel", "arbitrary")),
        name="na_attn",
    )(q, k, v, bias)


def _ctx_attn_kernel(q_ref, k_ref, v_ref, o_ref):
    lane = lax.broadcasted_iota(jnp.int32, (1, PW), 1)
    m0 = lane < HD
    zero = jnp.zeros((), BF16)
    q = q_ref[0]
    q2 = jnp.concatenate([jnp.where(m0, q, zero), jnp.where(m0, zero, q)], axis=0)
    kc = k_ref[0]
    o2 = _softmax_pv([_dot_nt(q2, kc)], [v_ref[0]])
    o_ref[0] = _bf(jnp.where(m0, o2[:TM], o2[TM:]))


def _ctx_attn(q, k, v):
    B, TT, _ = q.shape
    blk = pl.BlockSpec((1, TM, PW), lambda b, p: (b, 0, p))
    q_ctx = pl.BlockSpec((1, TM, PW), lambda b, p: (b, TT // TM - 1, p))
    return pl.pallas_call(
        _ctx_attn_kernel,
        grid=(B, NPAIR),
        in_specs=[q_ctx, blk, blk],
        out_specs=blk,
        out_shape=jax.ShapeDtypeStruct((B, TM, D), BF16),
        compiler_params=_params(("parallel", "parallel")),
        name="ctx_attn",
    )(q, k, v)


def _na_out_kernel(with_ctx, *refs):
    if with_ctx:
        ol_ref, oc_ref, g_ref, x_ref, mod_ref, vec_ref, w_ref, o_ref = refs
        o = jnp.where(pl.program_id(1) == 0, oc_ref[0], ol_ref[0])
    else:
        ol_ref, g_ref, x_ref, mod_ref, vec_ref, w_ref, o_ref = refs
        o = ol_ref[0]
    o = o.astype(F32) * g_ref[0].astype(F32)
    o_ref[0] = _out_tail(o, w_ref, vec_ref[1:2], mod_ref[0, 0, :, 2 * D:3 * D], x_ref[0])


def _na_out(o_lat, o_ctx, g, xc, mod, vec, w_out):
    B, TT, _ = xc.shape
    nt = TT // TM
    with_ctx = o_ctx is not None
    consts = [_const_spec((8, D)), _const_spec((D, D))]
    if with_ctx:
        tile = pl.BlockSpec((1, TM, D), lambda b, t: (b, t, 0))
        in_specs = [pl.BlockSpec((1, TM, D), lambda b, t: (b, jnp.maximum(t - 1, 0), 0)),
                    pl.BlockSpec((1, TM, D), lambda b, t: (b, 0, 0)),
                    tile, tile,
                    pl.BlockSpec((1, 1, 1, 3 * D), lambda b, t: (b, jnp.minimum(t, 1), 0, 0))] + consts
        args = (o_lat, o_ctx, g, xc, mod, vec, w_out)
        grid, out_spec, out_rows = (B, nt), tile, TT
    else:
        lat = pl.BlockSpec((1, TM, D), lambda b, t: (b, t + 1, 0))
        out_spec = pl.BlockSpec((1, TM, D), lambda b, t: (b, t, 0))
        in_specs = [out_spec, lat, lat,
                    pl.BlockSpec((1, 1, 1, 3 * D), lambda b, t: (b, 1, 0, 0))] + consts
        args = (o_lat, g, xc, mod, vec, w_out)
        grid, out_rows = (B, nt - 1), TT - TM
    return pl.pallas_call(
        functools.partial(_na_out_kernel, with_ctx),
        grid=grid,
        in_specs=in_specs,
        out_specs=out_spec,
        out_shape=jax.ShapeDtypeStruct((B, out_rows, D), F32),
        compiler_params=_params(("parallel", "parallel")),
        name="na_out",
    )(*args)


def _na_bias_table(rpb):
    layers = rpb.shape[0]
    ncol = 2 * WIN_W - 1
    j = np.arange(GRID_W)
    win_start = np.clip(j - WIN_W // 2, 0, GRID_W - WIN_W)
    kcol = np.arange(GRID_W)
    valid = (kcol[None, :] >= win_start[:, None]) & (kcol[None, :] < win_start[:, None] + WIN_W)
    dc = np.clip(kcol[None, :] - j[:, None], -(WIN_W - 1), WIN_W - 1) + (WIN_W - 1)
    off = np.arange(WIN_H)
    i = np.arange(WIN_H)
    dr = i[None, :] - off[:, None] + (WIN_H - 1)
    idx = dr[:, None, :, None] * ncol + dc[None, :, None, :]
    idx = idx.reshape(WIN_H, GRID_W, WIN_H * GRID_W)
    per_head = (2 * WIN_H - 1) * ncol
    idx = np.concatenate([idx, idx + per_head], axis=1)
    ok = np.broadcast_to(valid[None, :, None, :], (WIN_H, GRID_W, WIN_H, GRID_W)).reshape(idx.shape[0], GRID_W, -1)
    ok = np.concatenate([ok, ok], axis=1)
    flat = rpb.reshape(layers, NPAIR, 2 * per_head)
    tab = jnp.take(flat, jnp.asarray(idx, jnp.int32), axis=2)
    return jnp.where(jnp.asarray(ok)[None, None], tab, NEG_INF).astype(F32)


def _pad_rows(m, rows):
    return jnp.pad(m, ((0, rows - m.shape[0]), (0, 0)))


def kernel(x, c, ctx, c_ctx, ada_w, ada_b, pre_g, post_g, rw_mu, rw_w_rkvg, rw_w0, rw_w1, rw_w2, rw_a0, rw_a1, rw_a2, rw_v0, rw_v1, rw_v2, rw_k_k, rw_k_a, rw_r_k, rw_lnx_w, rw_lnx_b, rw_w_out, na_w_in, na_b_in, na_rpb, na_w_out):
    B, T, _ = x.shape
    depth = ada_w.shape[0]
    assert ctx.shape[1] == TM and T % (RQ * GRID_W) == 0 and T % TM == 0 and T // GRID_W >= WIN_H

    seg = (np.arange(D)[:, None] // HD) == np.arange(PW)[None, :]
    e = jnp.asarray(seg, BF16)
    et = jnp.asarray(seg.T, BF16)

    cond = jnp.concatenate([c, c_ctx[None, :]], axis=0)
    nrow = -(-(B + 1) // 8) * 8
    cond = _pad_rows(cond * jax.nn.sigmoid(cond), nrow)
    mod_all = _adaln(cond, ada_w, ada_b)
    mod_ctx = jnp.broadcast_to(mod_all[:, B:B + 1], (depth, B, 3 * D))
    mod_all = jnp.stack([mod_ctx, mod_all[:, :B]], axis=2)[:, :, :, None, :]

    bias_all = _na_bias_table(na_rpb)
    xc = jnp.concatenate([ctx, x], axis=1)
    v_first = None
    for i in range(depth):
        last = i == depth - 1
        j = i // 2
        mod = mod_all[i]
        if i % 2 == 0:
            zero = jnp.zeros((D,), F32)
            vec = jnp.stack([pre_g[i], *rw_mu[j], rw_k_k[j], rw_k_a[j], rw_r_k[j].reshape(D),
                             rw_w0[j, 0], rw_w0[j, 1], rw_a0[j, 0], rw_a0[j, 1],
                             rw_v0[j - 1] if j > 0 else zero, zero])
            lora = D // 16
            w1c = _bf(jnp.concatenate([rw_w1[j, 0], rw_w1[j, 1]], axis=1))
            a1c = _bf(jnp.concatenate([rw_a1[j, 0], rw_a1[j, 1]], axis=1))
            zl = jnp.zeros((lora, D), F32)
            w2z = _bf(jnp.stack([jnp.concatenate([rw_w2[j, 0], zl]), jnp.concatenate([zl, rw_w2[j, 1]])]))
            a2z = _bf(jnp.stack([jnp.concatenate([rw_a2[j, 0], zl]), jnp.concatenate([zl, rw_a2[j, 1]])]))
            vres = None
            if j > 0:
                v1p = _bf(jnp.pad(rw_v1[j - 1], ((0, 0), (0, PW - rw_v1.shape[-1]))))
                v2p = _bf(_pad_rows(rw_v2[j - 1], PW))
                vres = (v1p, v2p, v_first)
            wq = _bf(rw_w_rkvg[j])
            r, v, kk, g, bonus, kd0, kd1, b0, b1, lw0, lw1 = _rwkv_proj(
                xc, mod, vec, wq[0], wq[1], wq[2], wq[3], w1c, w2z, a1c, a2z, e, et, vres)
            if j == 0:
                v_first = v
            y0, y1 = _rwkv_scan(r, kk, v, kd0, b0, lw0, kd1, b1, lw1)
            vec_o = _pad_rows(jnp.stack([rw_lnx_w[j], rw_lnx_b[j], post_g[i]]), 8)
            xc = _rwkv_out(y0, y1, bonus, g, xc, mod, vec_o, _bf(rw_w_out[j]), e, et)
        else:
            vec = _pad_rows(jnp.stack([pre_g[i], post_g[i]]), 8)
            q, k, v, g = _na_proj(xc, mod, vec, _bf(na_w_in[j]), na_b_in[j][None, :])
            o_lat = _na_attn(q, k, v, bias_all[j])
            o_ctx = None if last else _ctx_attn(q, k, v)
            xc = _na_out(o_lat, o_ctx, g, xc, mod, vec, _bf(na_w_out[j]))
    return xc if xc.shape[1] == T else xc[:, TM:]
```

```python
import functools
import math

import numpy as np
import jax
import jax.numpy as jnp
from jax import lax
from jax.experimental import pallas as pl
from jax.experimental.pallas import tpu as pltpu

F32 = jnp.float32
BF16 = jnp.bfloat16

D = 1024
HD = 64
NH = D // HD
PW = 2 * HD
NPAIR = D // PW
TM = 256
CH = 64
GRID_W = 64
WIN_H = 8
WIN_W = 16
RMS_EPS = 1e-6
LNX_EPS = 64e-5
NEG_INF = -1e30
EXP_M05 = math.exp(-0.5)
RQ = 8
SCAN_PAIRS = 4
ATT_PAIRS = 2
VMEM_LIMIT = 56 * 1024 * 1024


def _bf(x):
    return x.astype(BF16)


def _dot(a, b):
    return jnp.dot(a, b, preferred_element_type=F32)


def _dot_nt(a, b):
    return lax.dot_general(a, b, (((1,), (1,)), ((), ())), preferred_element_type=F32)


def _dot_tn(a, b):
    return lax.dot_general(a, b, (((0,), (0,)), ((), ())), preferred_element_type=F32)


def _dot_split(x, e):
    hi = _bf(x)
    lo = _bf(x - hi.astype(F32))
    return _dot(hi, e) + _dot(lo, e)


def _seg_reduce(x, e, two_pass=False):
    return _dot_split(x, e) if two_pass else _dot(_bf(x), e)


def _seg_expand(c, et):
    return _dot_split(c, et)


def _sigmoid(x):
    return 0.5 * jnp.tanh(0.5 * x) + 0.5


def _prenorm(x, g, scale, shift):
    ms = jnp.mean(x * x, axis=-1, keepdims=True)
    return (x * lax.rsqrt(ms + RMS_EPS) * g) * (1.0 + scale) + shift


def _const_spec(shape):
    nd = len(shape)
    return pl.BlockSpec(shape, lambda *_: (0,) * nd, pipeline_mode=pl.Buffered(1))


def _params(sem):
    return pltpu.CompilerParams(dimension_semantics=sem, vmem_limit_bytes=VMEM_LIMIT)


def _adaln_kernel(s_ref, w_ref, b_ref, o_ref):
    s = s_ref[...]
    w = w_ref[0]
    hi = _bf(s)
    lo = _bf(s - hi.astype(F32))
    whi = _bf(w)
    wlo = _bf(w - whi.astype(F32))
    o_ref[0] = _dot(hi, whi) + _dot(lo, whi) + _dot(hi, wlo) + b_ref[0]


def _adaln(silu_rows, ada_w, ada_b):
    depth = ada_w.shape[0]
    nrow = silu_rows.shape[0]
    return pl.pallas_call(
        _adaln_kernel,
        grid=(depth, 3),
        in_specs=[pl.BlockSpec((nrow, D), lambda i, j: (0, 0)),
                  pl.BlockSpec((1, D, D), lambda i, j: (i, 0, j)),
                  pl.BlockSpec((1, 1, D), lambda i, j: (i, 0, j))],
        out_specs=pl.BlockSpec((1, nrow, D), lambda i, j: (i, 0, j)),
        out_shape=jax.ShapeDtypeStruct((depth, nrow, 3 * D), F32),
        compiler_params=_params(("parallel", "parallel")),
        name="adaln",
    )(silu_rows, ada_w, ada_b.reshape(depth, 1, 3 * D))


_V_PRE_G, _V_MU, _V_KK, _V_KA, _V_RK, _V_W0, _V_A0, _V_V0 = 0, 1, 7, 8, 9, 10, 12, 14


def _rwkv_proj_kernel(has_vres, split, nt, *refs):
    if split:
        ctx_ref, refs = refs[0], refs[1:]
    if has_vres:
        (x_ref, xp_ref, xn_ref, mod_ref, vec_ref, wr_ref, wk_ref, wv_ref, wg_ref, w1_ref, w2_ref,
         a1_ref, a2_ref, e_ref, et_ref, v1_ref, v2_ref, vf_ref,
         r_ref, v_ref, kk_ref, g_ref, bonus_ref, kd0_ref, kd1_ref, b0_ref, b1_ref, lw0_ref, lw1_ref) = refs
    else:
        (x_ref, xp_ref, xn_ref, mod_ref, vec_ref, wr_ref, wk_ref, wv_ref, wg_ref, w1_ref, w2_ref,
         a1_ref, a2_ref, e_ref, et_ref,
         r_ref, v_ref, kk_ref, g_ref, bonus_ref, kd0_ref, kd1_ref, b0_ref, b1_ref, lw0_ref, lw1_ref) = refs
    t = pl.program_id(1)
    shift = mod_ref[0, 0, :, 0:D]
    scale = mod_ref[0, 0, :, D:2 * D]
    g_pre = vec_ref[_V_PRE_G:_V_PRE_G + 1]

    x = jnp.where(t == 0, ctx_ref[0], x_ref[0]) if split else x_ref[0]
    h = _prenorm(x, g_pre, scale, shift)
    hp = _prenorm(xp_ref[0], g_pre, scale, shift)[7:8]
    hn = _prenorm(xn_ref[0], g_pre, scale, shift)[0:1]
    hp = jnp.where(t >= 2, hp, 0.0)
    hn = jnp.where(jnp.logical_and(t >= 1, t < nt - 1), hn, 0.0)
    row = lax.broadcasted_iota(jnp.int32, (TM, 1), 0)
    h_m1 = jnp.where(row == 0, hp, pltpu.roll(h, 1, 0))
    h_p1 = jnp.where(row == TM - 1, hn, pltpu.roll(h, TM - 1, 0))
    xx = 0.5 * (h_m1 + h_p1) - h

    def lerp(n):
        return _bf(h + xx * vec_ref[_V_MU + n:_V_MU + n + 1])

    r = _dot(lerp(0), wr_ref[...])
    k = _dot(lerp(2), wk_ref[...])
    xv = lerp(3)
    v = _dot(xv, wv_ref[...])
    g = _dot(lerp(5), wg_ref[...])
    g = g * _sigmoid(g)
    if has_vres:
        lv = _dot(_bf(_dot(xv, v1_ref[...])), v2_ref[...])
        v = v + (vf_ref[0] - v) * _sigmoid(vec_ref[_V_V0:_V_V0 + 1] + lv)
    tw = _bf(jnp.tanh(_dot(lerp(1), w1_ref[...])))
    la = _bf(_dot(lerp(4), a1_ref[...]))

    e = e_ref[...]
    et = et_ref[...]
    kkr = k * vec_ref[_V_KK:_V_KK + 1]
    kk = kkr * _seg_expand(jnp.minimum(lax.rsqrt(_seg_reduce(kkr * kkr, e)), 1e12), et)
    k_a = vec_ref[_V_KA:_V_KA + 1]

    ksum = None
    for d, (kd_ref, b_ref, lw_ref) in enumerate(((kd0_ref, b0_ref, lw0_ref), (kd1_ref, b1_ref, lw1_ref))):
        wl = vec_ref[_V_W0 + d:_V_W0 + d + 1] + _dot(tw, w2_ref[d])
        lw_ref[0] = -EXP_M05 * _sigmoid(wl)
        a = _sigmoid(vec_ref[_V_A0 + d:_V_A0 + d + 1] + _dot(la, a2_ref[d]))
        kd = k * (1.0 + (a - 1.0) * k_a)
        kd_ref[0] = kd
        b_ref[0] = kk * a
        ksum = kd if ksum is None else ksum + kd

    r_ref[0] = r
    v_ref[0] = v
    kk_ref[0] = kk
    g_ref[0] = _bf(g)
    bonus_ref[0] = _bf(_seg_expand(_seg_reduce(r * ksum * vec_ref[_V_RK:_V_RK + 1], e), et) * v)


def _stream_specs(stream):
    if isinstance(stream, tuple):
        ctx, x = stream
        B, T, _ = x.shape
        specs = [pl.BlockSpec((1, TM, D), lambda b, t: (b, 0, 0)),
                 pl.BlockSpec((1, TM, D), lambda b, t: (b, jnp.maximum(t - 1, 0), 0))]
        return specs, [ctx, x], B, T + TM, 1
    B, TT, _ = stream.shape
    return [pl.BlockSpec((1, TM, D), lambda b, t: (b, t, 0))], [stream], B, TT, 0


def _rwkv_proj(stream, mod, vec, wr, wk, wv, wg, w1c, w2z, a1c, a2z, e, et, vres):
    cur_specs, cur_args, B, TT, lat0 = _stream_specs(stream)
    split = lat0 == 1
    nt = TT // TM
    has_vres = vres is not None
    tile = pl.BlockSpec((1, TM, D), lambda b, t: (b, t, 0))
    rows8 = (TT - lat0 * TM) // 8
    in_specs = cur_specs + [
        pl.BlockSpec((1, 8, D), lambda b, t: (b, jnp.maximum((t - lat0) * (TM // 8) - 1, 0), 0)),
        pl.BlockSpec((1, 8, D), lambda b, t: (b, jnp.minimum((t + 1 - lat0) * (TM // 8), rows8 - 1), 0)),
        pl.BlockSpec((1, 1, 1, 3 * D), lambda b, t: (b, jnp.minimum(t, 1), 0, 0)),
        _const_spec((16, D)),
        _const_spec((D, D)), _const_spec((D, D)), _const_spec((D, D)), _const_spec((D, D)),
        _const_spec((D, PW)), _const_spec((2, PW, D)), _const_spec((D, PW)), _const_spec((2, PW, D)),
        _const_spec((D, PW)), _const_spec((PW, D)),
    ]
    args = cur_args + [cur_args[-1], cur_args[-1], mod, vec, wr, wk, wv, wg, w1c, w2z, a1c, a2z, e, et]
    if has_vres:
        v1p, v2p, v_first = vres
        in_specs += [_const_spec((D, PW)), _const_spec((PW, D)), tile]
        args += [v1p, v2p, v_first]
    out = jax.ShapeDtypeStruct((B, TT, D), F32)
    half = jax.ShapeDtypeStruct((B, TT, D), BF16)
    return pl.pallas_call(
        functools.partial(_rwkv_proj_kernel, has_vres, split, nt),
        grid=(B, nt),
        in_specs=in_specs,
        out_specs=[tile] * 11,
        out_shape=[out] * 3 + [half] * 2 + [out] * 6,
        compiler_params=_params(("parallel", "parallel")),
        name="rwkv_proj",
    )(*args)


def _cumsum_rows(x, rev):
    row = lax.broadcasted_iota(jnp.int32, (CH, 1), 0)
    s = 1
    while s < CH:
        if rev:
            x = x + jnp.where(row < CH - s, pltpu.roll(x, CH - s, 0), 0.0)
        else:
            x = x + jnp.where(row >= s, pltpu.roll(x, s, 0), 0.0)
        s *= 2
    return x


def _tri_masks(rev):
    ri = lax.broadcasted_iota(jnp.int32, (PW, PW), 0)
    ci = lax.broadcasted_iota(jnp.int32, (PW, PW), 1)
    same = (ri >> 6) == (ci >> 6)
    rl = ri & (CH - 1)
    cl = ci & (CH - 1)
    if rev:
        return jnp.logical_and(same, cl > rl), jnp.logical_and(same, cl >= rl)
    return jnp.logical_and(same, cl < rl), jnp.logical_and(same, cl <= rl)


def _scan_kernel(*refs):
    in_refs = (refs[0:6], refs[6:12])
    y_refs = refs[12:14]
    st_ref, q2_ref, g_ref, y0_ref, h_ref = refs[14:19]

    @pl.when(pl.program_id(2) == 0)
    def _():
        for ref in (st_ref, q2_ref, g_ref, y0_ref, h_ref):
            ref[...] = jnp.zeros_like(ref)

    nch = TM // CH
    chains = [(d, p, c) for d in (0, 1) for p in range(SCAN_PAIRS)
              for c in (range(nch - 1, -1, -1) if d else range(nch))]
    n = len(chains)
    nst = 2 * SCAN_PAIRS
    st = [st_ref[j] for j in range(nst)]

    def recurrence_round(k):
        for j in range(nst):
            i = j * nch + k
            d, p, c = chains[i]
            sb = _bf(st[j])
            y2 = _dot(q2_ref[i], sb) + y0_ref[i]
            st[j] = _dot(g_ref[i], sb) + h_ref[i]
            y_refs[d][0, pl.ds(c * CH, CH), p * PW:(p + 1) * PW] = _bf(y2[:CH] + y2[CH:])

    lane = lax.broadcasted_iota(jnp.int32, (1, PW), 1)
    m0 = (lane < HD).astype(F32)
    m1 = 1.0 - m0
    ri = lax.broadcasted_iota(jnp.int32, (PW, PW), 0)
    ci = lax.broadcasted_iota(jnp.int32, (PW, PW), 1)
    eye = (ri == ci).astype(F32)
    masks = (_tri_masks(False), _tri_masks(True))

    def stack(x):
        return jnp.concatenate([x * m0, x * m1], axis=0)

    g = {}

    def prepare(g):
        g.update(a2=[], r2=[], v2=[], bh2=[], kh2=[], wend=[], m=[])
        for d, p, c in chains:
            r_ref, kk_ref, v_ref, k_ref, b_ref, lw_ref = in_refs[d]
            sl = (0, pl.ds(c * CH, CH), slice(p * PW, (p + 1) * PW))
            lw = lw_ref[sl]
            cum = _cumsum_rows(lw, bool(d))
            last = cum[0:1] if d else cum[CH - 1:CH]
            e_n = jnp.exp(-cum)
            e_h = jnp.exp(last - cum)
            b = b_ref[sl]
            k = k_ref[sl]
            a2 = stack(-kk_ref[sl] * jnp.exp(cum - lw))
            r2 = stack(r_ref[sl] * jnp.exp(cum))
            g["a2"].append(a2)
            g["r2"].append(r2)
            g["v2"].append(stack(v_ref[sl]))
            g["bh2"].append(stack(b * e_h))
            g["kh2"].append(stack(k * e_h))
            g["wend"].append(jnp.exp(last))
            g["m"].append(_dot_nt(_bf(jnp.concatenate([a2, r2], axis=0)),
                                  _bf(jnp.concatenate([stack(b * e_n), stack(k * e_n)], axis=0))))

    def mask(g):
        strict = [masks[d][0] for d, _, _ in chains]
        incl = [jnp.concatenate([masks[d][1]] * 2, axis=1) for d, _, _ in chains]
        m_ab = [jnp.where(s, m[:PW, :PW], 0.0) for s, m in zip(strict, g["m"])]
        g["m_ak"] = [_bf(jnp.where(s, m[:PW, PW:], 0.0)) for s, m in zip(strict, g["m"])]
        g["m_r"] = [_bf(jnp.where(s, m[PW:, :], 0.0)) for s, m in zip(incl, g["m"])]
        g["q"] = [eye + x for x in m_ab]
        lb = [_bf(x) for x in m_ab]
        g["l"] = [_dot(x, x) for x in lb]

    def double(g):
        lb = [_bf(x) for x in g["l"]]
        res = [_dot(x, jnp.concatenate([x, _bf(q)], axis=1)) for x, q in zip(lb, g["q"])]
        g["l"] = [x[:, :PW] for x in res]
        g["q"] = [q + x[:, PW:] for q, x in zip(g["q"], res)]

    def invert(g):
        g["tinv"] = [q + _dot(_bf(l), _bf(q)) for q, l in zip(g["q"], g["l"])]
        g["mv"] = [_dot(x, _bf(v)) for x, v in zip(g["m_ak"], g["v2"])]

    def solve(g):
        g["pu"] = [_dot(_bf(t), _bf(jnp.concatenate([a, mv], axis=1)))
                   for t, a, mv in zip(g["tinv"], g["a2"], g["mv"])]

    def combine(g):
        rhs = [_bf(jnp.concatenate([jnp.concatenate([pu[:, PW:], pu[:, :PW]], axis=1),
                                    jnp.concatenate([v, jnp.zeros_like(v)], axis=1)], axis=0))
               for pu, v in zip(g["pu"], g["v2"])]
        g["yp"] = [_dot(x, y) for x, y in zip(g["m_r"], rhs)]
        g["gh"] = [_dot_tn(_bf(jnp.concatenate([bh, kh], axis=0)), y)
                   for bh, kh, y in zip(g["bh2"], g["kh2"], rhs)]

    def publish(g):
        for i in range(n):
            g_ref[i] = _bf(g["gh"][i][:, PW:] + eye * g["wend"][i])
            q2_ref[i] = _bf(g["r2"][i] + g["yp"][i][:, PW:])
            y0_ref[i] = g["yp"][i][:, :PW]
            h_ref[i] = g["gh"][i][:, :PW]

    stages = [prepare, mask, double, double, double, double, invert, solve, combine, publish]
    for t, stage in enumerate(stages):
        if t < nch:
            recurrence_round(t)
        if t == nch:
            for j in range(nst):
                st_ref[j] = st[j]
        stage(g)


def _rwkv_scan(r, kk, v, kd0, b0, lw0, kd1, b1, lw1):
    B, TT, _ = r.shape
    nt = TT // TM
    width = SCAN_PAIRS * PW

    def bwd_tile(j):
        return jnp.where(j == 0, 0, nt - j)

    fwd_in = pl.BlockSpec((1, TM, width), lambda b, p, i: (b, jnp.minimum(i, nt - 1), p))
    bwd_in = pl.BlockSpec((1, TM, width), lambda b, p, i: (b, bwd_tile(jnp.minimum(i, nt - 1)), p))
    fwd_out = pl.BlockSpec((1, TM, width), lambda b, p, i: (b, jnp.maximum(i - 1, 0), p))
    bwd_out = pl.BlockSpec((1, TM, width), lambda b, p, i: (b, bwd_tile(jnp.maximum(i - 1, 0)), p))
    out = jax.ShapeDtypeStruct((B, TT, D), BF16)
    n = 2 * SCAN_PAIRS * (TM // CH)
    return pl.pallas_call(
        _scan_kernel,
        grid=(B, NPAIR // SCAN_PAIRS, nt + 1),
        in_specs=[fwd_in] * 6 + [bwd_in] * 6,
        out_specs=[fwd_out, bwd_out],
        out_shape=[out, out],
        scratch_shapes=[pltpu.VMEM((2 * SCAN_PAIRS, PW, PW), F32),
                        pltpu.VMEM((n, PW, PW), BF16), pltpu.VMEM((n, PW, PW), BF16),
                        pltpu.VMEM((n, PW, PW), F32), pltpu.VMEM((n, PW, PW), F32)],
        compiler_params=_params(("parallel", "parallel", "arbitrary")),
        name="rwkv_scan",
    )(r, kk, v, kd0, b0, lw0, r, kk, v, kd1, b1, lw1)


def _out_tail(o, w_ref, post_g, gate, x):
    out = _dot(_bf(o), w_ref[...])
    ms = jnp.mean(out * out, axis=-1, keepdims=True)
    return x + gate * (out * lax.rsqrt(ms + RMS_EPS) * post_g)


def _rwkv_out_kernel(split, y0_ref, y1_ref, bonus_ref, g_ref, *refs):
    if split:
        ctx_ref, refs = refs[0], refs[1:]
    x_ref, mod_ref, vec_ref, w_ref, e_ref, et_ref, o_ref = refs
    e = e_ref[...]
    et = et_ref[...]
    y = y0_ref[0].astype(F32) + y1_ref[0].astype(F32)
    yc = y - _seg_expand(_seg_reduce(y, e, two_pass=True) * (1.0 / HD), et)
    var = _seg_reduce(yc * yc, e) * (1.0 / HD)
    yn = yc * _seg_expand(lax.rsqrt(var + LNX_EPS), et) * vec_ref[0:1] + vec_ref[1:2]
    o = (yn + bonus_ref[0].astype(F32)) * g_ref[0].astype(F32)
    x = jnp.where(pl.program_id(1) == 0, ctx_ref[0], x_ref[0]) if split else x_ref[0]
    o_ref[0] = _out_tail(o, w_ref, vec_ref[2:3], mod_ref[0, 0, :, 2 * D:3 * D], x)


def _rwkv_out(y0, y1, bonus, g, stream, mod, vec, w_out, e, et):
    cur_specs, cur_args, B, TT, lat0 = _stream_specs(stream)
    nt = TT // TM
    tile = pl.BlockSpec((1, TM, D), lambda b, t: (b, t, 0))
    return pl.pallas_call(
        functools.partial(_rwkv_out_kernel, lat0 == 1),
        grid=(B, nt),
        in_specs=[tile] * 4 + cur_specs + [
            pl.BlockSpec((1, 1, 1, 3 * D), lambda b, t: (b, jnp.minimum(t, 1), 0, 0)),
            _const_spec((8, D)), _const_spec((D, D)), _const_spec((D, PW)), _const_spec((PW, D))],
        out_specs=tile,
        out_shape=jax.ShapeDtypeStruct((B, TT, D), F32),
        compiler_params=_params(("parallel", "parallel")),
        name="rwkv_out",
    )(y0, y1, bonus, g, *cur_args, mod, vec, w_out, e, et)


def _na_proj_kernel(x_ref, mod_ref, vec_ref, w_ref, bias_ref, q_ref, k_ref, v_ref, g_ref):
    shift = mod_ref[0, 0, :, 0:D]
    scale = mod_ref[0, 0, :, D:2 * D]
    h = _bf(_prenorm(x_ref[0], vec_ref[0:1], scale, shift))
    q = _dot(h, w_ref[:, 0:D]) + bias_ref[:, 0:D]
    q_ref[0] = _bf(q * (HD ** -0.5))
    k_ref[0] = _bf(_dot(h, w_ref[:, D:2 * D]) + bias_ref[:, D:2 * D])
    v_ref[0] = _bf(_dot(h, w_ref[:, 2 * D:3 * D]) + bias_ref[:, 2 * D:3 * D])
    g = _dot(h, w_ref[:, 3 * D:4 * D]) + bias_ref[:, 3 * D:4 * D]
    g_ref[0] = _bf(g * _sigmoid(g))


def _na_proj(xc, mod, vec, w_in, b_in):
    B, TT, _ = xc.shape
    nt = TT // TM
    tile = pl.BlockSpec((1, TM, D), lambda b, t: (b, t, 0))
    bf = jax.ShapeDtypeStruct((B, TT, D), BF16)
    return pl.pallas_call(
        _na_proj_kernel,
        grid=(B, nt),
        in_specs=[tile,
                  pl.BlockSpec((1, 1, 1, 3 * D), lambda b, t: (b, jnp.minimum(t, 1), 0, 0)),
                  _const_spec((8, D)), _const_spec((D, 4 * D)), _const_spec((1, 4 * D))],
        out_specs=[pl.BlockSpec((1, TM, D), lambda b, t: (b, jnp.where(t == 0, nt - 1, t - 1), 0))] + [tile] * 3,
        out_shape=[bf] * 4,
        compiler_params=_params(("parallel", "parallel")),
        name="na_proj",
    )(xc, mod, vec, w_in, b_in)


def _softmax_pv(s_list, v_list):
    mx = None
    for s in s_list:
        m = jnp.max(s, axis=-1, keepdims=True)
        mx = m if mx is None else jnp.maximum(mx, m)
    den = None
    acc = None
    for s, v in zip(s_list, v_list):
        p = jnp.exp(s - mx)
        d = jnp.sum(p, axis=-1, keepdims=True)
        o = _dot(_bf(p), v)
        den = d if den is None else den + d
        acc = o if acc is None else acc + o
    return acc / den


def _na_attn_kernel(rows, q_ref, k_ref, v_ref, bias_ref, o_ref):
    lane = lax.broadcasted_iota(jnp.int32, (1, PW), 1)
    m0 = lane < HD
    step = pl.program_id(2)
    nkeys = WIN_H * GRID_W
    zero = jnp.zeros((), BF16)
    off, start = [], []
    for rr in range(RQ):
        r = step * RQ + rr
        rs = jnp.clip(r - WIN_H // 2, 0, rows - WIN_H)
        off.append(r - rs)
        start.append(pl.multiple_of(TM + rs * GRID_W, GRID_W))
    o_parts = []
    for pp in range(ATT_PAIRS):
        lanes = slice(pp * PW, (pp + 1) * PW)
        q2, kw, vw, bias = [], [], [], []
        for rr in range(RQ):
            q = q_ref[0, rr * GRID_W:(rr + 1) * GRID_W, lanes]
            q2.append(jnp.concatenate([jnp.where(m0, q, zero), jnp.where(m0, zero, q)], axis=0))
            kw.append(k_ref[0, pl.ds(start[rr], nkeys), lanes])
            vw.append(v_ref[0, pl.ds(start[rr], nkeys), lanes])
            bias.append(bias_ref[pp, off[rr]])
        o_parts.append(dict(q2=q2, kw=kw, vw=vw, bias=bias, lanes=lanes))
    for g in o_parts:
        g["s_ctx"] = _dot_nt(jnp.concatenate(g["q2"], axis=0), k_ref[0, 0:TM, g["lanes"]])
        g["s_win"] = [_dot_nt(g["q2"][rr], g["kw"][rr]) + g["bias"][rr] for rr in range(RQ)]
    for g in o_parts:
        mx_ctx = jnp.max(g["s_ctx"], axis=-1, keepdims=True)
        mx = [jnp.maximum(jnp.max(g["s_win"][rr], axis=-1, keepdims=True), mx_ctx[rr * PW:(rr + 1) * PW])
              for rr in range(RQ)]
        g["p_win"] = [jnp.exp(g["s_win"][rr] - mx[rr]) for rr in range(RQ)]
        g["p_ctx"] = jnp.exp(g["s_ctx"] - jnp.concatenate(mx, axis=0))
    for g in o_parts:
        den_ctx = jnp.sum(g["p_ctx"], axis=-1, keepdims=True)
        o_ctx = _dot(_bf(g["p_ctx"]), v_ref[0, 0:TM, g["lanes"]])
        for rr in range(RQ):
            den = jnp.sum(g["p_win"][rr], axis=-1, keepdims=True) + den_ctx[rr * PW:(rr + 1) * PW]
            o2 = (_dot(_bf(g["p_win"][rr]), g["vw"][rr]) + o_ctx[rr * PW:(rr + 1) * PW]) * (1.0 / den)
            o_ref[0, rr * GRID_W:(rr + 1) * GRID_W, g["lanes"]] = _bf(jnp.where(m0, o2[:GRID_W], o2[GRID_W:]))


def _na_attn(q, k, v, bias):
    B, TT, _ = q.shape
    T = TT - TM
    rows = T // GRID_W
    qb = RQ * GRID_W
    width = ATT_PAIRS * PW
    kv = pl.BlockSpec((1, TT, width), lambda b, p, s: (b, 0, p))
    return pl.pallas_call(
        functools.partial(_na_attn_kernel, rows),
        grid=(B, NPAIR // ATT_PAIRS, rows // RQ),
        in_specs=[pl.BlockSpec((1, qb, width), lambda b, p, s: (b, s, p)), kv, kv,
                  pl.BlockSpec((ATT_PAIRS, WIN_H, PW, WIN_H * GRID_W), lambda b, p, s: (p, 0, 0, 0))],
        out_specs=pl.BlockSpec((1, qb, width), lambda b, p, s: (b, s, p)),
        out_shape=jax.ShapeDtypeStruct((B, T, D), BF16),
        compiler_params=_params(("parallel", "parallel", "arbitrary")),
        name="na_attn",
    )(q, k, v, bias)


def _ctx_attn_kernel(q_ref, k_ref, v_ref, o_ref):
    lane = lax.broadcasted_iota(jnp.int32, (1, PW), 1)
    m0 = lane < HD
    zero = jnp.zeros((), BF16)
    q = q_ref[0]
    q2 = jnp.concatenate([jnp.where(m0, q, zero), jnp.where(m0, zero, q)], axis=0)
    kc = k_ref[0]
    o2 = _softmax_pv([_dot_nt(q2, kc)], [v_ref[0]])
    o_ref[0] = _bf(jnp.where(m0, o2[:TM], o2[TM:]))


def _ctx_attn(q, k, v):
    B, TT, _ = q.shape
    blk = pl.BlockSpec((1, TM, PW), lambda b, p: (b, 0, p))
    q_ctx = pl.BlockSpec((1, TM, PW), lambda b, p: (b, TT // TM - 1, p))
    return pl.pallas_call(
        _ctx_attn_kernel,
        grid=(B, NPAIR),
        in_specs=[q_ctx, blk, blk],
        out_specs=blk,
        out_shape=jax.ShapeDtypeStruct((B, TM, D), BF16),
        compiler_params=_params(("parallel", "parallel")),
        name="ctx_attn",
    )(q, k, v)


def _na_out_kernel(with_ctx, *refs):
    if with_ctx:
        ol_ref, oc_ref, g_ref, x_ref, mod_ref, vec_ref, w_ref, o_ref = refs
        o = jnp.where(pl.program_id(1) == 0, oc_ref[0], ol_ref[0])
    else:
        ol_ref, g_ref, x_ref, mod_ref, vec_ref, w_ref, o_ref = refs
        o = ol_ref[0]
    o = o.astype(F32) * g_ref[0].astype(F32)
    o_ref[0] = _out_tail(o, w_ref, vec_ref[1:2], mod_ref[0, 0, :, 2 * D:3 * D], x_ref[0])


def _na_out(o_lat, o_ctx, g, xc, mod, vec, w_out):
    B, TT, _ = xc.shape
    nt = TT // TM
    with_ctx = o_ctx is not None
    consts = [_const_spec((8, D)), _const_spec((D, D))]
    if with_ctx:
        tile = pl.BlockSpec((1, TM, D), lambda b, t: (b, t, 0))
        in_specs = [pl.BlockSpec((1, TM, D), lambda b, t: (b, jnp.maximum(t - 1, 0), 0)),
                    pl.BlockSpec((1, TM, D), lambda b, t: (b, 0, 0)),
                    tile, tile,
                    pl.BlockSpec((1, 1, 1, 3 * D), lambda b, t: (b, jnp.minimum(t, 1), 0, 0))] + consts
        args = (o_lat, o_ctx, g, xc, mod, vec, w_out)
        grid, out_spec, out_rows = (B, nt), tile, TT
    else:
        lat = pl.BlockSpec((1, TM, D), lambda b, t: (b, t + 1, 0))
        out_spec = pl.BlockSpec((1, TM, D), lambda b, t: (b, t, 0))
        in_specs = [out_spec, lat, lat,
                    pl.BlockSpec((1, 1, 1, 3 * D), lambda b, t: (b, 1, 0, 0))] + consts
        args = (o_lat, g, xc, mod, vec, w_out)
        grid, out_rows = (B, nt - 1), TT - TM
    return pl.pallas_call(
        functools.partial(_na_out_kernel, with_ctx),
        grid=grid,
        in_specs=in_specs,
        out_specs=out_spec,
        out_shape=jax.ShapeDtypeStruct((B, out_rows, D), F32),
        compiler_params=_params(("parallel", "parallel")),
        name="na_out",
    )(*args)


def _na_bias_table(rpb):
    ncol = 2 * WIN_W - 1
    j = np.arange(GRID_W)
    win_start = np.clip(j - WIN_W // 2, 0, GRID_W - WIN_W)
    kcol = np.arange(GRID_W)
    valid = (kcol[None, :] >= win_start[:, None]) & (kcol[None, :] < win_start[:, None] + WIN_W)
    dc = np.clip(kcol[None, :] - j[:, None], -(WIN_W - 1), WIN_W - 1) + (WIN_W - 1)
    onehot = jnp.asarray(dc[None] == np.arange(ncol)[:, None, None], F32)
    base = jnp.einsum('lhrc,cqk->lhqrk', rpb.astype(F32), onehot, precision=lax.Precision.HIGHEST)
    base = jnp.where(jnp.asarray(valid)[None, None, :, None, :], base, NEG_INF)
    halves = []
    for hl in range(2):
        b = base[:, hl::2]
        per_off = [b[:, :, :, WIN_H - 1 - off:2 * WIN_H - 1 - off, :].reshape(b.shape[0], NPAIR, GRID_W, WIN_H * GRID_W)
                   for off in range(WIN_H)]
        halves.append(jnp.stack(per_off, axis=2))
    return jnp.concatenate(halves, axis=3)


def _pad_rows(m, rows):
    return jnp.pad(m, ((0, rows - m.shape[0]), (0, 0)))


def kernel(x, c, ctx, c_ctx, ada_w, ada_b, pre_g, post_g, rw_mu, rw_w_rkvg, rw_w0, rw_w1, rw_w2, rw_a0, rw_a1, rw_a2, rw_v0, rw_v1, rw_v2, rw_k_k, rw_k_a, rw_r_k, rw_lnx_w, rw_lnx_b, rw_w_out, na_w_in, na_b_in, na_rpb, na_w_out):
    B, T, _ = x.shape
    depth = ada_w.shape[0]
    assert ctx.shape[1] == TM and T % (RQ * GRID_W) == 0 and T % TM == 0 and T // GRID_W >= WIN_H

    seg = (np.arange(D)[:, None] // HD) == np.arange(PW)[None, :]
    e = jnp.asarray(seg, BF16)
    et = jnp.asarray(seg.T, BF16)

    cond = jnp.concatenate([c, c_ctx[None, :]], axis=0)
    nrow = -(-(B + 1) // 8) * 8
    cond = _pad_rows(cond * jax.nn.sigmoid(cond), nrow)
    mod_all = _adaln(cond, ada_w, ada_b)
    mod_ctx = jnp.broadcast_to(mod_all[:, B:B + 1], (depth, B, 3 * D))
    mod_all = jnp.stack([mod_ctx, mod_all[:, :B]], axis=2)[:, :, :, None, :]

    bias_all = _na_bias_table(na_rpb)
    xc = (ctx, x)
    v_first = None
    for i in range(depth):
        last = i == depth - 1
        j = i // 2
        mod = mod_all[i]
        if i % 2 == 0:
            zero = jnp.zeros((D,), F32)
            vec = jnp.stack([pre_g[i], *rw_mu[j], rw_k_k[j], rw_k_a[j], rw_r_k[j].reshape(D),
                             rw_w0[j, 0], rw_w0[j, 1], rw_a0[j, 0], rw_a0[j, 1],
                             rw_v0[j - 1] if j > 0 else zero, zero])
            lora = D // 16
            w1c = _bf(jnp.concatenate([rw_w1[j, 0], rw_w1[j, 1]], axis=1))
            a1c = _bf(jnp.concatenate([rw_a1[j, 0], rw_a1[j, 1]], axis=1))
            zl = jnp.zeros((lora, D), F32)
            w2z = _bf(jnp.stack([jnp.concatenate([rw_w2[j, 0], zl]), jnp.concatenate([zl, rw_w2[j, 1]])]))
            a2z = _bf(jnp.stack([jnp.concatenate([rw_a2[j, 0], zl]), jnp.concatenate([zl, rw_a2[j, 1]])]))
            vres = None
            if j > 0:
                v1p = _bf(jnp.pad(rw_v1[j - 1], ((0, 0), (0, PW - rw_v1.shape[-1]))))
                v2p = _bf(_pad_rows(rw_v2[j - 1], PW))
                vres = (v1p, v2p, v_first)
            wq = _bf(rw_w_rkvg[j])
            r, v, kk, g, bonus, kd0, kd1, b0, b1, lw0, lw1 = _rwkv_proj(
                xc, mod, vec, wq[0], wq[1], wq[2], wq[3], w1c, w2z, a1c, a2z, e, et, vres)
            if j == 0:
                v_first = v
            y0, y1 = _rwkv_scan(r, kk, v, kd0, b0, lw0, kd1, b1, lw1)
            vec_o = _pad_rows(jnp.stack([rw_lnx_w[j], rw_lnx_b[j], post_g[i]]), 8)
            xc = _rwkv_out(y0, y1, bonus, g, xc, mod, vec_o, _bf(rw_w_out[j]), e, et)
        else:
            vec = _pad_rows(jnp.stack([pre_g[i], post_g[i]]), 8)
            q, k, v, g = _na_proj(xc, mod, vec, _bf(na_w_in[j]), na_b_in[j][None, :])
            o_lat = _na_attn(q, k, v, bias_all[j])
            o_ctx = None if last else _ctx_attn(q, k, v)
            xc = _na_out(o_lat, o_ctx, g, xc, mod, vec, _bf(na_w_out[j]))
    return xc if xc.shape[1] == T else xc[:, TM:]
```

```python
import functools
import math

import numpy as np
import jax
import jax.numpy as jnp
from jax import lax
from jax.experimental import pallas as pl
from jax.experimental.pallas import tpu as pltpu

F32 = jnp.float32
BF16 = jnp.bfloat16

D = 1024
HD = 64
NH = D // HD
PW = 2 * HD
NPAIR = D // PW
TM = 256
CH = 64
GRID_W = 64
WIN_H = 8
WIN_W = 16
RMS_EPS = 1e-6
LNX_EPS = 64e-5
NEG_INF = -1e30
EXP_M05 = math.exp(-0.5)
RQ = 8
SCAN_PAIRS = 4
ATT_PAIRS = 2
ATT_GROUP = 4
VMEM_LIMIT = 56 * 1024 * 1024


def _bf(x):
    return x.astype(BF16)


def _dot(a, b):
    return jnp.dot(a, b, preferred_element_type=F32)


def _dot_nt(a, b):
    return lax.dot_general(a, b, (((1,), (1,)), ((), ())), preferred_element_type=F32)


def _dot_tn(a, b):
    return lax.dot_general(a, b, (((0,), (0,)), ((), ())), preferred_element_type=F32)


def _dot_split(x, e):
    hi = _bf(x)
    lo = _bf(x - hi.astype(F32))
    return _dot(hi, e) + _dot(lo, e)


def _seg_reduce(x, e, two_pass=False):
    return _dot_split(x, e) if two_pass else _dot(_bf(x), e)


def _seg_expand(c, et):
    return _dot_split(c, et)


def _sigmoid(x):
    return 0.5 * jnp.tanh(0.5 * x) + 0.5


def _prenorm(x, g, scale, shift):
    ms = jnp.mean(x * x, axis=-1, keepdims=True)
    return (x * lax.rsqrt(ms + RMS_EPS) * g) * (1.0 + scale) + shift


def _const_spec(shape):
    nd = len(shape)
    return pl.BlockSpec(shape, lambda *_: (0,) * nd, pipeline_mode=pl.Buffered(1))


def _params(sem):
    return pltpu.CompilerParams(dimension_semantics=sem, vmem_limit_bytes=VMEM_LIMIT)


def _adaln_kernel(s_ref, w_ref, b_ref, o_ref):
    s = s_ref[...]
    w = w_ref[0]
    hi = _bf(s)
    lo = _bf(s - hi.astype(F32))
    whi = _bf(w)
    wlo = _bf(w - whi.astype(F32))
    o_ref[0] = _dot(hi, whi) + _dot(lo, whi) + _dot(hi, wlo) + b_ref[0]


def _adaln(silu_rows, ada_w, ada_b):
    depth = ada_w.shape[0]
    nrow = silu_rows.shape[0]
    return pl.pallas_call(
        _adaln_kernel,
        grid=(depth, 3),
        in_specs=[pl.BlockSpec((nrow, D), lambda i, j: (0, 0)),
                  pl.BlockSpec((1, D, D), lambda i, j: (i, 0, j)),
                  pl.BlockSpec((1, 1, D), lambda i, j: (i, 0, j))],
        out_specs=pl.BlockSpec((1, nrow, D), lambda i, j: (i, 0, j)),
        out_shape=jax.ShapeDtypeStruct((depth, nrow, 3 * D), F32),
        compiler_params=_params(("parallel", "parallel")),
        name="adaln",
    )(silu_rows, ada_w, ada_b.reshape(depth, 1, 3 * D))


_V_PRE_G, _V_MU, _V_KK, _V_KA, _V_RK, _V_W0, _V_A0, _V_V0 = 0, 1, 7, 8, 9, 10, 12, 14


def _rwkv_proj_kernel(has_vres, split, nt, *refs):
    if split:
        ctx_ref, refs = refs[0], refs[1:]
    if has_vres:
        (x_ref, xp_ref, xn_ref, mod_ref, vec_ref, wr_ref, wk_ref, wv_ref, wg_ref, w1_ref, w2_ref,
         a1_ref, a2_ref, e_ref, et_ref, v1_ref, v2_ref, vf_ref,
         r_ref, v_ref, kk_ref, g_ref, bonus_ref, kd0_ref, kd1_ref, b0_ref, b1_ref, lw0_ref, lw1_ref) = refs
    else:
        (x_ref, xp_ref, xn_ref, mod_ref, vec_ref, wr_ref, wk_ref, wv_ref, wg_ref, w1_ref, w2_ref,
         a1_ref, a2_ref, e_ref, et_ref,
         r_ref, v_ref, kk_ref, g_ref, bonus_ref, kd0_ref, kd1_ref, b0_ref, b1_ref, lw0_ref, lw1_ref) = refs
    t = pl.program_id(1)
    shift = mod_ref[0, 0, :, 0:D]
    scale = mod_ref[0, 0, :, D:2 * D]
    g_pre = vec_ref[_V_PRE_G:_V_PRE_G + 1]

    x = jnp.where(t == 0, ctx_ref[0], x_ref[0]) if split else x_ref[0]
    h = _prenorm(x, g_pre, scale, shift)
    hp = _prenorm(xp_ref[0], g_pre, scale, shift)[7:8]
    hn = _prenorm(xn_ref[0], g_pre, scale, shift)[0:1]
    hp = jnp.where(t >= 2, hp, 0.0)
    hn = jnp.where(jnp.logical_and(t >= 1, t < nt - 1), hn, 0.0)
    row = lax.broadcasted_iota(jnp.int32, (TM, 1), 0)
    h_m1 = jnp.where(row == 0, hp, pltpu.roll(h, 1, 0))
    h_p1 = jnp.where(row == TM - 1, hn, pltpu.roll(h, TM - 1, 0))
    xx = 0.5 * (h_m1 + h_p1) - h

    def lerp(n):
        return _bf(h + xx * vec_ref[_V_MU + n:_V_MU + n + 1])

    r = _dot(lerp(0), wr_ref[...])
    k = _dot(lerp(2), wk_ref[...])
    xv = lerp(3)
    v = _dot(xv, wv_ref[...])
    g = _dot(lerp(5), wg_ref[...])
    g = g * _sigmoid(g)
    if has_vres:
        lv = _dot(_bf(_dot(xv, v1_ref[...])), v2_ref[...])
        v = v + (vf_ref[0] - v) * _sigmoid(vec_ref[_V_V0:_V_V0 + 1] + lv)
    tw = _bf(jnp.tanh(_dot(lerp(1), w1_ref[...])))
    la = _bf(_dot(lerp(4), a1_ref[...]))

    e = e_ref[...]
    et = et_ref[...]
    kkr = k * vec_ref[_V_KK:_V_KK + 1]
    kk = kkr * _seg_expand(jnp.minimum(lax.rsqrt(_seg_reduce(kkr * kkr, e)), 1e12), et)
    k_a = vec_ref[_V_KA:_V_KA + 1]

    ksum = None
    for d, (kd_ref, b_ref, lw_ref) in enumerate(((kd0_ref, b0_ref, lw0_ref), (kd1_ref, b1_ref, lw1_ref))):
        wl = vec_ref[_V_W0 + d:_V_W0 + d + 1] + _dot(tw, w2_ref[d])
        lw_ref[0] = -EXP_M05 * _sigmoid(wl)
        a = _sigmoid(vec_ref[_V_A0 + d:_V_A0 + d + 1] + _dot(la, a2_ref[d]))
        kd = k * (1.0 + (a - 1.0) * k_a)
        kd_ref[0] = kd
        b_ref[0] = kk * a
        ksum = kd if ksum is None else ksum + kd

    r_ref[0] = r
    v_ref[0] = v
    kk_ref[0] = kk
    g_ref[0] = _bf(g)
    bonus_ref[0] = _bf(_seg_expand(_seg_reduce(r * ksum * vec_ref[_V_RK:_V_RK + 1], e), et) * v)


def _stream_specs(stream):
    if isinstance(stream, tuple):
        ctx, x = stream
        B, T, _ = x.shape
        specs = [pl.BlockSpec((1, TM, D), lambda b, t: (b, 0, 0)),
                 pl.BlockSpec((1, TM, D), lambda b, t: (b, jnp.maximum(t - 1, 0), 0))]
        return specs, [ctx, x], B, T + TM, 1
    B, TT, _ = stream.shape
    return [pl.BlockSpec((1, TM, D), lambda b, t: (b, t, 0))], [stream], B, TT, 0


def _rwkv_proj(stream, mod, vec, wr, wk, wv, wg, w1c, w2z, a1c, a2z, e, et, vres):
    cur_specs, cur_args, B, TT, lat0 = _stream_specs(stream)
    split = lat0 == 1
    nt = TT // TM
    has_vres = vres is not None
    tile = pl.BlockSpec((1, TM, D), lambda b, t: (b, t, 0))
    rows8 = (TT - lat0 * TM) // 8
    in_specs = cur_specs + [
        pl.BlockSpec((1, 8, D), lambda b, t: (b, jnp.maximum((t - lat0) * (TM // 8) - 1, 0), 0)),
        pl.BlockSpec((1, 8, D), lambda b, t: (b, jnp.minimum((t + 1 - lat0) * (TM // 8), rows8 - 1), 0)),
        pl.BlockSpec((1, 1, 1, 3 * D), lambda b, t: (b, jnp.minimum(t, 1), 0, 0)),
        _const_spec((16, D)),
        _const_spec((D, D)), _const_spec((D, D)), _const_spec((D, D)), _const_spec((D, D)),
        _const_spec((D, PW)), _const_spec((2, PW, D)), _const_spec((D, PW)), _const_spec((2, PW, D)),
        _const_spec((D, PW)), _const_spec((PW, D)),
    ]
    args = cur_args + [cur_args[-1], cur_args[-1], mod, vec, wr, wk, wv, wg, w1c, w2z, a1c, a2z, e, et]
    if has_vres:
        v1p, v2p, v_first = vres
        in_specs += [_const_spec((D, PW)), _const_spec((PW, D)), tile]
        args += [v1p, v2p, v_first]
    out = jax.ShapeDtypeStruct((B, TT, D), F32)
    half = jax.ShapeDtypeStruct((B, TT, D), BF16)
    return pl.pallas_call(
        functools.partial(_rwkv_proj_kernel, has_vres, split, nt),
        grid=(B, nt),
        in_specs=in_specs,
        out_specs=[tile] * 11,
        out_shape=[out] * 3 + [half] * 2 + [out] * 6,
        compiler_params=_params(("parallel", "parallel")),
        name="rwkv_proj",
    )(*args)


def _cumsum_rows(x, rev):
    row = lax.broadcasted_iota(jnp.int32, (CH, 1), 0)
    s = 1
    while s < CH:
        if rev:
            x = x + jnp.where(row < CH - s, pltpu.roll(x, CH - s, 0), 0.0)
        else:
            x = x + jnp.where(row >= s, pltpu.roll(x, s, 0), 0.0)
        s *= 2
    return x


def _tri_masks(rev):
    ri = lax.broadcasted_iota(jnp.int32, (PW, PW), 0)
    ci = lax.broadcasted_iota(jnp.int32, (PW, PW), 1)
    same = (ri >> 6) == (ci >> 6)
    rl = ri & (CH - 1)
    cl = ci & (CH - 1)
    if rev:
        return jnp.logical_and(same, cl > rl), jnp.logical_and(same, cl >= rl)
    return jnp.logical_and(same, cl < rl), jnp.logical_and(same, cl <= rl)


def _scan_kernel(*refs):
    in_refs = (refs[0:6], refs[6:12])
    y_refs = refs[12:14]
    st_ref, q2_ref, g_ref, y0_ref, h_ref = refs[14:19]

    @pl.when(pl.program_id(2) == 0)
    def _():
        for ref in (st_ref, q2_ref, g_ref, y0_ref, h_ref):
            ref[...] = jnp.zeros_like(ref)

    nch = TM // CH
    chains = [(d, p, c) for d in (0, 1) for p in range(SCAN_PAIRS)
              for c in (range(nch - 1, -1, -1) if d else range(nch))]
    n = len(chains)
    nst = 2 * SCAN_PAIRS
    st = [st_ref[j] for j in range(nst)]

    def recurrence_round(k):
        for j in range(nst):
            i = j * nch + k
            d, p, c = chains[i]
            sb = _bf(st[j])
            y2 = _dot(q2_ref[i], sb) + y0_ref[i]
            st[j] = _dot(g_ref[i], sb) + h_ref[i]
            y_refs[d][0, pl.ds(c * CH, CH), p * PW:(p + 1) * PW] = _bf(y2[:CH] + y2[CH:])

    lane = lax.broadcasted_iota(jnp.int32, (1, PW), 1)
    m0 = (lane < HD).astype(F32)
    m1 = 1.0 - m0
    ri = lax.broadcasted_iota(jnp.int32, (PW, PW), 0)
    ci = lax.broadcasted_iota(jnp.int32, (PW, PW), 1)
    eye = (ri == ci).astype(F32)
    masks = (_tri_masks(False), _tri_masks(True))
    same = (ri >> 6) == (ci >> 6)

    def stack(x):
        return jnp.concatenate([x * m0, x * m1], axis=0)

    g = {}

    def prepare(g):
        g.update(a2=[], r2=[], v2=[], bk=[], wend=[], m=[])
        for d, p, c in chains:
            r_ref, kk_ref, v_ref, k_ref, b_ref, lw_ref = in_refs[d]
            sl = (0, pl.ds(c * CH, CH), slice(p * PW, (p + 1) * PW))
            lw = lw_ref[sl]
            cum = _cumsum_rows(lw, bool(d))
            last = cum[0:1] if d else cum[CH - 1:CH]
            wend = jnp.exp(last)
            e_n = jnp.exp(-cum)
            e_h = wend * e_n
            b = b_ref[sl]
            k = k_ref[sl]
            a2 = stack(-kk_ref[sl] * jnp.exp(cum - lw))
            r2 = stack(r_ref[sl] * jnp.exp(cum))
            g["a2"].append(a2)
            g["r2"].append(r2)
            g["v2"].append(stack(v_ref[sl]))
            bh = _bf(b * e_h)
            kh = _bf(k * e_h)
            g["bk"].append(jnp.concatenate([bh, bh, kh, kh], axis=0))
            g["wend"].append(wend)
            bt = _bf(b * e_n)
            kt = _bf(k * e_n)
            g["m"].append(_dot_nt(_bf(jnp.concatenate([a2, r2], axis=0)),
                                  jnp.concatenate([bt, bt, kt, kt], axis=0)))

    def mask(g):
        strict = [masks[d][0] for d, _, _ in chains]
        incl = [jnp.concatenate([masks[d][1]] * 2, axis=1) for d, _, _ in chains]
        m_ab = [jnp.where(s, m[:PW, :PW], 0.0) for s, m in zip(strict, g["m"])]
        g["m_ak"] = [_bf(jnp.where(s, m[:PW, PW:], 0.0)) for s, m in zip(strict, g["m"])]
        g["m_r"] = [_bf(jnp.where(s, m[PW:, :], 0.0)) for s, m in zip(incl, g["m"])]
        g["q"] = [eye + x for x in m_ab]
        lb = [_bf(x) for x in m_ab]
        g["l"] = [_dot(x, x) for x in lb]

    def double(g):
        lb = [_bf(x) for x in g["l"]]
        res = [_dot(x, jnp.concatenate([x, _bf(q)], axis=1)) for x, q in zip(lb, g["q"])]
        g["l"] = [x[:, :PW] for x in res]
        g["q"] = [q + x[:, PW:] for q, x in zip(g["q"], res)]

    def invert(g):
        g["tinv"] = [q + _dot(_bf(l), _bf(q)) for q, l in zip(g["q"], g["l"])]
        g["mv"] = [_dot(x, _bf(v)) for x, v in zip(g["m_ak"], g["v2"])]

    def solve(g):
        g["pu"] = [_dot(_bf(t), _bf(jnp.concatenate([a, mv], axis=1)))
                   for t, a, mv in zip(g["tinv"], g["a2"], g["mv"])]

    def combine(g):
        rhs = [_bf(jnp.concatenate([jnp.concatenate([pu[:, PW:], pu[:, :PW]], axis=1),
                                    jnp.concatenate([v, jnp.zeros_like(v)], axis=1)], axis=0))
               for pu, v in zip(g["pu"], g["v2"])]
        g["yp"] = [_dot(x, y) for x, y in zip(g["m_r"], rhs)]
        same2 = jnp.concatenate([same, same], axis=1)
        g["gh"] = [jnp.where(same2, _dot_tn(bk, y), 0.0) for bk, y in zip(g["bk"], rhs)]

    def publish(g):
        for i in range(n):
            g_ref[i] = _bf(g["gh"][i][:, PW:] + eye * g["wend"][i])
            q2_ref[i] = _bf(g["r2"][i] + g["yp"][i][:, PW:])
            y0_ref[i] = g["yp"][i][:, :PW]
            h_ref[i] = g["gh"][i][:, :PW]

    stages = [prepare, mask, double, double, double, double, invert, solve, combine, publish]
    for t, stage in enumerate(stages):
        if t < nch:
            recurrence_round(t)
        if t == nch:
            for j in range(nst):
                st_ref[j] = st[j]
        stage(g)


def _rwkv_scan(r, kk, v, kd0, b0, lw0, kd1, b1, lw1):
    B, TT, _ = r.shape
    nt = TT // TM
    width = SCAN_PAIRS * PW

    def bwd_tile(j):
        return jnp.where(j == 0, 0, nt - j)

    fwd_in = pl.BlockSpec((1, TM, width), lambda b, p, i: (b, jnp.minimum(i, nt - 1), p))
    bwd_in = pl.BlockSpec((1, TM, width), lambda b, p, i: (b, bwd_tile(jnp.minimum(i, nt - 1)), p))
    fwd_out = pl.BlockSpec((1, TM, width), lambda b, p, i: (b, jnp.maximum(i - 1, 0), p))
    bwd_out = pl.BlockSpec((1, TM, width), lambda b, p, i: (b, bwd_tile(jnp.maximum(i - 1, 0)), p))
    out = jax.ShapeDtypeStruct((B, TT, D), BF16)
    n = 2 * SCAN_PAIRS * (TM // CH)
    return pl.pallas_call(
        _scan_kernel,
        grid=(B, NPAIR // SCAN_PAIRS, nt + 1),
        in_specs=[fwd_in] * 6 + [bwd_in] * 6,
        out_specs=[fwd_out, bwd_out],
        out_shape=[out, out],
        scratch_shapes=[pltpu.VMEM((2 * SCAN_PAIRS, PW, PW), F32),
                        pltpu.VMEM((n, PW, PW), BF16), pltpu.VMEM((n, PW, PW), BF16),
                        pltpu.VMEM((n, PW, PW), F32), pltpu.VMEM((n, PW, PW), F32)],
        compiler_params=_params(("parallel", "parallel", "arbitrary")),
        name="rwkv_scan",
    )(r, kk, v, kd0, b0, lw0, r, kk, v, kd1, b1, lw1)


def _out_tail(o, w_ref, post_g, gate, x):
    out = _dot(_bf(o), w_ref[...])
    ms = jnp.mean(out * out, axis=-1, keepdims=True)
    return x + gate * (out * lax.rsqrt(ms + RMS_EPS) * post_g)


def _rwkv_out_kernel(split, y0_ref, y1_ref, bonus_ref, g_ref, *refs):
    if split:
        ctx_ref, refs = refs[0], refs[1:]
    x_ref, mod_ref, vec_ref, w_ref, e_ref, et_ref, o_ref = refs
    e = e_ref[...]
    et = et_ref[...]
    y = y0_ref[0].astype(F32) + y1_ref[0].astype(F32)
    yc = y - _seg_expand(_seg_reduce(y, e, two_pass=True) * (1.0 / HD), et)
    var = _seg_reduce(yc * yc, e) * (1.0 / HD)
    yn = yc * _seg_expand(lax.rsqrt(var + LNX_EPS), et) * vec_ref[0:1] + vec_ref[1:2]
    o = (yn + bonus_ref[0].astype(F32)) * g_ref[0].astype(F32)
    x = jnp.where(pl.program_id(1) == 0, ctx_ref[0], x_ref[0]) if split else x_ref[0]
    o_ref[0] = _out_tail(o, w_ref, vec_ref[2:3], mod_ref[0, 0, :, 2 * D:3 * D], x)


def _rwkv_out(y0, y1, bonus, g, stream, mod, vec, w_out, e, et):
    cur_specs, cur_args, B, TT, lat0 = _stream_specs(stream)
    nt = TT // TM
    tile = pl.BlockSpec((1, TM, D), lambda b, t: (b, t, 0))
    return pl.pallas_call(
        functools.partial(_rwkv_out_kernel, lat0 == 1),
        grid=(B, nt),
        in_specs=[tile] * 4 + cur_specs + [
            pl.BlockSpec((1, 1, 1, 3 * D), lambda b, t: (b, jnp.minimum(t, 1), 0, 0)),
            _const_spec((8, D)), _const_spec((D, D)), _const_spec((D, PW)), _const_spec((PW, D))],
        out_specs=tile,
        out_shape=jax.ShapeDtypeStruct((B, TT, D), F32),
        compiler_params=_params(("parallel", "parallel")),
        name="rwkv_out",
    )(y0, y1, bonus, g, *cur_args, mod, vec, w_out, e, et)


def _na_proj_kernel(x_ref, mod_ref, vec_ref, w_ref, bias_ref, q_ref, k_ref, v_ref, g_ref):
    shift = mod_ref[0, 0, :, 0:D]
    scale = mod_ref[0, 0, :, D:2 * D]
    h = _bf(_prenorm(x_ref[0], vec_ref[0:1], scale, shift))
    q = _dot(h, w_ref[:, 0:D]) + bias_ref[:, 0:D]
    q_ref[0] = _bf(q * (HD ** -0.5))
    k_ref[0] = _bf(_dot(h, w_ref[:, D:2 * D]) + bias_ref[:, D:2 * D])
    v_ref[0] = _bf(_dot(h, w_ref[:, 2 * D:3 * D]) + bias_ref[:, 2 * D:3 * D])
    g = _dot(h, w_ref[:, 3 * D:4 * D]) + bias_ref[:, 3 * D:4 * D]
    g_ref[0] = _bf(g * _sigmoid(g))


def _na_proj(xc, mod, vec, w_in, b_in):
    B, TT, _ = xc.shape
    nt = TT // TM
    tile = pl.BlockSpec((1, TM, D), lambda b, t: (b, t, 0))
    bf = jax.ShapeDtypeStruct((B, TT, D), BF16)
    return pl.pallas_call(
        _na_proj_kernel,
        grid=(B, nt),
        in_specs=[tile,
                  pl.BlockSpec((1, 1, 1, 3 * D), lambda b, t: (b, jnp.minimum(t, 1), 0, 0)),
                  _const_spec((8, D)), _const_spec((D, 4 * D)), _const_spec((1, 4 * D))],
        out_specs=[pl.BlockSpec((1, TM, D), lambda b, t: (b, jnp.where(t == 0, nt - 1, t - 1), 0))] + [tile] * 3,
        out_shape=[bf] * 4,
        compiler_params=_params(("parallel", "parallel")),
        name="na_proj",
    )(xc, mod, vec, w_in, b_in)


def _softmax_pv(s_list, v_list):
    mx = None
    for s in s_list:
        m = jnp.max(s, axis=-1, keepdims=True)
        mx = m if mx is None else jnp.maximum(mx, m)
    den = None
    acc = None
    for s, v in zip(s_list, v_list):
        p = jnp.exp(s - mx)
        d = jnp.sum(p, axis=-1, keepdims=True)
        o = _dot(_bf(p), v)
        den = d if den is None else den + d
        acc = o if acc is None else acc + o
    return acc / den


def _na_attn_kernel(rows, q_ref, k_ref, v_ref, bias_ref, o_ref):
    lane = lax.broadcasted_iota(jnp.int32, (1, PW), 1)
    m0 = lane < HD
    step = pl.program_id(2)
    nkeys = WIN_H * GRID_W
    zero = jnp.zeros((), BF16)
    off, start = [], []
    for rr in range(RQ):
        r = step * RQ + rr
        rs = jnp.clip(r - WIN_H // 2, 0, rows - WIN_H)
        off.append(r - rs)
        start.append(pl.multiple_of(TM + rs * GRID_W, GRID_W))
    groups = [dict(pp=pp, lanes=slice(pp * PW, (pp + 1) * PW), rows=list(range(r0, r0 + ATT_GROUP)))
              for pp in range(ATT_PAIRS) for r0 in range(0, RQ, ATT_GROUP)]

    def scores(g):
        q2 = []
        for rr in g["rows"]:
            q = q_ref[0, rr * GRID_W:(rr + 1) * GRID_W, g["lanes"]]
            q2.append(jnp.concatenate([jnp.where(m0, q, zero), jnp.where(m0, zero, q)], axis=0))
        g["s_ctx"] = _dot_nt(jnp.concatenate(q2, axis=0), k_ref[0, 0:TM, g["lanes"]])
        g["s_win"] = [_dot_nt(q2[j], k_ref[0, pl.ds(start[rr], nkeys), g["lanes"]]) + bias_ref[g["pp"], off[rr]]
                      for j, rr in enumerate(g["rows"])]

    def softmax(g):
        n = len(g["rows"])
        mx_ctx = jnp.max(g["s_ctx"], axis=-1, keepdims=True)
        mx = [jnp.maximum(jnp.max(g["s_win"][j], axis=-1, keepdims=True), mx_ctx[j * PW:(j + 1) * PW])
              for j in range(n)]
        g["p_win"] = [jnp.exp(g["s_win"][j] - mx[j]) for j in range(n)]
        g["p_ctx"] = jnp.exp(g["s_ctx"] - jnp.concatenate(mx, axis=0))

    def weighted_sum(g):
        den_ctx = jnp.sum(g["p_ctx"], axis=-1, keepdims=True)
        o_ctx = _dot(_bf(g["p_ctx"]), v_ref[0, 0:TM, g["lanes"]])
        for j, rr in enumerate(g["rows"]):
            den = jnp.sum(g["p_win"][j], axis=-1, keepdims=True) + den_ctx[j * PW:(j + 1) * PW]
            vw = v_ref[0, pl.ds(start[rr], nkeys), g["lanes"]]
            o2 = (_dot(_bf(g["p_win"][j]), vw) + o_ctx[j * PW:(j + 1) * PW]) * (1.0 / den)
            o_ref[0, rr * GRID_W:(rr + 1) * GRID_W, g["lanes"]] = _bf(jnp.where(m0, o2[:GRID_W], o2[GRID_W:]))

    stages = (scores, softmax, weighted_sum)
    for t in range(len(groups) + len(stages) - 1):
        for k, stage in enumerate(stages):
            if 0 <= t - k < len(groups):
                stage(groups[t - k])


def _na_attn(q, k, v, bias):
    B, TT, _ = q.shape
    T = TT - TM
    rows = T // GRID_W
    qb = RQ * GRID_W
    width = ATT_PAIRS * PW
    kv = pl.BlockSpec((1, TT, width), lambda b, p, s: (b, 0, p))
    return pl.pallas_call(
        functools.partial(_na_attn_kernel, rows),
        grid=(B, NPAIR // ATT_PAIRS, rows // RQ),
        in_specs=[pl.BlockSpec((1, qb, width), lambda b, p, s: (b, s, p)), kv, kv,
                  pl.BlockSpec((ATT_PAIRS, WIN_H, PW, WIN_H * GRID_W), lambda b, p, s: (p, 0, 0, 0))],
        out_specs=pl.BlockSpec((1, qb, width), lambda b, p, s: (b, s, p)),
        out_shape=jax.ShapeDtypeStruct((B, T, D), BF16),
        compiler_params=_params(("parallel", "parallel", "arbitrary")),
        name="na_attn",
    )(q, k, v, bias)


def _ctx_attn_kernel(q_ref, k_ref, v_ref, o_ref):
    lane = lax.broadcasted_iota(jnp.int32, (1, PW), 1)
    m0 = lane < HD
    zero = jnp.zeros((), BF16)
    q = q_ref[0]
    q2 = jnp.concatenate([jnp.where(m0, q, zero), jnp.where(m0, zero, q)], axis=0)
    kc = k_ref[0]
    o2 = _softmax_pv([_dot_nt(q2, kc)], [v_ref[0]])
    o_ref[0] = _bf(jnp.where(m0, o2[:TM], o2[TM:]))


def _ctx_attn(q, k, v):
    B, TT, _ = q.shape
    blk = pl.BlockSpec((1, TM, PW), lambda b, p: (b, 0, p))
    q_ctx = pl.BlockSpec((1, TM, PW), lambda b, p: (b, TT // TM - 1, p))
    return pl.pallas_call(
        _ctx_attn_kernel,
        grid=(B, NPAIR),
        in_specs=[q_ctx, blk, blk],
        out_specs=blk,
        out_shape=jax.ShapeDtypeStruct((B, TM, D), BF16),
        compiler_params=_params(("parallel", "parallel")),
        name="ctx_attn",
    )(q, k, v)


def _na_out_kernel(with_ctx, *refs):
    if with_ctx:
        ol_ref, oc_ref, g_ref, x_ref, mod_ref, vec_ref, w_ref, o_ref = refs
        o = jnp.where(pl.program_id(1) == 0, oc_ref[0], ol_ref[0])
    else:
        ol_ref, g_ref, x_ref, mod_ref, vec_ref, w_ref, o_ref = refs
        o = ol_ref[0]
    o = o.astype(F32) * g_ref[0].astype(F32)
    o_ref[0] = _out_tail(o, w_ref, vec_ref[1:2], mod_ref[0, 0, :, 2 * D:3 * D], x_ref[0])


def _na_out(o_lat, o_ctx, g, xc, mod, vec, w_out):
    B, TT, _ = xc.shape
    nt = TT // TM
    with_ctx = o_ctx is not None
    consts = [_const_spec((8, D)), _const_spec((D, D))]
    if with_ctx:
        tile = pl.BlockSpec((1, TM, D), lambda b, t: (b, t, 0))
        in_specs = [pl.BlockSpec((1, TM, D), lambda b, t: (b, jnp.maximum(t - 1, 0), 0)),
                    pl.BlockSpec((1, TM, D), lambda b, t: (b, 0, 0)),
                    tile, tile,
                    pl.BlockSpec((1, 1, 1, 3 * D), lambda b, t: (b, jnp.minimum(t, 1), 0, 0))] + consts
        args = (o_lat, o_ctx, g, xc, mod, vec, w_out)
        grid, out_spec, out_rows = (B, nt), tile, TT
    else:
        lat = pl.BlockSpec((1, TM, D), lambda b, t: (b, t + 1, 0))
        out_spec = pl.BlockSpec((1, TM, D), lambda b, t: (b, t, 0))
        in_specs = [out_spec, lat, lat,
                    pl.BlockSpec((1, 1, 1, 3 * D), lambda b, t: (b, 1, 0, 0))] + consts
        args = (o_lat, g, xc, mod, vec, w_out)
        grid, out_rows = (B, nt - 1), TT - TM
    return pl.pallas_call(
        functools.partial(_na_out_kernel, with_ctx),
        grid=grid,
        in_specs=in_specs,
        out_specs=out_spec,
        out_shape=jax.ShapeDtypeStruct((B, out_rows, D), F32),
        compiler_params=_params(("parallel", "parallel")),
        name="na_out",
    )(*args)


def _na_bias_table(rpb):
    ncol = 2 * WIN_W - 1
    j = np.arange(GRID_W)
    win_start = np.clip(j - WIN_W // 2, 0, GRID_W - WIN_W)
    kcol = np.arange(GRID_W)
    valid = (kcol[None, :] >= win_start[:, None]) & (kcol[None, :] < win_start[:, None] + WIN_W)
    dc = np.clip(kcol[None, :] - j[:, None], -(WIN_W - 1), WIN_W - 1) + (WIN_W - 1)
    onehot = jnp.asarray(dc[None] == np.arange(ncol)[:, None, None], F32)
    base = jnp.einsum('lhrc,cqk->lhqrk', rpb.astype(F32), onehot, precision=lax.Precision.HIGHEST)
    base = jnp.where(jnp.asarray(valid)[None, None, :, None, :], base, NEG_INF)
    halves = []
    for hl in range(2):
        b = base[:, hl::2]
        per_off = [b[:, :, :, WIN_H - 1 - off:2 * WIN_H - 1 - off, :].reshape(b.shape[0], NPAIR, GRID_W, WIN_H * GRID_W)
                   for off in range(WIN_H)]
        halves.append(jnp.stack(per_off, axis=2))
    return jnp.concatenate(halves, axis=3)


def _pad_rows(m, rows):
    return jnp.pad(m, ((0, rows - m.shape[0]), (0, 0)))


def kernel(x, c, ctx, c_ctx, ada_w, ada_b, pre_g, post_g, rw_mu, rw_w_rkvg, rw_w0, rw_w1, rw_w2, rw_a0, rw_a1, rw_a2, rw_v0, rw_v1, rw_v2, rw_k_k, rw_k_a, rw_r_k, rw_lnx_w, rw_lnx_b, rw_w_out, na_w_in, na_b_in, na_rpb, na_w_out):
    B, T, _ = x.shape
    depth = ada_w.shape[0]
    assert ctx.shape[1] == TM and T % (RQ * GRID_W) == 0 and T % TM == 0 and T // GRID_W >= WIN_H

    seg = (np.arange(D)[:, None] // HD) == np.arange(PW)[None, :]
    e = jnp.asarray(seg, BF16)
    et = jnp.asarray(seg.T, BF16)

    cond = jnp.concatenate([c, c_ctx[None, :]], axis=0)
    nrow = -(-(B + 1) // 8) * 8
    cond = _pad_rows(cond * jax.nn.sigmoid(cond), nrow)
    mod_all = _adaln(cond, ada_w, ada_b)
    mod_ctx = jnp.broadcast_to(mod_all[:, B:B + 1], (depth, B, 3 * D))
    mod_all = jnp.stack([mod_ctx, mod_all[:, :B]], axis=2)[:, :, :, None, :]

    bias_all = _na_bias_table(na_rpb)
    xc = (ctx, x)
    v_first = None
    for i in range(depth):
        last = i == depth - 1
        j = i // 2
        mod = mod_all[i]
        if i % 2 == 0:
            zero = jnp.zeros((D,), F32)
            vec = jnp.stack([pre_g[i], *rw_mu[j], rw_k_k[j], rw_k_a[j], rw_r_k[j].reshape(D),
                             rw_w0[j, 0], rw_w0[j, 1], rw_a0[j, 0], rw_a0[j, 1],
                             rw_v0[j - 1] if j > 0 else zero, zero])
            lora = D // 16
            w1c = _bf(jnp.concatenate([rw_w1[j, 0], rw_w1[j, 1]], axis=1))
            a1c = _bf(jnp.concatenate([rw_a1[j, 0], rw_a1[j, 1]], axis=1))
            zl = jnp.zeros((lora, D), F32)
            w2z = _bf(jnp.stack([jnp.concatenate([rw_w2[j, 0], zl]), jnp.concatenate([zl, rw_w2[j, 1]])]))
            a2z = _bf(jnp.stack([jnp.concatenate([rw_a2[j, 0], zl]), jnp.concatenate([zl, rw_a2[j, 1]])]))
            vres = None
            if j > 0:
                v1p = _bf(jnp.pad(rw_v1[j - 1], ((0, 0), (0, PW - rw_v1.shape[-1]))))
                v2p = _bf(_pad_rows(rw_v2[j - 1], PW))
                vres = (v1p, v2p, v_first)
            wq = _bf(rw_w_rkvg[j])
            r, v, kk, g, bonus, kd0, kd1, b0, b1, lw0, lw1 = _rwkv_proj(
                xc, mod, vec, wq[0], wq[1], wq[2], wq[3], w1c, w2z, a1c, a2z, e, et, vres)
            if j == 0:
                v_first = v
            y0, y1 = _rwkv_scan(r, kk, v, kd0, b0, lw0, kd1, b1, lw1)
            vec_o = _pad_rows(jnp.stack([rw_lnx_w[j], rw_lnx_b[j], post_g[i]]), 8)
            xc = _rwkv_out(y0, y1, bonus, g, xc, mod, vec_o, _bf(rw_w_out[j]), e, et)
        else:
            vec = _pad_rows(jnp.stack([pre_g[i], post_g[i]]), 8)
            q, k, v, g = _na_proj(xc, mod, vec, _bf(na_w_in[j]), na_b_in[j][None, :])
            o_lat = _na_attn(q, k, v, bias_all[j])
            o_ctx = None if last else _ctx_attn(q, k, v)
            xc = _na_out(o_lat, o_ctx, g, xc, mod, vec, _bf(na_w_out[j]))
    return xc if xc.shape[1] == T else xc[:, TM:]
```

```python
import functools
import math

import numpy as np
import jax
import jax.numpy as jnp
from jax import lax
from jax.experimental import pallas as pl
from jax.experimental.pallas import tpu as pltpu

F32 = jnp.float32
BF16 = jnp.bfloat16

D = 1024
HD = 64
NH = D // HD
PW = 2 * HD
NPAIR = D // PW
TM = 256
CH = 64
GRID_W = 64
WIN_H = 8
WIN_W = 16
RMS_EPS = 1e-6
LNX_EPS = 64e-5
NEG_INF = -1e30
EXP_M05 = math.exp(-0.5)
RQ = 8
SCAN_PAIRS = 4
ATT_PAIRS = 2
ATT_GROUP = 4
VMEM_LIMIT = 56 * 1024 * 1024


def _bf(x):
    return x.astype(BF16)


def _dot(a, b):
    return jnp.dot(a, b, preferred_element_type=F32)


def _dot_nt(a, b):
    return lax.dot_general(a, b, (((1,), (1,)), ((), ())), preferred_element_type=F32)


def _dot_tn(a, b):
    return lax.dot_general(a, b, (((0,), (0,)), ((), ())), preferred_element_type=F32)


def _dot_split(x, e):
    hi = _bf(x)
    lo = _bf(x - hi.astype(F32))
    return _dot(hi, e) + _dot(lo, e)


def _seg_reduce(x, e, two_pass=False):
    return _dot_split(x, e) if two_pass else _dot(_bf(x), e)


def _seg_expand(c, et):
    return _dot_split(c, et)


def _sigmoid(x):
    return 0.5 * jnp.tanh(0.5 * x) + 0.5


def _silu(x):
    half = 0.5 * x
    return half * jnp.tanh(half) + half


def _prenorm(x, g, scale, shift):
    ms = jnp.mean(x * x, axis=-1, keepdims=True)
    return x * lax.rsqrt(ms + RMS_EPS) * (g * (1.0 + scale)) + shift


def _const_spec(shape):
    nd = len(shape)
    return pl.BlockSpec(shape, lambda *_: (0,) * nd, pipeline_mode=pl.Buffered(1))


def _params(sem):
    return pltpu.CompilerParams(dimension_semantics=sem, vmem_limit_bytes=VMEM_LIMIT)


def _adaln_kernel(s_ref, w_ref, b_ref, o_ref):
    s = s_ref[...]
    w = w_ref[0]
    hi = _bf(s)
    lo = _bf(s - hi.astype(F32))
    whi = _bf(w)
    wlo = _bf(w - whi.astype(F32))
    o_ref[0] = _dot(hi, whi) + _dot(lo, whi) + _dot(hi, wlo) + b_ref[0]


def _adaln(silu_rows, ada_w, ada_b):
    depth = ada_w.shape[0]
    nrow = silu_rows.shape[0]
    return pl.pallas_call(
        _adaln_kernel,
        grid=(depth, 3),
        in_specs=[pl.BlockSpec((nrow, D), lambda i, j: (0, 0)),
                  pl.BlockSpec((1, D, D), lambda i, j: (i, 0, j)),
                  pl.BlockSpec((1, 1, D), lambda i, j: (i, 0, j))],
        out_specs=pl.BlockSpec((1, nrow, D), lambda i, j: (i, 0, j)),
        out_shape=jax.ShapeDtypeStruct((depth, nrow, 3 * D), F32),
        compiler_params=_params(("parallel", "parallel")),
        name="adaln",
    )(silu_rows, ada_w, ada_b.reshape(depth, 1, 3 * D))


_V_PRE_G, _V_MU, _V_KK, _V_KA, _V_RK, _V_W0, _V_A0, _V_V0 = 0, 1, 7, 8, 9, 10, 12, 14


def _rwkv_proj_kernel(has_vres, split, nt, *refs):
    if split:
        ctx_ref, refs = refs[0], refs[1:]
    if has_vres:
        (x_ref, xp_ref, xn_ref, mod_ref, vec_ref, wr_ref, wk_ref, wv_ref, wg_ref, w1_ref, w2_ref,
         a1_ref, a2_ref, e_ref, et_ref, v1_ref, v2_ref, vf_ref,
         r_ref, v_ref, kk_ref, g_ref, bonus_ref, kd0_ref, kd1_ref, b0_ref, b1_ref, lw0_ref, lw1_ref) = refs
    else:
        (x_ref, xp_ref, xn_ref, mod_ref, vec_ref, wr_ref, wk_ref, wv_ref, wg_ref, w1_ref, w2_ref,
         a1_ref, a2_ref, e_ref, et_ref,
         r_ref, v_ref, kk_ref, g_ref, bonus_ref, kd0_ref, kd1_ref, b0_ref, b1_ref, lw0_ref, lw1_ref) = refs
    t = pl.program_id(1)
    shift = mod_ref[0, 0, :, 0:D]
    scale = mod_ref[0, 0, :, D:2 * D]
    g_pre = vec_ref[_V_PRE_G:_V_PRE_G + 1]

    x = jnp.where(t == 0, ctx_ref[0], x_ref[0]) if split else x_ref[0]
    h = _prenorm(x, g_pre, scale, shift)
    hp = _prenorm(xp_ref[0], g_pre, scale, shift)[7:8]
    hn = _prenorm(xn_ref[0], g_pre, scale, shift)[0:1]
    hp = jnp.where(t >= 2, hp, 0.0)
    hn = jnp.where(jnp.logical_and(t >= 1, t < nt - 1), hn, 0.0)
    row = lax.broadcasted_iota(jnp.int32, (TM, 1), 0)
    h_m1 = jnp.where(row == 0, hp, pltpu.roll(h, 1, 0))
    h_p1 = jnp.where(row == TM - 1, hn, pltpu.roll(h, TM - 1, 0))
    xx = 0.5 * (h_m1 + h_p1) - h

    def lerp(n):
        return _bf(h + xx * vec_ref[_V_MU + n:_V_MU + n + 1])

    r = _dot(lerp(0), wr_ref[...])
    k = _dot(lerp(2), wk_ref[...])
    xv = lerp(3)
    v = _dot(xv, wv_ref[...])
    g = _silu(_dot(lerp(5), wg_ref[...]))
    if has_vres:
        lv = _dot(_bf(_dot(xv, v1_ref[...])), v2_ref[...])
        v = v + (vf_ref[0] - v) * _sigmoid(vec_ref[_V_V0:_V_V0 + 1] + lv)
    tw = _bf(jnp.tanh(_dot(lerp(1), w1_ref[...])))
    la = _bf(_dot(lerp(4), a1_ref[...]))

    e = e_ref[...]
    et = et_ref[...]
    kkr = k * vec_ref[_V_KK:_V_KK + 1]
    kk = kkr * _seg_expand(jnp.minimum(lax.rsqrt(_seg_reduce(kkr * kkr, e)), 1e12), et)
    half_ka = 0.5 * vec_ref[_V_KA:_V_KA + 1]

    ksum = None
    for d, (kd_ref, b_ref, lw_ref) in enumerate(((kd0_ref, b0_ref, lw0_ref), (kd1_ref, b1_ref, lw1_ref))):
        wl = vec_ref[_V_W0 + d:_V_W0 + d + 1] + _dot(tw, w2_ref[d])
        lw_ref[0] = (-0.5 * EXP_M05) * jnp.tanh(0.5 * wl) + (-0.5 * EXP_M05)
        ta = jnp.tanh(0.5 * (vec_ref[_V_A0 + d:_V_A0 + d + 1] + _dot(la, a2_ref[d])))
        a = 0.5 * ta + 0.5
        kd = k * ((1.0 - half_ka) + half_ka * ta)
        kd_ref[0] = kd
        b_ref[0] = kk * a
        ksum = kd if ksum is None else ksum + kd

    r_ref[0] = r
    v_ref[0] = v
    kk_ref[0] = kk
    g_ref[0] = _bf(g)
    bonus_ref[0] = _bf(_seg_expand(_seg_reduce(r * ksum * vec_ref[_V_RK:_V_RK + 1], e), et) * v)


def _stream_specs(stream):
    if isinstance(stream, tuple):
        ctx, x = stream
        B, T, _ = x.shape
        specs = [pl.BlockSpec((1, TM, D), lambda b, t: (b, 0, 0)),
                 pl.BlockSpec((1, TM, D), lambda b, t: (b, jnp.maximum(t - 1, 0), 0))]
        return specs, [ctx, x], B, T + TM, 1
    B, TT, _ = stream.shape
    return [pl.BlockSpec((1, TM, D), lambda b, t: (b, t, 0))], [stream], B, TT, 0


def _rwkv_proj(stream, mod, vec, wr, wk, wv, wg, w1c, w2z, a1c, a2z, e, et, vres):
    cur_specs, cur_args, B, TT, lat0 = _stream_specs(stream)
    split = lat0 == 1
    nt = TT // TM
    has_vres = vres is not None
    tile = pl.BlockSpec((1, TM, D), lambda b, t: (b, t, 0))
    rows8 = (TT - lat0 * TM) // 8
    in_specs = cur_specs + [
        pl.BlockSpec((1, 8, D), lambda b, t: (b, jnp.maximum((t - lat0) * (TM // 8) - 1, 0), 0)),
        pl.BlockSpec((1, 8, D), lambda b, t: (b, jnp.minimum((t + 1 - lat0) * (TM // 8), rows8 - 1), 0)),
        pl.BlockSpec((1, 1, 1, 3 * D), lambda b, t: (b, jnp.minimum(t, 1), 0, 0)),
        _const_spec((16, D)),
        _const_spec((D, D)), _const_spec((D, D)), _const_spec((D, D)), _const_spec((D, D)),
        _const_spec((D, PW)), _const_spec((2, PW, D)), _const_spec((D, PW)), _const_spec((2, PW, D)),
        _const_spec((D, PW)), _const_spec((PW, D)),
    ]
    args = cur_args + [cur_args[-1], cur_args[-1], mod, vec, wr, wk, wv, wg, w1c, w2z, a1c, a2z, e, et]
    if has_vres:
        v1p, v2p, v_first = vres
        in_specs += [_const_spec((D, PW)), _const_spec((PW, D)), tile]
        args += [v1p, v2p, v_first]
    out = jax.ShapeDtypeStruct((B, TT, D), F32)
    half = jax.ShapeDtypeStruct((B, TT, D), BF16)
    return pl.pallas_call(
        functools.partial(_rwkv_proj_kernel, has_vres, split, nt),
        grid=(B, nt),
        in_specs=in_specs,
        out_specs=[tile] * 11,
        out_shape=[out] * 3 + [half] * 2 + [out] * 6,
        compiler_params=_params(("parallel", "parallel")),
        name="rwkv_proj",
    )(*args)


def _cumsum_rows(x, rev):
    row = lax.broadcasted_iota(jnp.int32, (CH, 1), 0)
    s = 1
    while s < CH:
        if rev:
            x = x + jnp.where(row < CH - s, pltpu.roll(x, CH - s, 0), 0.0)
        else:
            x = x + jnp.where(row >= s, pltpu.roll(x, s, 0), 0.0)
        s *= 2
    return x


def _tri_masks(rev):
    ri = lax.broadcasted_iota(jnp.int32, (PW, PW), 0)
    ci = lax.broadcasted_iota(jnp.int32, (PW, PW), 1)
    same = (ri >> 6) == (ci >> 6)
    rl = ri & (CH - 1)
    cl = ci & (CH - 1)
    if rev:
        return jnp.logical_and(same, cl > rl), jnp.logical_and(same, cl >= rl)
    return jnp.logical_and(same, cl < rl), jnp.logical_and(same, cl <= rl)


def _scan_kernel(*refs):
    in_refs = (refs[0:6], refs[6:12])
    y_refs = refs[12:14]
    st_ref, q2_ref, g_ref, y0_ref, h_ref = refs[14:19]

    @pl.when(pl.program_id(2) == 0)
    def _():
        for ref in (st_ref, q2_ref, g_ref, y0_ref, h_ref):
            ref[...] = jnp.zeros_like(ref)

    nch = TM // CH
    chains = [(d, p, c) for d in (0, 1) for p in range(SCAN_PAIRS)
              for c in (range(nch - 1, -1, -1) if d else range(nch))]
    n = len(chains)
    nst = 2 * SCAN_PAIRS
    st = [st_ref[j] for j in range(nst)]

    def recurrence_round(k):
        for j in range(nst):
            i = j * nch + k
            d, p, c = chains[i]
            sb = _bf(st[j])
            y2 = _dot(q2_ref[i], sb) + y0_ref[i]
            st[j] = _dot(g_ref[i], sb) + h_ref[i]
            y_refs[d][0, pl.ds(c * CH, CH), p * PW:(p + 1) * PW] = _bf(y2[:CH] + y2[CH:])

    lane = lax.broadcasted_iota(jnp.int32, (1, PW), 1)
    m0 = (lane < HD).astype(F32)
    m1 = 1.0 - m0
    ri = lax.broadcasted_iota(jnp.int32, (PW, PW), 0)
    ci = lax.broadcasted_iota(jnp.int32, (PW, PW), 1)
    eye = (ri == ci).astype(F32)
    masks = (_tri_masks(False), _tri_masks(True))
    same = (ri >> 6) == (ci >> 6)

    def stack(x):
        return jnp.concatenate([x * m0, x * m1], axis=0)

    g = {}

    def prepare(g):
        g.update(a2=[], r2=[], v2=[], bk=[], wend=[], m=[])
        for d, p, c in chains:
            r_ref, kk_ref, v_ref, k_ref, b_ref, lw_ref = in_refs[d]
            sl = (0, pl.ds(c * CH, CH), slice(p * PW, (p + 1) * PW))
            lw = lw_ref[sl]
            cum = _cumsum_rows(lw, bool(d))
            last = cum[0:1] if d else cum[CH - 1:CH]
            wend = jnp.exp(last)
            e_n = jnp.exp(-cum)
            e_h = wend * e_n
            b = b_ref[sl]
            k = k_ref[sl]
            a2 = stack(-kk_ref[sl] * jnp.exp(cum - lw))
            r2 = stack(r_ref[sl] * jnp.exp(cum))
            g["a2"].append(a2)
            g["r2"].append(r2)
            g["v2"].append(stack(v_ref[sl]))
            bh = _bf(b * e_h)
            kh = _bf(k * e_h)
            g["bk"].append(jnp.concatenate([bh, bh, kh, kh], axis=0))
            g["wend"].append(wend)
            bt = _bf(b * e_n)
            kt = _bf(k * e_n)
            g["m"].append(_dot_nt(_bf(jnp.concatenate([a2, r2], axis=0)),
                                  jnp.concatenate([bt, bt, kt, kt], axis=0)))

    def mask(g):
        strict = [masks[d][0] for d, _, _ in chains]
        incl = [jnp.concatenate([masks[d][1]] * 2, axis=1) for d, _, _ in chains]
        m_ab = [jnp.where(s, m[:PW, :PW], 0.0) for s, m in zip(strict, g["m"])]
        g["m_ak"] = [_bf(jnp.where(s, m[:PW, PW:], 0.0)) for s, m in zip(strict, g["m"])]
        g["m_r"] = [_bf(jnp.where(s, m[PW:, :], 0.0)) for s, m in zip(incl, g["m"])]
        g["q"] = [eye + x for x in m_ab]
        lb = [_bf(x) for x in m_ab]
        g["l"] = [_dot(x, x) for x in lb]

    def double(g):
        lb = [_bf(x) for x in g["l"]]
        res = [_dot(x, jnp.concatenate([x, _bf(q)], axis=1)) for x, q in zip(lb, g["q"])]
        g["l"] = [x[:, :PW] for x in res]
        g["q"] = [q + x[:, PW:] for q, x in zip(g["q"], res)]

    def invert(g):
        g["tinv"] = [q + _dot(_bf(l), _bf(q)) for q, l in zip(g["q"], g["l"])]
        g["mv"] = [_dot(x, _bf(v)) for x, v in zip(g["m_ak"], g["v2"])]

    def solve(g):
        g["pu"] = [_dot(_bf(t), _bf(jnp.concatenate([a, mv], axis=1)))
                   for t, a, mv in zip(g["tinv"], g["a2"], g["mv"])]

    def combine(g):
        rhs = [_bf(jnp.concatenate([jnp.concatenate([pu[:, PW:], pu[:, :PW]], axis=1),
                                    jnp.concatenate([v, jnp.zeros_like(v)], axis=1)], axis=0))
               for pu, v in zip(g["pu"], g["v2"])]
        g["yp"] = [_dot(x, y) for x, y in zip(g["m_r"], rhs)]
        same2 = jnp.concatenate([same, same], axis=1)
        g["gh"] = [jnp.where(same2, _dot_tn(bk, y), 0.0) for bk, y in zip(g["bk"], rhs)]

    def publish(g):
        for i in range(n):
            g_ref[i] = _bf(g["gh"][i][:, PW:] + eye * g["wend"][i])
            q2_ref[i] = _bf(g["r2"][i] + g["yp"][i][:, PW:])
            y0_ref[i] = g["yp"][i][:, :PW]
            h_ref[i] = g["gh"][i][:, :PW]

    stages = [prepare, mask, double, double, double, double, invert, solve, combine, publish]
    for t, stage in enumerate(stages):
        if t < nch:
            recurrence_round(t)
        if t == nch:
            for j in range(nst):
                st_ref[j] = st[j]
        stage(g)


def _rwkv_scan(r, kk, v, kd0, b0, lw0, kd1, b1, lw1):
    B, TT, _ = r.shape
    nt = TT // TM
    width = SCAN_PAIRS * PW

    def bwd_tile(j):
        return jnp.where(j == 0, 0, nt - j)

    fwd_in = pl.BlockSpec((1, TM, width), lambda b, p, i: (b, jnp.minimum(i, nt - 1), p))
    bwd_in = pl.BlockSpec((1, TM, width), lambda b, p, i: (b, bwd_tile(jnp.minimum(i, nt - 1)), p))
    fwd_out = pl.BlockSpec((1, TM, width), lambda b, p, i: (b, jnp.maximum(i - 1, 0), p))
    bwd_out = pl.BlockSpec((1, TM, width), lambda b, p, i: (b, bwd_tile(jnp.maximum(i - 1, 0)), p))
    out = jax.ShapeDtypeStruct((B, TT, D), BF16)
    n = 2 * SCAN_PAIRS * (TM // CH)
    return pl.pallas_call(
        _scan_kernel,
        grid=(B, NPAIR // SCAN_PAIRS, nt + 1),
        in_specs=[fwd_in] * 6 + [bwd_in] * 6,
        out_specs=[fwd_out, bwd_out],
        out_shape=[out, out],
        scratch_shapes=[pltpu.VMEM((2 * SCAN_PAIRS, PW, PW), F32),
                        pltpu.VMEM((n, PW, PW), BF16), pltpu.VMEM((n, PW, PW), BF16),
                        pltpu.VMEM((n, PW, PW), F32), pltpu.VMEM((n, PW, PW), F32)],
        compiler_params=_params(("parallel", "parallel", "arbitrary")),
        name="rwkv_scan",
    )(r, kk, v, kd0, b0, lw0, r, kk, v, kd1, b1, lw1)


def _out_tail(o, w_ref, post_g, gate, x):
    out = _dot(_bf(o), w_ref[...])
    ms = jnp.mean(out * out, axis=-1, keepdims=True)
    return x + gate * (out * lax.rsqrt(ms + RMS_EPS) * post_g)


def _rwkv_out_kernel(split, y0_ref, y1_ref, bonus_ref, g_ref, *refs):
    if split:
        ctx_ref, refs = refs[0], refs[1:]
    x_ref, mod_ref, vec_ref, w_ref, e_ref, et_ref, o_ref = refs
    e = e_ref[...]
    et = et_ref[...]
    y = y0_ref[0].astype(F32) + y1_ref[0].astype(F32)
    yc = y - _seg_expand(_seg_reduce(y, e, two_pass=True) * (1.0 / HD), et)
    var = _seg_reduce(yc * yc, e) * (1.0 / HD)
    yn = yc * _seg_expand(lax.rsqrt(var + LNX_EPS), et) * vec_ref[0:1] + vec_ref[1:2]
    o = (yn + bonus_ref[0].astype(F32)) * g_ref[0].astype(F32)
    x = jnp.where(pl.program_id(1) == 0, ctx_ref[0], x_ref[0]) if split else x_ref[0]
    o_ref[0] = _out_tail(o, w_ref, vec_ref[2:3], mod_ref[0, 0, :, 2 * D:3 * D], x)


def _rwkv_out(y0, y1, bonus, g, stream, mod, vec, w_out, e, et):
    cur_specs, cur_args, B, TT, lat0 = _stream_specs(stream)
    nt = TT // TM
    tile = pl.BlockSpec((1, TM, D), lambda b, t: (b, t, 0))
    return pl.pallas_call(
        functools.partial(_rwkv_out_kernel, lat0 == 1),
        grid=(B, nt),
        in_specs=[tile] * 4 + cur_specs + [
            pl.BlockSpec((1, 1, 1, 3 * D), lambda b, t: (b, jnp.minimum(t, 1), 0, 0)),
            _const_spec((8, D)), _const_spec((D, D)), _const_spec((D, PW)), _const_spec((PW, D))],
        out_specs=tile,
        out_shape=jax.ShapeDtypeStruct((B, TT, D), F32),
        compiler_params=_params(("parallel", "parallel")),
        name="rwkv_out",
    )(y0, y1, bonus, g, *cur_args, mod, vec, w_out, e, et)


def _na_proj_kernel(x_ref, mod_ref, vec_ref, w_ref, bias_ref, q_ref, k_ref, v_ref, g_ref):
    shift = mod_ref[0, 0, :, 0:D]
    scale = mod_ref[0, 0, :, D:2 * D]
    h = _bf(_prenorm(x_ref[0], vec_ref[0:1], scale, shift))
    q = _dot(h, w_ref[:, 0:D]) + bias_ref[:, 0:D]
    q_ref[0] = _bf(q * (HD ** -0.5))
    k_ref[0] = _bf(_dot(h, w_ref[:, D:2 * D]) + bias_ref[:, D:2 * D])
    v_ref[0] = _bf(_dot(h, w_ref[:, 2 * D:3 * D]) + bias_ref[:, 2 * D:3 * D])
    g = _dot(h, w_ref[:, 3 * D:4 * D]) + bias_ref[:, 3 * D:4 * D]
    g_ref[0] = _bf(_silu(g))


def _na_proj(xc, mod, vec, w_in, b_in):
    B, TT, _ = xc.shape
    nt = TT // TM
    tile = pl.BlockSpec((1, TM, D), lambda b, t: (b, t, 0))
    bf = jax.ShapeDtypeStruct((B, TT, D), BF16)
    return pl.pallas_call(
        _na_proj_kernel,
        grid=(B, nt),
        in_specs=[tile,
                  pl.BlockSpec((1, 1, 1, 3 * D), lambda b, t: (b, jnp.minimum(t, 1), 0, 0)),
                  _const_spec((8, D)), _const_spec((D, 4 * D)), _const_spec((1, 4 * D))],
        out_specs=[pl.BlockSpec((1, TM, D), lambda b, t: (b, jnp.where(t == 0, nt - 1, t - 1), 0))] + [tile] * 3,
        out_shape=[bf] * 4,
        compiler_params=_params(("parallel", "parallel")),
        name="na_proj",
    )(xc, mod, vec, w_in, b_in)


def _softmax_pv(s_list, v_list):
    mx = None
    for s in s_list:
        m = jnp.max(s, axis=-1, keepdims=True)
        mx = m if mx is None else jnp.maximum(mx, m)
    den = None
    acc = None
    for s, v in zip(s_list, v_list):
        p = jnp.exp(s - mx)
        d = jnp.sum(p, axis=-1, keepdims=True)
        o = _dot(_bf(p), v)
        den = d if den is None else den + d
        acc = o if acc is None else acc + o
    return acc / den


def _na_attn_kernel(rows, q_ref, k_ref, v_ref, bias_ref, o_ref):
    lane = lax.broadcasted_iota(jnp.int32, (1, PW), 1)
    m0 = lane < HD
    step = pl.program_id(2)
    nkeys = WIN_H * GRID_W
    zero = jnp.zeros((), BF16)
    off, start = [], []
    for rr in range(RQ):
        r = step * RQ + rr
        rs = jnp.clip(r - WIN_H // 2, 0, rows - WIN_H)
        off.append(r - rs)
        start.append(pl.multiple_of(TM + rs * GRID_W, GRID_W))
    groups = [dict(pp=pp, lanes=slice(pp * PW, (pp + 1) * PW), rows=list(range(r0, r0 + ATT_GROUP)))
              for pp in range(ATT_PAIRS) for r0 in range(0, RQ, ATT_GROUP)]

    def scores(g):
        q2 = []
        for rr in g["rows"]:
            q = q_ref[0, rr * GRID_W:(rr + 1) * GRID_W, g["lanes"]]
            q2.append(jnp.concatenate([jnp.where(m0, q, zero), jnp.where(m0, zero, q)], axis=0))
        g["s_ctx"] = _dot_nt(jnp.concatenate(q2, axis=0), k_ref[0, 0:TM, g["lanes"]])
        g["s_win"] = [_dot_nt(q2[j], k_ref[0, pl.ds(start[rr], nkeys), g["lanes"]]) + bias_ref[g["pp"], off[rr]]
                      for j, rr in enumerate(g["rows"])]

    def softmax(g):
        n = len(g["rows"])
        mx_ctx = jnp.max(g["s_ctx"], axis=-1, keepdims=True)
        mx = [jnp.maximum(jnp.max(g["s_win"][j], axis=-1, keepdims=True), mx_ctx[j * PW:(j + 1) * PW])
              for j in range(n)]
        g["p_win"] = [jnp.exp(g["s_win"][j] - mx[j]) for j in range(n)]
        g["p_ctx"] = jnp.exp(g["s_ctx"] - jnp.concatenate(mx, axis=0))

    def weighted_sum(g):
        den_ctx = jnp.sum(g["p_ctx"], axis=-1, keepdims=True)
        o_ctx = _dot(_bf(g["p_ctx"]), v_ref[0, 0:TM, g["lanes"]])
        for j, rr in enumerate(g["rows"]):
            den = jnp.sum(g["p_win"][j], axis=-1, keepdims=True) + den_ctx[j * PW:(j + 1) * PW]
            vw = v_ref[0, pl.ds(start[rr], nkeys), g["lanes"]]
            o2 = (_dot(_bf(g["p_win"][j]), vw) + o_ctx[j * PW:(j + 1) * PW]) * (1.0 / den)
            o_ref[0, rr * GRID_W:(rr + 1) * GRID_W, g["lanes"]] = _bf(jnp.where(m0, o2[:GRID_W], o2[GRID_W:]))

    stages = (scores, softmax, weighted_sum)
    for t in range(len(groups) + len(stages) - 1):
        for k, stage in enumerate(stages):
            if 0 <= t - k < len(groups):
                stage(groups[t - k])


def _na_attn(q, k, v, bias):
    B, TT, _ = q.shape
    T = TT - TM
    rows = T // GRID_W
    qb = RQ * GRID_W
    width = ATT_PAIRS * PW
    kv = pl.BlockSpec((1, TT, width), lambda b, p, s: (b, 0, p))
    return pl.pallas_call(
        functools.partial(_na_attn_kernel, rows),
        grid=(B, NPAIR // ATT_PAIRS, rows // RQ),
        in_specs=[pl.BlockSpec((1, qb, width), lambda b, p, s: (b, s, p)), kv, kv,
                  pl.BlockSpec((ATT_PAIRS, WIN_H, PW, WIN_H * GRID_W), lambda b, p, s: (p, 0, 0, 0))],
        out_specs=pl.BlockSpec((1, qb, width), lambda b, p, s: (b, s, p)),
        out_shape=jax.ShapeDtypeStruct((B, T, D), BF16),
        compiler_params=_params(("parallel", "parallel", "arbitrary")),
        name="na_attn",
    )(q, k, v, bias)


def _ctx_attn_kernel(q_ref, k_ref, v_ref, o_ref):
    lane = lax.broadcasted_iota(jnp.int32, (1, PW), 1)
    m0 = lane < HD
    zero = jnp.zeros((), BF16)
    q = q_ref[0]
    q2 = jnp.concatenate([jnp.where(m0, q, zero), jnp.where(m0, zero, q)], axis=0)
    kc = k_ref[0]
    o2 = _softmax_pv([_dot_nt(q2, kc)], [v_ref[0]])
    o_ref[0] = _bf(jnp.where(m0, o2[:TM], o2[TM:]))


def _ctx_attn(q, k, v):
    B, TT, _ = q.shape
    blk = pl.BlockSpec((1, TM, PW), lambda b, p: (b, 0, p))
    q_ctx = pl.BlockSpec((1, TM, PW), lambda b, p: (b, TT // TM - 1, p))
    return pl.pallas_call(
        _ctx_attn_kernel,
        grid=(B, NPAIR),
        in_specs=[q_ctx, blk, blk],
        out_specs=blk,
        out_shape=jax.ShapeDtypeStruct((B, TM, D), BF16),
        compiler_params=_params(("parallel", "parallel")),
        name="ctx_attn",
    )(q, k, v)


def _na_out_kernel(with_ctx, *refs):
    if with_ctx:
        ol_ref, oc_ref, g_ref, x_ref, mod_ref, vec_ref, w_ref, o_ref = refs
        o = jnp.where(pl.program_id(1) == 0, oc_ref[0], ol_ref[0])
    else:
        ol_ref, g_ref, x_ref, mod_ref, vec_ref, w_ref, o_ref = refs
        o = ol_ref[0]
    o = o.astype(F32) * g_ref[0].astype(F32)
    o_ref[0] = _out_tail(o, w_ref, vec_ref[1:2], mod_ref[0, 0, :, 2 * D:3 * D], x_ref[0])


def _na_out(o_lat, o_ctx, g, xc, mod, vec, w_out):
    B, TT, _ = xc.shape
    nt = TT // TM
    with_ctx = o_ctx is not None
    consts = [_const_spec((8, D)), _const_spec((D, D))]
    if with_ctx:
        tile = pl.BlockSpec((1, TM, D), lambda b, t: (b, t, 0))
        in_specs = [pl.BlockSpec((1, TM, D), lambda b, t: (b, jnp.maximum(t - 1, 0), 0)),
                    pl.BlockSpec((1, TM, D), lambda b, t: (b, 0, 0)),
                    tile, tile,
                    pl.BlockSpec((1, 1, 1, 3 * D), lambda b, t: (b, jnp.minimum(t, 1), 0, 0))] + consts
        args = (o_lat, o_ctx, g, xc, mod, vec, w_out)
        grid, out_spec, out_rows = (B, nt), tile, TT
    else:
        lat = pl.BlockSpec((1, TM, D), lambda b, t: (b, t + 1, 0))
        out_spec = pl.BlockSpec((1, TM, D), lambda b, t: (b, t, 0))
        in_specs = [out_spec, lat, lat,
                    pl.BlockSpec((1, 1, 1, 3 * D), lambda b, t: (b, 1, 0, 0))] + consts
        args = (o_lat, g, xc, mod, vec, w_out)
        grid, out_rows = (B, nt - 1), TT - TM
    return pl.pallas_call(
        functools.partial(_na_out_kernel, with_ctx),
        grid=grid,
        in_specs=in_specs,
        out_specs=out_spec,
        out_shape=jax.ShapeDtypeStruct((B, out_rows, D), F32),
        compiler_params=_params(("parallel", "parallel")),
        name="na_out",
    )(*args)


def _na_bias_table(rpb):
    ncol = 2 * WIN_W - 1
    j = np.arange(GRID_W)
    win_start = np.clip(j - WIN_W // 2, 0, GRID_W - WIN_W)
    kcol = np.arange(GRID_W)
    valid = (kcol[None, :] >= win_start[:, None]) & (kcol[None, :] < win_start[:, None] + WIN_W)
    dc = np.clip(kcol[None, :] - j[:, None], -(WIN_W - 1), WIN_W - 1) + (WIN_W - 1)
    onehot = jnp.asarray(dc[None] == np.arange(ncol)[:, None, None], F32)
    rp = rpb.astype(F32).reshape(rpb.shape[0], NPAIR, 2, 2 * WIN_H - 1, ncol)
    base = jnp.einsum('lphrc,cqk->lphqrk', rp, onehot, precision=lax.Precision.HIGHEST)
    base = jnp.where(jnp.asarray(valid)[None, None, None, :, None, :], base, NEG_INF)
    halves = []
    for hl in range(2):
        b = base[:, :, hl]
        per_off = [b[:, :, :, WIN_H - 1 - off:2 * WIN_H - 1 - off, :].reshape(b.shape[0], NPAIR, GRID_W, WIN_H * GRID_W)
                   for off in range(WIN_H)]
        halves.append(jnp.stack(per_off, axis=2))
    return jnp.concatenate(halves, axis=3)


def _pad_rows(m, rows):
    return jnp.pad(m, ((0, rows - m.shape[0]), (0, 0)))


def kernel(x, c, ctx, c_ctx, ada_w, ada_b, pre_g, post_g, rw_mu, rw_w_rkvg, rw_w0, rw_w1, rw_w2, rw_a0, rw_a1, rw_a2, rw_v0, rw_v1, rw_v2, rw_k_k, rw_k_a, rw_r_k, rw_lnx_w, rw_lnx_b, rw_w_out, na_w_in, na_b_in, na_rpb, na_w_out):
    B, T, _ = x.shape
    depth = ada_w.shape[0]
    assert ctx.shape[1] == TM and T % (RQ * GRID_W) == 0 and T % TM == 0 and T // GRID_W >= WIN_H

    seg = (np.arange(D)[:, None] // HD) == np.arange(PW)[None, :]
    e = jnp.asarray(seg, BF16)
    et = jnp.asarray(seg.T, BF16)

    cond = jnp.concatenate([c, c_ctx[None, :]], axis=0)
    nrow = -(-(B + 1) // 8) * 8
    cond = _pad_rows(cond * jax.nn.sigmoid(cond), nrow)
    mod_all = _adaln(cond, ada_w, ada_b)
    mod_ctx = jnp.broadcast_to(mod_all[:, B:B + 1], (depth, B, 3 * D))
    mod_all = jnp.stack([mod_ctx, mod_all[:, :B]], axis=2)[:, :, :, None, :]

    bias_all = _na_bias_table(na_rpb)
    xc = (ctx, x)
    v_first = None
    for i in range(depth):
        last = i == depth - 1
        j = i // 2
        mod = mod_all[i]
        if i % 2 == 0:
            zero = jnp.zeros((D,), F32)
            vec = jnp.stack([pre_g[i], *rw_mu[j], rw_k_k[j], rw_k_a[j], rw_r_k[j].reshape(D),
                             rw_w0[j, 0], rw_w0[j, 1], rw_a0[j, 0], rw_a0[j, 1],
                             rw_v0[j - 1] if j > 0 else zero, zero])
            lora = D // 16
            w1c = _bf(jnp.concatenate([rw_w1[j, 0], rw_w1[j, 1]], axis=1))
            a1c = _bf(jnp.concatenate([rw_a1[j, 0], rw_a1[j, 1]], axis=1))
            zl = jnp.zeros((lora, D), F32)
            w2z = _bf(jnp.stack([jnp.concatenate([rw_w2[j, 0], zl]), jnp.concatenate([zl, rw_w2[j, 1]])]))
            a2z = _bf(jnp.stack([jnp.concatenate([rw_a2[j, 0], zl]), jnp.concatenate([zl, rw_a2[j, 1]])]))
            vres = None
            if j > 0:
                v1p = _bf(jnp.pad(rw_v1[j - 1], ((0, 0), (0, PW - rw_v1.shape[-1]))))
                v2p = _bf(_pad_rows(rw_v2[j - 1], PW))
                vres = (v1p, v2p, v_first)
            wq = _bf(rw_w_rkvg[j])
            r, v, kk, g, bonus, kd0, kd1, b0, b1, lw0, lw1 = _rwkv_proj(
                xc, mod, vec, wq[0], wq[1], wq[2], wq[3], w1c, w2z, a1c, a2z, e, et, vres)
            if j == 0:
                v_first = v
            y0, y1 = _rwkv_scan(r, kk, v, kd0, b0, lw0, kd1, b1, lw1)
            vec_o = _pad_rows(jnp.stack([rw_lnx_w[j], rw_lnx_b[j], post_g[i]]), 8)
            xc = _rwkv_out(y0, y1, bonus, g, xc, mod, vec_o, _bf(rw_w_out[j]), e, et)
        else:
            vec = _pad_rows(jnp.stack([pre_g[i], post_g[i]]), 8)
            q, k, v, g = _na_proj(xc, mod, vec, _bf(na_w_in[j]), na_b_in[j][None, :])
            o_lat = _na_attn(q, k, v, bias_all[j])
            o_ctx = None if last else _ctx_attn(q, k, v)
            xc = _na_out(o_lat, o_ctx, g, xc, mod, vec, _bf(na_w_out[j]))
    return xc if xc.shape[1] == T else xc[:, TM:]
```

```python
import functools
import math

import numpy as np
import jax
import jax.numpy as jnp
from jax import lax
from jax.experimental import pallas as pl
from jax.experimental.pallas import tpu as pltpu

F32 = jnp.float32
BF16 = jnp.bfloat16

D = 1024
HD = 64
HD_SHIFT = HD.bit_length() - 1
NH = D // HD
PW = 2 * HD
NPAIR = D // PW
TM = 256
CH = 64
assert CH == HD
GRID_W = 64
WIN_H = 8
WIN_W = 16
RMS_EPS = 1e-6
LNX_EPS = 64e-5
NEG_INF = -1e30
EXP_M05 = math.exp(-0.5)
RQ = 8
SCAN_PAIRS = 4
ATT_PAIRS = 2
ATT_GROUP = 4
OUT_SPLIT = 4
VMEM_LIMIT = 56 * 1024 * 1024


def _bf(x):
    return x.astype(BF16)


def _dot(a, b):
    return jnp.dot(a, b, preferred_element_type=F32)


def _dot_nt(a, b):
    return lax.dot_general(a, b, (((1,), (1,)), ((), ())), preferred_element_type=F32)


def _dot_tn(a, b):
    return lax.dot_general(a, b, (((0,), (0,)), ((), ())), preferred_element_type=F32)


def _dot_split(x, e):
    hi = _bf(x)
    lo = _bf(x - hi.astype(F32))
    return _dot(hi, e) + _dot(lo, e)


def _seg_reduce(x, e, two_pass=False):
    return _dot_split(x, e) if two_pass else _dot(_bf(x), e)


def _seg_expand(c, et):
    return _dot_split(c, et)


def _sigmoid(x):
    return 0.5 * jnp.tanh(0.5 * x) + 0.5


def _silu(x):
    half = 0.5 * x
    return half * jnp.tanh(half) + half


def _prenorm(x, g, scale, shift):
    ms = jnp.mean(x * x, axis=-1, keepdims=True)
    return x * lax.rsqrt(ms + RMS_EPS) * (g * (1.0 + scale)) + shift


def _const_spec(shape):
    nd = len(shape)
    return pl.BlockSpec(shape, lambda *_: (0,) * nd, pipeline_mode=pl.Buffered(1))


def _params(sem):
    return pltpu.CompilerParams(dimension_semantics=sem, vmem_limit_bytes=VMEM_LIMIT)


def _adaln_kernel(s_ref, w_ref, b_ref, o_ref):
    s = s_ref[...]
    w = w_ref[0]
    hi = _bf(s)
    lo = _bf(s - hi.astype(F32))
    whi = _bf(w)
    wlo = _bf(w - whi.astype(F32))
    o_ref[0] = _dot(hi, whi) + _dot(lo, whi) + _dot(hi, wlo) + b_ref[0]


def _adaln(silu_rows, ada_w, ada_b):
    depth = ada_w.shape[0]
    nrow = silu_rows.shape[0]
    return pl.pallas_call(
        _adaln_kernel,
        grid=(depth, 3),
        in_specs=[pl.BlockSpec((nrow, D), lambda i, j: (0, 0)),
                  pl.BlockSpec((1, D, D), lambda i, j: (i, 0, j)),
                  pl.BlockSpec((1, 1, D), lambda i, j: (i, 0, j))],
        out_specs=pl.BlockSpec((1, nrow, D), lambda i, j: (i, 0, j)),
        out_shape=jax.ShapeDtypeStruct((depth, nrow, 3 * D), F32),
        compiler_params=_params(("parallel", "parallel")),
        name="adaln",
    )(silu_rows, ada_w, ada_b.reshape(depth, 1, 3 * D))


_V_PRE_G, _V_MU, _V_KK, _V_KA, _V_RK, _V_W0, _V_A0, _V_V0 = 0, 1, 7, 8, 9, 10, 12, 14


def _rwkv_proj_kernel(has_vres, split, nt, *refs):
    if split:
        ctx_ref, refs = refs[0], refs[1:]
    if has_vres:
        (x_ref, xp_ref, xn_ref, mod_ref, vec_ref, wr_ref, wk_ref, wv_ref, wg_ref, w1_ref, w2_ref,
         a1_ref, a2_ref, e_ref, et_ref, v1_ref, v2_ref, vf_ref,
         r_ref, v_ref, kk_ref, g_ref, bonus_ref, kd0_ref, kd1_ref, b0_ref, b1_ref, lw0_ref, lw1_ref) = refs
    else:
        (x_ref, xp_ref, xn_ref, mod_ref, vec_ref, wr_ref, wk_ref, wv_ref, wg_ref, w1_ref, w2_ref,
         a1_ref, a2_ref, e_ref, et_ref,
         r_ref, v_ref, kk_ref, g_ref, bonus_ref, kd0_ref, kd1_ref, b0_ref, b1_ref, lw0_ref, lw1_ref) = refs
    t = pl.program_id(1)
    shift = mod_ref[0, 0, :, 0:D]
    scale = mod_ref[0, 0, :, D:2 * D]
    g_pre = vec_ref[_V_PRE_G:_V_PRE_G + 1]

    x = jnp.where(t == 0, ctx_ref[0], x_ref[0]) if split else x_ref[0]
    h = _prenorm(x, g_pre, scale, shift)
    hp = _prenorm(xp_ref[0], g_pre, scale, shift)[7:8]
    hn = _prenorm(xn_ref[0], g_pre, scale, shift)[0:1]
    hp = jnp.where(t >= 2, hp, 0.0)
    hn = jnp.where(jnp.logical_and(t >= 1, t < nt - 1), hn, 0.0)
    row = lax.broadcasted_iota(jnp.int32, (TM, 1), 0)
    h_m1 = jnp.where(row == 0, hp, pltpu.roll(h, 1, 0))
    h_p1 = jnp.where(row == TM - 1, hn, pltpu.roll(h, TM - 1, 0))
    xx = 0.5 * (h_m1 + h_p1) - h

    def lerp(n):
        return _bf(h + xx * vec_ref[_V_MU + n:_V_MU + n + 1])

    r = _dot(lerp(0), wr_ref[...])
    k = _dot(lerp(2), wk_ref[...])
    xv = lerp(3)
    v = _dot(xv, wv_ref[...])
    g = _silu(_dot(lerp(5), wg_ref[...]))
    if has_vres:
        lv = _dot(_bf(_dot(xv, v1_ref[...])), v2_ref[...])
        v = v + (vf_ref[0] - v) * _sigmoid(vec_ref[_V_V0:_V_V0 + 1] + lv)
    tw = _bf(jnp.tanh(_dot(lerp(1), w1_ref[...])))
    la = _bf(_dot(lerp(4), a1_ref[...]))

    e = e_ref[...]
    et = et_ref[...]
    kkr = k * vec_ref[_V_KK:_V_KK + 1]
    kk = kkr * _seg_expand(jnp.minimum(lax.rsqrt(_seg_reduce(kkr * kkr, e)), 1e12), et)
    half_ka = 0.5 * vec_ref[_V_KA:_V_KA + 1]

    ksum = None
    for d, (kd_ref, b_ref, lw_ref) in enumerate(((kd0_ref, b0_ref, lw0_ref), (kd1_ref, b1_ref, lw1_ref))):
        wl = vec_ref[_V_W0 + d:_V_W0 + d + 1] + _dot(tw, w2_ref[d])
        lw_ref[0] = (-0.5 * EXP_M05) * jnp.tanh(0.5 * wl) + (-0.5 * EXP_M05)
        ta = jnp.tanh(0.5 * (vec_ref[_V_A0 + d:_V_A0 + d + 1] + _dot(la, a2_ref[d])))
        a = 0.5 * ta + 0.5
        kd = k * ((1.0 - half_ka) + half_ka * ta)
        kd_ref[0] = kd
        b_ref[0] = kk * a
        ksum = kd if ksum is None else ksum + kd

    r_ref[0] = r
    v_ref[0] = v
    kk_ref[0] = kk
    g_ref[0] = _bf(g)
    bonus_ref[0] = _bf(_seg_expand(_seg_reduce(r * ksum * vec_ref[_V_RK:_V_RK + 1], e), et) * v)


def _stream_specs(stream):
    if isinstance(stream, tuple):
        ctx, x = stream
        B, T, _ = x.shape
        specs = [pl.BlockSpec((1, TM, D), lambda b, t: (b, 0, 0)),
                 pl.BlockSpec((1, TM, D), lambda b, t: (b, jnp.maximum(t - 1, 0), 0))]
        return specs, [ctx, x], B, T + TM, 1
    B, TT, _ = stream.shape
    return [pl.BlockSpec((1, TM, D), lambda b, t: (b, t, 0))], [stream], B, TT, 0


def _rwkv_proj(stream, mod, vec, wr, wk, wv, wg, w1c, w2z, a1c, a2z, e, et, vres):
    cur_specs, cur_args, B, TT, lat0 = _stream_specs(stream)
    split = lat0 == 1
    nt = TT // TM
    has_vres = vres is not None
    tile = pl.BlockSpec((1, TM, D), lambda b, t: (b, t, 0))
    rows8 = (TT - lat0 * TM) // 8
    in_specs = cur_specs + [
        pl.BlockSpec((1, 8, D), lambda b, t: (b, jnp.maximum((t - lat0) * (TM // 8) - 1, 0), 0)),
        pl.BlockSpec((1, 8, D), lambda b, t: (b, jnp.minimum((t + 1 - lat0) * (TM // 8), rows8 - 1), 0)),
        pl.BlockSpec((1, 1, 1, 3 * D), lambda b, t: (b, jnp.minimum(t, 1), 0, 0)),
        _const_spec((16, D)),
        _const_spec((D, D)), _const_spec((D, D)), _const_spec((D, D)), _const_spec((D, D)),
        _const_spec((D, PW)), _const_spec((2, PW, D)), _const_spec((D, PW)), _const_spec((2, PW, D)),
        _const_spec((D, PW)), _const_spec((PW, D)),
    ]
    args = cur_args + [cur_args[-1], cur_args[-1], mod, vec, wr, wk, wv, wg, w1c, w2z, a1c, a2z, e, et]
    if has_vres:
        v1p, v2p, v_first = vres
        in_specs += [_const_spec((D, PW)), _const_spec((PW, D)), tile]
        args += [v1p, v2p, v_first]
    out = jax.ShapeDtypeStruct((B, TT, D), F32)
    half = jax.ShapeDtypeStruct((B, TT, D), BF16)
    return pl.pallas_call(
        functools.partial(_rwkv_proj_kernel, has_vres, split, nt),
        grid=(B, nt),
        in_specs=in_specs,
        out_specs=[tile] * 11,
        out_shape=[out] * 3 + [half] * 2 + [out] * 6,
        compiler_params=_params(("parallel", "parallel")),
        name="rwkv_proj",
    )(*args)


def _cumsum_rows(x, rev):
    row = lax.broadcasted_iota(jnp.int32, (CH, 1), 0)
    s = 1
    while s < CH:
        if rev:
            x = x + jnp.where(row < CH - s, pltpu.roll(x, CH - s, 0), 0.0)
        else:
            x = x + jnp.where(row >= s, pltpu.roll(x, s, 0), 0.0)
        s *= 2
    return x


def _tri_masks(rev):
    ri = lax.broadcasted_iota(jnp.int32, (PW, PW), 0)
    ci = lax.broadcasted_iota(jnp.int32, (PW, PW), 1)
    same = (ri >> HD_SHIFT) == (ci >> HD_SHIFT)
    rl = ri & (CH - 1)
    cl = ci & (CH - 1)
    if rev:
        return jnp.logical_and(same, cl > rl), jnp.logical_and(same, cl >= rl)
    return jnp.logical_and(same, cl < rl), jnp.logical_and(same, cl <= rl)


def _scan_kernel(*refs):
    in_refs = (refs[0:6], refs[6:12])
    y_refs = refs[12:14]
    st_ref, q2_ref, g_ref, y0_ref, h_ref = refs[14:19]

    @pl.when(pl.program_id(2) == 0)
    def _():
        for ref in (st_ref, q2_ref, g_ref, y0_ref, h_ref):
            ref[...] = jnp.zeros_like(ref)

    nch = TM // CH
    chains = [(d, p, c) for d in (0, 1) for p in range(SCAN_PAIRS)
              for c in (range(nch - 1, -1, -1) if d else range(nch))]
    n = len(chains)
    nst = 2 * SCAN_PAIRS
    st = [st_ref[j] for j in range(nst)]

    def recurrence_round(k):
        for j in range(nst):
            i = j * nch + k
            d, p, c = chains[i]
            sb = _bf(st[j])
            y2 = _dot(q2_ref[i], sb) + y0_ref[i]
            st[j] = _dot(g_ref[i], sb) + h_ref[i]
            y_refs[d][0, pl.ds(c * CH, CH), p * PW:(p + 1) * PW] = _bf(y2[:CH] + y2[CH:])

    lane = lax.broadcasted_iota(jnp.int32, (1, PW), 1)
    m0 = (lane < HD).astype(F32)
    m1 = 1.0 - m0
    ri = lax.broadcasted_iota(jnp.int32, (PW, PW), 0)
    ci = lax.broadcasted_iota(jnp.int32, (PW, PW), 1)
    eye = (ri == ci).astype(F32)
    masks = (_tri_masks(False), _tri_masks(True))
    same = (ri >> HD_SHIFT) == (ci >> HD_SHIFT)

    def stack(x):
        return jnp.concatenate([x * m0, x * m1], axis=0)

    g = {}

    def prepare(g):
        g.update(a2=[], r2=[], v2=[], bk=[], wend=[], m=[])
        for d, p, c in chains:
            r_ref, kk_ref, v_ref, k_ref, b_ref, lw_ref = in_refs[d]
            sl = (0, pl.ds(c * CH, CH), slice(p * PW, (p + 1) * PW))
            lw = lw_ref[sl]
            cum = _cumsum_rows(lw, bool(d))
            last = cum[0:1] if d else cum[CH - 1:CH]
            wend = jnp.exp(last)
            e_n = jnp.exp(-cum)
            e_h = wend * e_n
            b = b_ref[sl]
            k = k_ref[sl]
            a2 = stack(-kk_ref[sl] * jnp.exp(cum - lw))
            r2 = stack(r_ref[sl] * jnp.exp(cum))
            g["a2"].append(a2)
            g["r2"].append(r2)
            g["v2"].append(stack(v_ref[sl]))
            bh = _bf(b * e_h)
            kh = _bf(k * e_h)
            g["bk"].append(jnp.concatenate([bh, bh, kh, kh], axis=0))
            g["wend"].append(wend)
            bt = _bf(b * e_n)
            kt = _bf(k * e_n)
            g["m"].append(_dot_nt(_bf(jnp.concatenate([a2, r2], axis=0)),
                                  jnp.concatenate([bt, bt, kt, kt], axis=0)))

    def mask(g):
        strict = [masks[d][0] for d, _, _ in chains]
        incl = [jnp.concatenate([masks[d][1]] * 2, axis=1) for d, _, _ in chains]
        m_ab = [jnp.where(s, m[:PW, :PW], 0.0) for s, m in zip(strict, g["m"])]
        g["m_ak"] = [_bf(jnp.where(s, m[:PW, PW:], 0.0)) for s, m in zip(strict, g["m"])]
        g["m_r"] = [_bf(jnp.where(s, m[PW:, :], 0.0)) for s, m in zip(incl, g["m"])]
        g["q"] = [eye + x for x in m_ab]
        lb = [_bf(x) for x in m_ab]
        g["l"] = [_dot(x, x) for x in lb]

    def double(g):
        lb = [_bf(x) for x in g["l"]]
        res = [_dot(x, jnp.concatenate([x, _bf(q)], axis=1)) for x, q in zip(lb, g["q"])]
        g["l"] = [x[:, :PW] for x in res]
        g["q"] = [q + x[:, PW:] for q, x in zip(g["q"], res)]

    def invert(g):
        g["tinv"] = [q + _dot(_bf(l), _bf(q)) for q, l in zip(g["q"], g["l"])]
        g["mv"] = [_dot(x, _bf(v)) for x, v in zip(g["m_ak"], g["v2"])]

    def solve(g):
        g["pu"] = [_dot(_bf(t), _bf(jnp.concatenate([a, mv], axis=1)))
                   for t, a, mv in zip(g["tinv"], g["a2"], g["mv"])]

    def combine(g):
        rhs = [_bf(jnp.concatenate([jnp.concatenate([pu[:, PW:], pu[:, :PW]], axis=1),
                                    jnp.concatenate([v, jnp.zeros_like(v)], axis=1)], axis=0))
               for pu, v in zip(g["pu"], g["v2"])]
        g["yp"] = [_dot(x, y) for x, y in zip(g["m_r"], rhs)]
        same2 = jnp.concatenate([same, same], axis=1)
        g["gh"] = [jnp.where(same2, _dot_tn(bk, y), 0.0) for bk, y in zip(g["bk"], rhs)]

    def publish(g):
        for i in range(n):
            g_ref[i] = _bf(g["gh"][i][:, PW:] + eye * g["wend"][i])
            q2_ref[i] = _bf(g["r2"][i] + g["yp"][i][:, PW:])
            y0_ref[i] = g["yp"][i][:, :PW]
            h_ref[i] = g["gh"][i][:, :PW]

    stages = [prepare, mask, double, double, double, double, invert, solve, combine, publish]
    for t, stage in enumerate(stages):
        if t < nch:
            recurrence_round(t)
        if t == nch:
            for j in range(nst):
                st_ref[j] = st[j]
        stage(g)


def _rwkv_scan(r, kk, v, kd0, b0, lw0, kd1, b1, lw1):
    B, TT, _ = r.shape
    nt = TT // TM
    width = SCAN_PAIRS * PW

    def bwd_tile(j):
        return jnp.where(j == 0, 0, nt - j)

    fwd_in = pl.BlockSpec((1, TM, width), lambda b, p, i: (b, jnp.minimum(i, nt - 1), p))
    bwd_in = pl.BlockSpec((1, TM, width), lambda b, p, i: (b, bwd_tile(jnp.minimum(i, nt - 1)), p))
    fwd_out = pl.BlockSpec((1, TM, width), lambda b, p, i: (b, jnp.maximum(i - 1, 0), p))
    bwd_out = pl.BlockSpec((1, TM, width), lambda b, p, i: (b, bwd_tile(jnp.maximum(i - 1, 0)), p))
    out = jax.ShapeDtypeStruct((B, TT, D), BF16)
    n = 2 * SCAN_PAIRS * (TM // CH)
    return pl.pallas_call(
        _scan_kernel,
        grid=(B, NPAIR // SCAN_PAIRS, nt + 1),
        in_specs=[fwd_in] * 6 + [bwd_in] * 6,
        out_specs=[fwd_out, bwd_out],
        out_shape=[out, out],
        scratch_shapes=[pltpu.VMEM((2 * SCAN_PAIRS, PW, PW), F32),
                        pltpu.VMEM((n, PW, PW), BF16), pltpu.VMEM((n, PW, PW), BF16),
                        pltpu.VMEM((n, PW, PW), F32), pltpu.VMEM((n, PW, PW), F32)],
        compiler_params=_params(("parallel", "parallel", "arbitrary")),
        name="rwkv_scan",
    )(r, kk, v, kd0, b0, lw0, r, kk, v, kd1, b1, lw1)


def _out_tail(o, w_ref, post_g, gate, x):
    out = _dot(_bf(o), w_ref[...])
    ms = jnp.mean(out * out, axis=-1, keepdims=True)
    return x + gate * (out * lax.rsqrt(ms + RMS_EPS) * post_g)


def _rwkv_out_kernel(split, y0_ref, y1_ref, bonus_ref, g_ref, *refs):
    if split:
        ctx_ref, refs = refs[0], refs[1:]
    x_ref, mod_ref, vec_ref, w_ref, e_ref, et_ref, o_ref = refs
    e = e_ref[...]
    et = et_ref[...]
    blocks = [pl.ds(i * (TM // OUT_SPLIT), TM // OUT_SPLIT) for i in range(OUT_SPLIT)]
    y = [y0_ref[0, b, :].astype(F32) + y1_ref[0, b, :].astype(F32) for b in blocks]
    mu = [_seg_reduce(v, e, two_pass=True) * (1.0 / HD) for v in y]
    yc = [v - _seg_expand(m, et) for v, m in zip(y, mu)]
    var = [_seg_reduce(v * v, e) * (1.0 / HD) for v in yc]
    yn = [v * _seg_expand(lax.rsqrt(s + LNX_EPS), et) * vec_ref[0:1] + vec_ref[1:2] for v, s in zip(yc, var)]
    o = jnp.concatenate([(v + bonus_ref[0, b, :].astype(F32)) * g_ref[0, b, :].astype(F32)
                         for v, b in zip(yn, blocks)], axis=0)
    x = jnp.where(pl.program_id(1) == 0, ctx_ref[0], x_ref[0]) if split else x_ref[0]
    o_ref[0] = _out_tail(o, w_ref, vec_ref[2:3], mod_ref[0, 0, :, 2 * D:3 * D], x)


def _rwkv_out(y0, y1, bonus, g, stream, mod, vec, w_out, e, et):
    cur_specs, cur_args, B, TT, lat0 = _stream_specs(stream)
    nt = TT // TM
    tile = pl.BlockSpec((1, TM, D), lambda b, t: (b, t, 0))
    return pl.pallas_call(
        functools.partial(_rwkv_out_kernel, lat0 == 1),
        grid=(B, nt),
        in_specs=[tile] * 4 + cur_specs + [
            pl.BlockSpec((1, 1, 1, 3 * D), lambda b, t: (b, jnp.minimum(t, 1), 0, 0)),
            _const_spec((8, D)), _const_spec((D, D)), _const_spec((D, PW)), _const_spec((PW, D))],
        out_specs=tile,
        out_shape=jax.ShapeDtypeStruct((B, TT, D), F32),
        compiler_params=_params(("parallel", "parallel")),
        name="rwkv_out",
    )(y0, y1, bonus, g, *cur_args, mod, vec, w_out, e, et)


def _na_proj_kernel(x_ref, mod_ref, vec_ref, w_ref, bias_ref, q_ref, k_ref, v_ref, g_ref):
    shift = mod_ref[0, 0, :, 0:D]
    scale = mod_ref[0, 0, :, D:2 * D]
    h = _bf(_prenorm(x_ref[0], vec_ref[0:1], scale, shift))
    q = _dot(h, w_ref[:, 0:D]) + bias_ref[:, 0:D]
    q_ref[0] = _bf(q * (HD ** -0.5))
    k_ref[0] = _bf(_dot(h, w_ref[:, D:2 * D]) + bias_ref[:, D:2 * D])
    v_ref[0] = _bf(_dot(h, w_ref[:, 2 * D:3 * D]) + bias_ref[:, 2 * D:3 * D])
    g = _dot(h, w_ref[:, 3 * D:4 * D]) + bias_ref[:, 3 * D:4 * D]
    g_ref[0] = _bf(_silu(g))


def _na_proj(xc, mod, vec, w_in, b_in):
    B, TT, _ = xc.shape
    nt = TT // TM
    tile = pl.BlockSpec((1, TM, D), lambda b, t: (b, t, 0))
    bf = jax.ShapeDtypeStruct((B, TT, D), BF16)
    return pl.pallas_call(
        _na_proj_kernel,
        grid=(B, nt),
        in_specs=[tile,
                  pl.BlockSpec((1, 1, 1, 3 * D), lambda b, t: (b, jnp.minimum(t, 1), 0, 0)),
                  _const_spec((8, D)), _const_spec((D, 4 * D)), _const_spec((1, 4 * D))],
        out_specs=[pl.BlockSpec((1, TM, D), lambda b, t: (b, jnp.where(t == 0, nt - 1, t - 1), 0))] + [tile] * 3,
        out_shape=[bf] * 4,
        compiler_params=_params(("parallel", "parallel")),
        name="na_proj",
    )(xc, mod, vec, w_in, b_in)


def _softmax_pv(s_list, v_list):
    mx = None
    for s in s_list:
        m = jnp.max(s, axis=-1, keepdims=True)
        mx = m if mx is None else jnp.maximum(mx, m)
    den = None
    acc = None
    for s, v in zip(s_list, v_list):
        p = jnp.exp(s - mx)
        d = jnp.sum(p, axis=-1, keepdims=True)
        o = _dot(_bf(p), v)
        den = d if den is None else den + d
        acc = o if acc is None else acc + o
    return acc / den


def _na_attn_kernel(rows, q_ref, k_ref, v_ref, bias_ref, o_ref):
    lane = lax.broadcasted_iota(jnp.int32, (1, PW), 1)
    m0 = lane < HD
    step = pl.program_id(2)
    nkeys = WIN_H * GRID_W
    zero = jnp.zeros((), BF16)
    off, start = [], []
    for rr in range(RQ):
        r = step * RQ + rr
        rs = jnp.clip(r - WIN_H // 2, 0, rows - WIN_H)
        off.append(r - rs)
        start.append(pl.multiple_of(TM + rs * GRID_W, GRID_W))
    groups = [dict(pp=pp, lanes=slice(pp * PW, (pp + 1) * PW), rows=list(range(r0, r0 + ATT_GROUP)))
              for pp in range(ATT_PAIRS) for r0 in range(0, RQ, ATT_GROUP)]

    def scores(g):
        q2 = []
        for rr in g["rows"]:
            q = q_ref[0, rr * GRID_W:(rr + 1) * GRID_W, g["lanes"]]
            q2.append(jnp.concatenate([jnp.where(m0, q, zero), jnp.where(m0, zero, q)], axis=0))
        g["s_ctx"] = _dot_nt(jnp.concatenate(q2, axis=0), k_ref[0, 0:TM, g["lanes"]])
        g["s_win"] = [_dot_nt(q2[j], k_ref[0, pl.ds(start[rr], nkeys), g["lanes"]]) + bias_ref[g["pp"], off[rr]]
                      for j, rr in enumerate(g["rows"])]

    def softmax(g):
        n = len(g["rows"])
        mx_ctx = jnp.max(g["s_ctx"], axis=-1, keepdims=True)
        mx = [jnp.maximum(jnp.max(g["s_win"][j], axis=-1, keepdims=True), mx_ctx[j * PW:(j + 1) * PW])
              for j in range(n)]
        g["p_win"] = [jnp.exp(g["s_win"][j] - mx[j]) for j in range(n)]
        g["p_ctx"] = jnp.exp(g["s_ctx"] - jnp.concatenate(mx, axis=0))

    def weighted_sum(g):
        den_ctx = jnp.sum(g["p_ctx"], axis=-1, keepdims=True)
        o_ctx = _dot(_bf(g["p_ctx"]), v_ref[0, 0:TM, g["lanes"]])
        for j, rr in enumerate(g["rows"]):
            den = jnp.sum(g["p_win"][j], axis=-1, keepdims=True) + den_ctx[j * PW:(j + 1) * PW]
            vw = v_ref[0, pl.ds(start[rr], nkeys), g["lanes"]]
            o2 = (_dot(_bf(g["p_win"][j]), vw) + o_ctx[j * PW:(j + 1) * PW]) * (1.0 / den)
            o_ref[0, rr * GRID_W:(rr + 1) * GRID_W, g["lanes"]] = _bf(jnp.where(m0, o2[:GRID_W], o2[GRID_W:]))

    stages = (scores, softmax, weighted_sum)
    for t in range(len(groups) + len(stages) - 1):
        for k, stage in enumerate(stages):
            if 0 <= t - k < len(groups):
                stage(groups[t - k])


def _na_attn(q, k, v, bias):
    B, TT, _ = q.shape
    T = TT - TM
    rows = T // GRID_W
    qb = RQ * GRID_W
    width = ATT_PAIRS * PW
    kv = pl.BlockSpec((1, TT, width), lambda b, p, s: (b, 0, p))
    return pl.pallas_call(
        functools.partial(_na_attn_kernel, rows),
        grid=(B, NPAIR // ATT_PAIRS, rows // RQ),
        in_specs=[pl.BlockSpec((1, qb, width), lambda b, p, s: (b, s, p)), kv, kv,
                  pl.BlockSpec((ATT_PAIRS, WIN_H, PW, WIN_H * GRID_W), lambda b, p, s: (p, 0, 0, 0))],
        out_specs=pl.BlockSpec((1, qb, width), lambda b, p, s: (b, s, p)),
        out_shape=jax.ShapeDtypeStruct((B, T, D), BF16),
        compiler_params=_params(("parallel", "parallel", "arbitrary")),
        name="na_attn",
    )(q, k, v, bias)


def _ctx_attn_kernel(q_ref, k_ref, v_ref, o_ref):
    lane = lax.broadcasted_iota(jnp.int32, (1, PW), 1)
    m0 = lane < HD
    zero = jnp.zeros((), BF16)
    q = q_ref[0]
    q2 = jnp.concatenate([jnp.where(m0, q, zero), jnp.where(m0, zero, q)], axis=0)
    kc = k_ref[0]
    o2 = _softmax_pv([_dot_nt(q2, kc)], [v_ref[0]])
    o_ref[0] = _bf(jnp.where(m0, o2[:TM], o2[TM:]))


def _ctx_attn(q, k, v):
    B, TT, _ = q.shape
    blk = pl.BlockSpec((1, TM, PW), lambda b, p: (b, 0, p))
    q_ctx = pl.BlockSpec((1, TM, PW), lambda b, p: (b, TT // TM - 1, p))
    return pl.pallas_call(
        _ctx_attn_kernel,
        grid=(B, NPAIR),
        in_specs=[q_ctx, blk, blk],
        out_specs=blk,
        out_shape=jax.ShapeDtypeStruct((B, TM, D), BF16),
        compiler_params=_params(("parallel", "parallel")),
        name="ctx_attn",
    )(q, k, v)


def _na_out_kernel(with_ctx, *refs):
    if with_ctx:
        ol_ref, oc_ref, g_ref, x_ref, mod_ref, vec_ref, w_ref, o_ref = refs
        o = jnp.where(pl.program_id(1) == 0, oc_ref[0], ol_ref[0])
    else:
        ol_ref, g_ref, x_ref, mod_ref, vec_ref, w_ref, o_ref = refs
        o = ol_ref[0]
    o = o.astype(F32) * g_ref[0].astype(F32)
    o_ref[0] = _out_tail(o, w_ref, vec_ref[1:2], mod_ref[0, 0, :, 2 * D:3 * D], x_ref[0])


def _na_out(o_lat, o_ctx, g, xc, mod, vec, w_out):
    B, TT, _ = xc.shape
    nt = TT // TM
    with_ctx = o_ctx is not None
    consts = [_const_spec((8, D)), _const_spec((D, D))]
    if with_ctx:
        tile = pl.BlockSpec((1, TM, D), lambda b, t: (b, t, 0))
        in_specs = [pl.BlockSpec((1, TM, D), lambda b, t: (b, jnp.maximum(t - 1, 0), 0)),
                    pl.BlockSpec((1, TM, D), lambda b, t: (b, 0, 0)),
                    tile, tile,
                    pl.BlockSpec((1, 1, 1, 3 * D), lambda b, t: (b, jnp.minimum(t, 1), 0, 0))] + consts
        args = (o_lat, o_ctx, g, xc, mod, vec, w_out)
        grid, out_spec, out_rows = (B, nt), tile, TT
    else:
        lat = pl.BlockSpec((1, TM, D), lambda b, t: (b, t + 1, 0))
        out_spec = pl.BlockSpec((1, TM, D), lambda b, t: (b, t, 0))
        in_specs = [out_spec, lat, lat,
                    pl.BlockSpec((1, 1, 1, 3 * D), lambda b, t: (b, 1, 0, 0))] + consts
        args = (o_lat, g, xc, mod, vec, w_out)
        grid, out_rows = (B, nt - 1), TT - TM
    return pl.pallas_call(
        functools.partial(_na_out_kernel, with_ctx),
        grid=grid,
        in_specs=in_specs,
        out_specs=out_spec,
        out_shape=jax.ShapeDtypeStruct((B, out_rows, D), F32),
        compiler_params=_params(("parallel", "parallel")),
        name="na_out",
    )(*args)


def _na_bias_table(rpb):
    ncol = 2 * WIN_W - 1
    j = np.arange(GRID_W)
    win_start = np.clip(j - WIN_W // 2, 0, GRID_W - WIN_W)
    kcol = np.arange(GRID_W)
    valid = (kcol[None, :] >= win_start[:, None]) & (kcol[None, :] < win_start[:, None] + WIN_W)
    dc = np.clip(kcol[None, :] - j[:, None], -(WIN_W - 1), WIN_W - 1) + (WIN_W - 1)
    onehot = jnp.asarray(dc[None] == np.arange(ncol)[:, None, None], F32)
    rp = rpb.astype(F32).reshape(rpb.shape[0], NPAIR, 2, 2 * WIN_H - 1, ncol)
    base = jnp.einsum('lphrc,cqk->lphqrk', rp, onehot, precision=lax.Precision.HIGHEST)
    base = jnp.where(jnp.asarray(valid)[None, None, None, :, None, :], base, NEG_INF)
    halves = []
    for hl in range(2):
        b = base[:, :, hl]
        per_off = [b[:, :, :, WIN_H - 1 - off:2 * WIN_H - 1 - off, :].reshape(b.shape[0], NPAIR, GRID_W, WIN_H * GRID_W)
                   for off in range(WIN_H)]
        halves.append(jnp.stack(per_off, axis=2))
    return jnp.concatenate(halves, axis=3)


def _pad_rows(m, rows):
    return jnp.pad(m, ((0, rows - m.shape[0]), (0, 0)))


def kernel(x, c, ctx, c_ctx, ada_w, ada_b, pre_g, post_g, rw_mu, rw_w_rkvg, rw_w0, rw_w1, rw_w2, rw_a0, rw_a1, rw_a2, rw_v0, rw_v1, rw_v2, rw_k_k, rw_k_a, rw_r_k, rw_lnx_w, rw_lnx_b, rw_w_out, na_w_in, na_b_in, na_rpb, na_w_out):
    B, T, _ = x.shape
    depth = ada_w.shape[0]
    assert ctx.shape[1] == TM and T % (RQ * GRID_W) == 0 and T % TM == 0 and T // GRID_W >= WIN_H

    seg = (np.arange(D)[:, None] // HD) == np.arange(PW)[None, :]
    e = jnp.asarray(seg, BF16)
    et = jnp.asarray(seg.T, BF16)

    cond = jnp.concatenate([c, c_ctx[None, :]], axis=0)
    nrow = -(-(B + 1) // 8) * 8
    cond = _pad_rows(cond * jax.nn.sigmoid(cond), nrow)
    mod_all = _adaln(cond, ada_w, ada_b)
    mod_ctx = jnp.broadcast_to(mod_all[:, B:B + 1], (depth, B, 3 * D))
    mod_all = jnp.stack([mod_ctx, mod_all[:, :B]], axis=2)[:, :, :, None, :]

    bias_all = _na_bias_table(na_rpb)
    xc = (ctx, x)
    v_first = None
    for i in range(depth):
        last = i == depth - 1
        j = i // 2
        mod = mod_all[i]
        if i % 2 == 0:
            zero = jnp.zeros((D,), F32)
            vec = jnp.stack([pre_g[i], *rw_mu[j], rw_k_k[j], rw_k_a[j], rw_r_k[j].reshape(D),
                             rw_w0[j, 0], rw_w0[j, 1], rw_a0[j, 0], rw_a0[j, 1],
                             rw_v0[j - 1] if j > 0 else zero, zero])
            lora = D // 16
            w1c = _bf(jnp.concatenate([rw_w1[j, 0], rw_w1[j, 1]], axis=1))
            a1c = _bf(jnp.concatenate([rw_a1[j, 0], rw_a1[j, 1]], axis=1))
            zl = jnp.zeros((lora, D), F32)
            w2z = _bf(jnp.stack([jnp.concatenate([rw_w2[j, 0], zl]), jnp.concatenate([zl, rw_w2[j, 1]])]))
            a2z = _bf(jnp.stack([jnp.concatenate([rw_a2[j, 0], zl]), jnp.concatenate([zl, rw_a2[j, 1]])]))
            vres = None
            if j > 0:
                v1p = _bf(jnp.pad(rw_v1[j - 1], ((0, 0), (0, PW - rw_v1.shape[-1]))))
                v2p = _bf(_pad_rows(rw_v2[j - 1], PW))
                vres = (v1p, v2p, v_first)
            wq = _bf(rw_w_rkvg[j])
            r, v, kk, g, bonus, kd0, kd1, b0, b1, lw0, lw1 = _rwkv_proj(
                xc, mod, vec, wq[0], wq[1], wq[2], wq[3], w1c, w2z, a1c, a2z, e, et, vres)
            if j == 0:
                v_first = v
            y0, y1 = _rwkv_scan(r, kk, v, kd0, b0, lw0, kd1, b1, lw1)
            vec_o = _pad_rows(jnp.stack([rw_lnx_w[j], rw_lnx_b[j], post_g[i]]), 8)
            xc = _rwkv_out(y0, y1, bonus, g, xc, mod, vec_o, _bf(rw_w_out[j]), e, et)
        else:
            vec = _pad_rows(jnp.stack([pre_g[i], post_g[i]]), 8)
            q, k, v, g = _na_proj(xc, mod, vec, _bf(na_w_in[j]), na_b_in[j][None, :])
            o_lat = _na_attn(q, k, v, bias_all[j])
            o_ctx = None if last else _ctx_attn(q, k, v)
            xc = _na_out(o_lat, o_ctx, g, xc, mod, vec, _bf(na_w_out[j]))
    return xc if xc.shape[1] == T else xc[:, TM:]
```

```python
import functools
import math

import numpy as np
import jax
import jax.numpy as jnp
from jax import lax
from jax.experimental import pallas as pl
from jax.experimental.pallas import tpu as pltpu

F32 = jnp.float32
BF16 = jnp.bfloat16

D = 1024
HD = 64
HD_SHIFT = HD.bit_length() - 1
NH = D // HD
PW = 2 * HD
NPAIR = D // PW
TM = 256
CH = 64
assert CH == HD
GRID_W = 64
WIN_H = 8
WIN_W = 16
RMS_EPS = 1e-6
LNX_EPS = 64e-5
NEG_INF = -1e30
EXP_M05 = math.exp(-0.5)
RQ = 8
SCAN_PAIRS = 4
ATT_PAIRS = 2
ATT_GROUP = 4
OUT_SPLIT = 4
VMEM_LIMIT = 56 * 1024 * 1024


def _bf(x):
    return x.astype(BF16)


def _dot(a, b):
    return jnp.dot(a, b, preferred_element_type=F32)


def _dot_nt(a, b):
    return lax.dot_general(a, b, (((1,), (1,)), ((), ())), preferred_element_type=F32)


def _dot_tn(a, b):
    return lax.dot_general(a, b, (((0,), (0,)), ((), ())), preferred_element_type=F32)


def _dot_split(x, e):
    hi = _bf(x)
    lo = _bf(x - hi.astype(F32))
    return _dot(hi, e) + _dot(lo, e)


def _seg_reduce(x, e, two_pass=False):
    return _dot_split(x, e) if two_pass else _dot(_bf(x), e)


def _seg_expand(c, et):
    return _dot_split(c, et)


def _sigmoid(x):
    return 0.5 * jnp.tanh(0.5 * x) + 0.5


def _silu(x):
    half = 0.5 * x
    return half * jnp.tanh(half) + half


def _prenorm(x, g, scale, shift):
    ms = jnp.mean(x * x, axis=-1, keepdims=True)
    return x * lax.rsqrt(ms + RMS_EPS) * (g * (1.0 + scale)) + shift


def _const_spec(shape):
    nd = len(shape)
    return pl.BlockSpec(shape, lambda *_: (0,) * nd, pipeline_mode=pl.Buffered(1))


def _params(sem):
    return pltpu.CompilerParams(dimension_semantics=sem, vmem_limit_bytes=VMEM_LIMIT)


def _adaln_kernel(s_ref, w_ref, b_ref, o_ref):
    s = s_ref[...]
    w = w_ref[0]
    hi = _bf(s)
    lo = _bf(s - hi.astype(F32))
    whi = _bf(w)
    wlo = _bf(w - whi.astype(F32))
    o_ref[0] = _dot(hi, whi) + _dot(lo, whi) + _dot(hi, wlo) + b_ref[0]


def _adaln(silu_rows, ada_w, ada_b):
    depth = ada_w.shape[0]
    nrow = silu_rows.shape[0]
    return pl.pallas_call(
        _adaln_kernel,
        grid=(depth, 3),
        in_specs=[pl.BlockSpec((nrow, D), lambda i, j: (0, 0)),
                  pl.BlockSpec((1, D, D), lambda i, j: (i, 0, j)),
                  pl.BlockSpec((1, 1, D), lambda i, j: (i, 0, j))],
        out_specs=pl.BlockSpec((1, nrow, D), lambda i, j: (i, 0, j)),
        out_shape=jax.ShapeDtypeStruct((depth, nrow, 3 * D), F32),
        compiler_params=_params(("parallel", "parallel")),
        name="adaln",
    )(silu_rows, ada_w, ada_b.reshape(depth, 1, 3 * D))


_V_PRE_G, _V_MU, _V_KK, _V_KA, _V_RK, _V_W0, _V_A0, _V_V0 = 0, 1, 7, 8, 9, 10, 12, 14


def _rwkv_proj_kernel(has_vres, split, nt, *refs):
    if split:
        ctx_ref, refs = refs[0], refs[1:]
    if has_vres:
        (x_ref, xp_ref, xn_ref, mod_ref, vec_ref, wr_ref, wk_ref, wv_ref, wg_ref, w1_ref, w2_ref,
         a1_ref, a2_ref, e_ref, et_ref, v1_ref, v2_ref, vf_ref,
         r_ref, v_ref, kk_ref, g_ref, bonus_ref, kd0_ref, kd1_ref, b0_ref, b1_ref, lw0_ref, lw1_ref) = refs
    else:
        (x_ref, xp_ref, xn_ref, mod_ref, vec_ref, wr_ref, wk_ref, wv_ref, wg_ref, w1_ref, w2_ref,
         a1_ref, a2_ref, e_ref, et_ref,
         r_ref, v_ref, kk_ref, g_ref, bonus_ref, kd0_ref, kd1_ref, b0_ref, b1_ref, lw0_ref, lw1_ref) = refs
    t = pl.program_id(1)
    shift = mod_ref[0, 0, :, 0:D]
    scale = mod_ref[0, 0, :, D:2 * D]
    g_pre = vec_ref[_V_PRE_G:_V_PRE_G + 1]

    x = jnp.where(t == 0, ctx_ref[0], x_ref[0]) if split else x_ref[0]
    h = _prenorm(x, g_pre, scale, shift)
    hp = _prenorm(xp_ref[0], g_pre, scale, shift)[7:8]
    hn = _prenorm(xn_ref[0], g_pre, scale, shift)[0:1]
    hp = jnp.where(t >= 2, hp, 0.0)
    hn = jnp.where(jnp.logical_and(t >= 1, t < nt - 1), hn, 0.0)
    row = lax.broadcasted_iota(jnp.int32, (TM, 1), 0)
    h_m1 = jnp.where(row == 0, hp, pltpu.roll(h, 1, 0))
    h_p1 = jnp.where(row == TM - 1, hn, pltpu.roll(h, TM - 1, 0))
    xx = 0.5 * (h_m1 + h_p1) - h

    def lerp(n):
        return _bf(h + xx * vec_ref[_V_MU + n:_V_MU + n + 1])

    r = _dot(lerp(0), wr_ref[...])
    k = _dot(lerp(2), wk_ref[...])
    xv = lerp(3)
    v = _dot(xv, wv_ref[...])
    g = _silu(_dot(lerp(5), wg_ref[...]))
    if has_vres:
        lv = _dot(_bf(_dot(xv, v1_ref[...])), v2_ref[...])
        v = v + (vf_ref[0] - v) * _sigmoid(vec_ref[_V_V0:_V_V0 + 1] + lv)
    tw = _bf(jnp.tanh(_dot(lerp(1), w1_ref[...])))
    la = _bf(_dot(lerp(4), a1_ref[...]))

    e = e_ref[...]
    et = et_ref[...]
    kkr = k * vec_ref[_V_KK:_V_KK + 1]
    kk = kkr * _seg_expand(jnp.minimum(lax.rsqrt(_seg_reduce(kkr * kkr, e)), 1e12), et)
    half_ka = 0.5 * vec_ref[_V_KA:_V_KA + 1]

    ksum = None
    for d, (kd_ref, b_ref, lw_ref) in enumerate(((kd0_ref, b0_ref, lw0_ref), (kd1_ref, b1_ref, lw1_ref))):
        wl = vec_ref[_V_W0 + d:_V_W0 + d + 1] + _dot(tw, w2_ref[d])
        lw_ref[0] = (-0.5 * EXP_M05) * jnp.tanh(0.5 * wl) + (-0.5 * EXP_M05)
        ta = jnp.tanh(0.5 * (vec_ref[_V_A0 + d:_V_A0 + d + 1] + _dot(la, a2_ref[d])))
        a = 0.5 * ta + 0.5
        kd = k * ((1.0 - half_ka) + half_ka * ta)
        kd_ref[0] = kd
        b_ref[0] = kk * a
        ksum = kd if ksum is None else ksum + kd

    r_ref[0] = r
    v_ref[0] = v
    kk_ref[0] = kk
    g_ref[0] = _bf(g)
    bonus_ref[0] = _bf(_seg_expand(_seg_reduce(r * ksum * vec_ref[_V_RK:_V_RK + 1], e), et) * v)


def _stream_specs(stream):
    if isinstance(stream, tuple):
        ctx, x = stream
        B, T, _ = x.shape
        specs = [pl.BlockSpec((1, TM, D), lambda b, t: (b, 0, 0)),
                 pl.BlockSpec((1, TM, D), lambda b, t: (b, jnp.maximum(t - 1, 0), 0))]
        return specs, [ctx, x], B, T + TM, 1
    B, TT, _ = stream.shape
    return [pl.BlockSpec((1, TM, D), lambda b, t: (b, t, 0))], [stream], B, TT, 0


def _rwkv_proj(stream, mod, vec, wr, wk, wv, wg, w1c, w2z, a1c, a2z, e, et, vres):
    cur_specs, cur_args, B, TT, lat0 = _stream_specs(stream)
    split = lat0 == 1
    nt = TT // TM
    has_vres = vres is not None
    tile = pl.BlockSpec((1, TM, D), lambda b, t: (b, t, 0))
    rows8 = (TT - lat0 * TM) // 8
    in_specs = cur_specs + [
        pl.BlockSpec((1, 8, D), lambda b, t: (b, jnp.maximum((t - lat0) * (TM // 8) - 1, 0), 0)),
        pl.BlockSpec((1, 8, D), lambda b, t: (b, jnp.minimum((t + 1 - lat0) * (TM // 8), rows8 - 1), 0)),
        pl.BlockSpec((1, 1, 1, 3 * D), lambda b, t: (b, jnp.minimum(t, 1), 0, 0)),
        _const_spec((16, D)),
        _const_spec((D, D)), _const_spec((D, D)), _const_spec((D, D)), _const_spec((D, D)),
        _const_spec((D, PW)), _const_spec((2, PW, D)), _const_spec((D, PW)), _const_spec((2, PW, D)),
        _const_spec((D, PW)), _const_spec((PW, D)),
    ]
    args = cur_args + [cur_args[-1], cur_args[-1], mod, vec, wr, wk, wv, wg, w1c, w2z, a1c, a2z, e, et]
    if has_vres:
        v1p, v2p, v_first = vres
        in_specs += [_const_spec((D, PW)), _const_spec((PW, D)), tile]
        args += [v1p, v2p, v_first]
    out = jax.ShapeDtypeStruct((B, TT, D), F32)
    half = jax.ShapeDtypeStruct((B, TT, D), BF16)
    return pl.pallas_call(
        functools.partial(_rwkv_proj_kernel, has_vres, split, nt),
        grid=(B, nt),
        in_specs=in_specs,
        out_specs=[tile] * 11,
        out_shape=[out] * 3 + [half] * 2 + [out] * 6,
        compiler_params=_params(("parallel", "parallel")),
        name="rwkv_proj",
    )(*args)


def _cumsum_rows(x, rev):
    row = lax.broadcasted_iota(jnp.int32, (CH, 1), 0)
    s = 1
    while s < CH:
        if rev:
            x = x + jnp.where(row < CH - s, pltpu.roll(x, CH - s, 0), 0.0)
        else:
            x = x + jnp.where(row >= s, pltpu.roll(x, s, 0), 0.0)
        s *= 2
    return x


def _tri_masks(rev):
    ri = lax.broadcasted_iota(jnp.int32, (PW, PW), 0)
    ci = lax.broadcasted_iota(jnp.int32, (PW, PW), 1)
    same = (ri >> HD_SHIFT) == (ci >> HD_SHIFT)
    rl = ri & (CH - 1)
    cl = ci & (CH - 1)
    if rev:
        return jnp.logical_and(same, cl > rl), jnp.logical_and(same, cl >= rl)
    return jnp.logical_and(same, cl < rl), jnp.logical_and(same, cl <= rl)


def _scan_kernel(*refs):
    in_refs = (refs[0:6], refs[6:12])
    y_refs = refs[12:14]
    st_ref, q2_ref, g_ref, y0_ref, h_ref = refs[14:19]

    @pl.when(pl.program_id(2) == 0)
    def _():
        for ref in (st_ref, q2_ref, g_ref, y0_ref, h_ref):
            ref[...] = jnp.zeros_like(ref)

    nch = TM // CH
    chains = [(d, p, c) for d in (0, 1) for p in range(SCAN_PAIRS)
              for c in (range(nch - 1, -1, -1) if d else range(nch))]
    n = len(chains)
    nst = 2 * SCAN_PAIRS
    st = [st_ref[j] for j in range(nst)]

    def recurrence_round(k):
        for j in range(nst):
            i = j * nch + k
            d, p, c = chains[i]
            sb = _bf(st[j])
            y2 = _dot(q2_ref[i], sb) + y0_ref[i]
            st[j] = _dot(g_ref[i], sb) + h_ref[i]
            y_refs[d][0, pl.ds(c * CH, CH), p * PW:(p + 1) * PW] = _bf(y2[:CH] + y2[CH:])

    lane = lax.broadcasted_iota(jnp.int32, (1, PW), 1)
    m0 = (lane < HD).astype(F32)
    m1 = 1.0 - m0
    ri = lax.broadcasted_iota(jnp.int32, (PW, PW), 0)
    ci = lax.broadcasted_iota(jnp.int32, (PW, PW), 1)
    eye = (ri == ci).astype(F32)
    masks = (_tri_masks(False), _tri_masks(True))
    same = (ri >> HD_SHIFT) == (ci >> HD_SHIFT)

    def stack(x):
        return jnp.concatenate([x * m0, x * m1], axis=0)

    g = {}

    def prepare(g):
        g.update(a2=[], r2=[], v2=[], bk=[], wend=[], m=[])
        for d, p, c in chains:
            r_ref, kk_ref, v_ref, k_ref, b_ref, lw_ref = in_refs[d]
            sl = (0, pl.ds(c * CH, CH), slice(p * PW, (p + 1) * PW))
            lw = lw_ref[sl]
            cum = _cumsum_rows(lw, bool(d))
            last = cum[0:1] if d else cum[CH - 1:CH]
            wend = jnp.exp(last)
            e_n = jnp.exp(-cum)
            e_h = wend * e_n
            b = b_ref[sl]
            k = k_ref[sl]
            a2 = stack(-kk_ref[sl] * jnp.exp(cum - lw))
            r2 = stack(r_ref[sl] * jnp.exp(cum))
            g["a2"].append(a2)
            g["r2"].append(r2)
            g["v2"].append(stack(v_ref[sl]))
            bh = _bf(b * e_h)
            kh = _bf(k * e_h)
            g["bk"].append(jnp.concatenate([bh, bh, kh, kh], axis=0))
            g["wend"].append(wend)
            bt = _bf(b * e_n)
            kt = _bf(k * e_n)
            g["m"].append(_dot_nt(_bf(jnp.concatenate([a2, r2], axis=0)),
                                  jnp.concatenate([bt, bt, kt, kt], axis=0)))

    def mask(g):
        strict = [masks[d][0] for d, _, _ in chains]
        incl = [jnp.concatenate([masks[d][1]] * 2, axis=1) for d, _, _ in chains]
        m_ab = [jnp.where(s, m[:PW, :PW], 0.0) for s, m in zip(strict, g["m"])]
        g["m_ak"] = [_bf(jnp.where(s, m[:PW, PW:], 0.0)) for s, m in zip(strict, g["m"])]
        g["m_r"] = [_bf(jnp.where(s, m[PW:, :], 0.0)) for s, m in zip(incl, g["m"])]
        g["q"] = [eye + x for x in m_ab]
        lb = [_bf(x) for x in m_ab]
        g["l"] = [_dot(x, x) for x in lb]

    def double(g):
        lb = [_bf(x) for x in g["l"]]
        res = [_dot(x, jnp.concatenate([x, _bf(q)], axis=1)) for x, q in zip(lb, g["q"])]
        g["l"] = [x[:, :PW] for x in res]
        g["q"] = [q + x[:, PW:] for q, x in zip(g["q"], res)]

    def invert(g):
        g["tinv"] = [q + _dot(_bf(l), _bf(q)) for q, l in zip(g["q"], g["l"])]
        g["mv"] = [_dot(x, _bf(v)) for x, v in zip(g["m_ak"], g["v2"])]

    def solve(g):
        g["pu"] = [_dot(_bf(t), _bf(jnp.concatenate([a, mv], axis=1)))
                   for t, a, mv in zip(g["tinv"], g["a2"], g["mv"])]

    def combine(g):
        rhs = [_bf(jnp.concatenate([jnp.concatenate([pu[:, PW:], pu[:, :PW]], axis=1),
                                    jnp.concatenate([v, jnp.zeros_like(v)], axis=1)], axis=0))
               for pu, v in zip(g["pu"], g["v2"])]
        g["yp"] = [_dot(x, y) for x, y in zip(g["m_r"], rhs)]
        same2 = jnp.concatenate([same, same], axis=1)
        g["gh"] = [jnp.where(same2, _dot_tn(bk, y), 0.0) for bk, y in zip(g["bk"], rhs)]

    def publish(g):
        for i in range(n):
            g_ref[i] = _bf(g["gh"][i][:, PW:] + eye * g["wend"][i])
            q2_ref[i] = _bf(g["r2"][i] + g["yp"][i][:, PW:])
            y0_ref[i] = g["yp"][i][:, :PW]
            h_ref[i] = g["gh"][i][:, :PW]

    stages = [prepare, mask, double, double, double, double, invert, solve, combine, publish]
    for t, stage in enumerate(stages):
        if t < nch:
            recurrence_round(t)
        if t == nch:
            for j in range(nst):
                st_ref[j] = st[j]
        stage(g)


def _rwkv_scan(r, kk, v, kd0, b0, lw0, kd1, b1, lw1):
    B, TT, _ = r.shape
    nt = TT // TM
    width = SCAN_PAIRS * PW

    def bwd_tile(j):
        return jnp.where(j == 0, 0, nt - j)

    fwd_in = pl.BlockSpec((1, TM, width), lambda b, p, i: (b, jnp.minimum(i, nt - 1), p))
    bwd_in = pl.BlockSpec((1, TM, width), lambda b, p, i: (b, bwd_tile(jnp.minimum(i, nt - 1)), p))
    fwd_out = pl.BlockSpec((1, TM, width), lambda b, p, i: (b, jnp.maximum(i - 1, 0), p))
    bwd_out = pl.BlockSpec((1, TM, width), lambda b, p, i: (b, bwd_tile(jnp.maximum(i - 1, 0)), p))
    out = jax.ShapeDtypeStruct((B, TT, D), BF16)
    n = 2 * SCAN_PAIRS * (TM // CH)
    return pl.pallas_call(
        _scan_kernel,
        grid=(B, NPAIR // SCAN_PAIRS, nt + 1),
        in_specs=[fwd_in] * 6 + [bwd_in] * 6,
        out_specs=[fwd_out, bwd_out],
        out_shape=[out, out],
        scratch_shapes=[pltpu.VMEM((2 * SCAN_PAIRS, PW, PW), F32),
                        pltpu.VMEM((n, PW, PW), BF16), pltpu.VMEM((n, PW, PW), BF16),
                        pltpu.VMEM((n, PW, PW), F32), pltpu.VMEM((n, PW, PW), F32)],
        compiler_params=_params(("parallel", "parallel", "arbitrary")),
        name="rwkv_scan",
    )(r, kk, v, kd0, b0, lw0, r, kk, v, kd1, b1, lw1)


def _out_tail(o, w_ref, post_g, gate, x):
    out = _dot(_bf(o), w_ref[...])
    ms = jnp.mean(out * out, axis=-1, keepdims=True)
    return x + gate * (out * lax.rsqrt(ms + RMS_EPS) * post_g)


def _rwkv_out_kernel(split, fuse_na, y0_ref, y1_ref, bonus_ref, g_ref, *refs):
    if split:
        ctx_ref, refs = refs[0], refs[1:]
    if fuse_na:
        na_refs, refs = refs[6:10] + refs[11:], refs[:6] + refs[10:11]
    x_ref, mod_ref, vec_ref, w_ref, e_ref, et_ref, o_ref = refs
    e = e_ref[...]
    et = et_ref[...]
    blocks = [pl.ds(i * (TM // OUT_SPLIT), TM // OUT_SPLIT) for i in range(OUT_SPLIT)]
    y = [y0_ref[0, b, :].astype(F32) + y1_ref[0, b, :].astype(F32) for b in blocks]
    mu = [_seg_reduce(v, e, two_pass=True) * (1.0 / HD) for v in y]
    yc = [v - _seg_expand(m, et) for v, m in zip(y, mu)]
    var = [_seg_reduce(v * v, e) * (1.0 / HD) for v in yc]
    yn = [v * _seg_expand(lax.rsqrt(s + LNX_EPS), et) * vec_ref[0:1] + vec_ref[1:2] for v, s in zip(yc, var)]
    o = jnp.concatenate([(v + bonus_ref[0, b, :].astype(F32)) * g_ref[0, b, :].astype(F32)
                         for v, b in zip(yn, blocks)], axis=0)
    x = jnp.where(pl.program_id(1) == 0, ctx_ref[0], x_ref[0]) if split else x_ref[0]
    x_new = _out_tail(o, w_ref, vec_ref[2:3], mod_ref[0, 0, :, 2 * D:3 * D], x)
    o_ref[0] = x_new
    if fuse_na:
        _na_proj_tile(x_new, *na_refs)


def _rwkv_out(y0, y1, bonus, g, stream, mod, vec, w_out, e, et, na=None):
    cur_specs, cur_args, B, TT, lat0 = _stream_specs(stream)
    nt = TT // TM
    tile = pl.BlockSpec((1, TM, D), lambda b, t: (b, t, 0))
    mod_spec = pl.BlockSpec((1, 1, 1, 3 * D), lambda b, t: (b, jnp.minimum(t, 1), 0, 0))
    in_specs = [tile] * 4 + cur_specs + [
        mod_spec, _const_spec((8, D)), _const_spec((D, D)), _const_spec((D, PW)), _const_spec((PW, D))]
    args = [y0, y1, bonus, g, *cur_args, mod, vec, w_out, e, et]
    out_specs = [tile]
    out_shape = [jax.ShapeDtypeStruct((B, TT, D), F32)]
    if na is not None:
        in_specs += [mod_spec, _const_spec((8, D)), _const_spec((D, 4 * D)), _const_spec((1, 4 * D))]
        args += list(na)
        out_specs += _na_proj_out_specs(nt)
        out_shape += [jax.ShapeDtypeStruct((B, TT, D), BF16)] * 4
    out = pl.pallas_call(
        functools.partial(_rwkv_out_kernel, lat0 == 1, na is not None),
        grid=(B, nt),
        in_specs=in_specs,
        out_specs=out_specs,
        out_shape=out_shape,
        compiler_params=_params(("parallel", "parallel")),
        name="rwkv_out",
    )(*args)
    return out if na is not None else out[0]


def _na_proj_tile(x, mod_ref, vec_ref, w_ref, bias_ref, q_ref, k_ref, v_ref, g_ref):
    shift = mod_ref[0, 0, :, 0:D]
    scale = mod_ref[0, 0, :, D:2 * D]
    h = _bf(_prenorm(x, vec_ref[0:1], scale, shift))
    q = _dot(h, w_ref[:, 0:D]) + bias_ref[:, 0:D]
    q_ref[0] = _bf(q * (HD ** -0.5))
    k_ref[0] = _bf(_dot(h, w_ref[:, D:2 * D]) + bias_ref[:, D:2 * D])
    v_ref[0] = _bf(_dot(h, w_ref[:, 2 * D:3 * D]) + bias_ref[:, 2 * D:3 * D])
    g = _dot(h, w_ref[:, 3 * D:4 * D]) + bias_ref[:, 3 * D:4 * D]
    g_ref[0] = _bf(_silu(g))


def _na_proj_kernel(x_ref, *refs):
    _na_proj_tile(x_ref[0], *refs)


def _na_proj_out_specs(nt):
    tile = pl.BlockSpec((1, TM, D), lambda b, t: (b, t, 0))
    return [pl.BlockSpec((1, TM, D), lambda b, t: (b, jnp.where(t == 0, nt - 1, t - 1), 0))] + [tile] * 3


def _na_proj(xc, mod, vec, w_in, b_in):
    B, TT, _ = xc.shape
    nt = TT // TM
    tile = pl.BlockSpec((1, TM, D), lambda b, t: (b, t, 0))
    bf = jax.ShapeDtypeStruct((B, TT, D), BF16)
    return pl.pallas_call(
        _na_proj_kernel,
        grid=(B, nt),
        in_specs=[tile,
                  pl.BlockSpec((1, 1, 1, 3 * D), lambda b, t: (b, jnp.minimum(t, 1), 0, 0)),
                  _const_spec((8, D)), _const_spec((D, 4 * D)), _const_spec((1, 4 * D))],
        out_specs=_na_proj_out_specs(nt),
        out_shape=[bf] * 4,
        compiler_params=_params(("parallel", "parallel")),
        name="na_proj",
    )(xc, mod, vec, w_in, b_in)


def _softmax_pv(s_list, v_list):
    mx = None
    for s in s_list:
        m = jnp.max(s, axis=-1, keepdims=True)
        mx = m if mx is None else jnp.maximum(mx, m)
    den = None
    acc = None
    for s, v in zip(s_list, v_list):
        p = jnp.exp(s - mx)
        d = jnp.sum(p, axis=-1, keepdims=True)
        o = _dot(_bf(p), v)
        den = d if den is None else den + d
        acc = o if acc is None else acc + o
    return acc / den


def _na_attn_kernel(rows, q_ref, k_ref, v_ref, bias_ref, o_ref):
    lane = lax.broadcasted_iota(jnp.int32, (1, PW), 1)
    m0 = lane < HD
    step = pl.program_id(2)
    nkeys = WIN_H * GRID_W
    zero = jnp.zeros((), BF16)
    off, start = [], []
    for rr in range(RQ):
        r = step * RQ + rr
        rs = jnp.clip(r - WIN_H // 2, 0, rows - WIN_H)
        off.append(r - rs)
        start.append(pl.multiple_of(TM + rs * GRID_W, GRID_W))
    groups = [dict(pp=pp, lanes=slice(pp * PW, (pp + 1) * PW), rows=list(range(r0, r0 + ATT_GROUP)))
              for pp in range(ATT_PAIRS) for r0 in range(0, RQ, ATT_GROUP)]

    def scores(g):
        q2 = []
        for rr in g["rows"]:
            q = q_ref[0, rr * GRID_W:(rr + 1) * GRID_W, g["lanes"]]
            q2.append(jnp.concatenate([jnp.where(m0, q, zero), jnp.where(m0, zero, q)], axis=0))
        g["s_ctx"] = _dot_nt(jnp.concatenate(q2, axis=0), k_ref[0, 0:TM, g["lanes"]])
        g["s_win"] = [_dot_nt(q2[j], k_ref[0, pl.ds(start[rr], nkeys), g["lanes"]]) + bias_ref[g["pp"], off[rr]]
                      for j, rr in enumerate(g["rows"])]

    def softmax(g):
        n = len(g["rows"])
        mx_ctx = jnp.max(g["s_ctx"], axis=-1, keepdims=True)
        mx = [jnp.maximum(jnp.max(g["s_win"][j], axis=-1, keepdims=True), mx_ctx[j * PW:(j + 1) * PW])
              for j in range(n)]
        g["p_win"] = [jnp.exp(g["s_win"][j] - mx[j]) for j in range(n)]
        g["p_ctx"] = jnp.exp(g["s_ctx"] - jnp.concatenate(mx, axis=0))

    def weighted_sum(g):
        den_ctx = jnp.sum(g["p_ctx"], axis=-1, keepdims=True)
        o_ctx = _dot(_bf(g["p_ctx"]), v_ref[0, 0:TM, g["lanes"]])
        for j, rr in enumerate(g["rows"]):
            den = jnp.sum(g["p_win"][j], axis=-1, keepdims=True) + den_ctx[j * PW:(j + 1) * PW]
            vw = v_ref[0, pl.ds(start[rr], nkeys), g["lanes"]]
            o2 = (_dot(_bf(g["p_win"][j]), vw) + o_ctx[j * PW:(j + 1) * PW]) * (1.0 / den)
            o_ref[0, rr * GRID_W:(rr + 1) * GRID_W, g["lanes"]] = _bf(jnp.where(m0, o2[:GRID_W], o2[GRID_W:]))

    stages = (scores, softmax, weighted_sum)
    for t in range(len(groups) + len(stages) - 1):
        for k, stage in enumerate(stages):
            if 0 <= t - k < len(groups):
                stage(groups[t - k])


def _na_attn(q, k, v, bias):
    B, TT, _ = q.shape
    T = TT - TM
    rows = T // GRID_W
    qb = RQ * GRID_W
    width = ATT_PAIRS * PW
    kv = pl.BlockSpec((1, TT, width), lambda b, p, s: (b, 0, p))
    return pl.pallas_call(
        functools.partial(_na_attn_kernel, rows),
        grid=(B, NPAIR // ATT_PAIRS, rows // RQ),
        in_specs=[pl.BlockSpec((1, qb, width), lambda b, p, s: (b, s, p)), kv, kv,
                  pl.BlockSpec((ATT_PAIRS, WIN_H, PW, WIN_H * GRID_W), lambda b, p, s: (p, 0, 0, 0))],
        out_specs=pl.BlockSpec((1, qb, width), lambda b, p, s: (b, s, p)),
        out_shape=jax.ShapeDtypeStruct((B, T, D), BF16),
        compiler_params=_params(("parallel", "parallel", "arbitrary")),
        name="na_attn",
    )(q, k, v, bias)


def _ctx_attn_kernel(q_ref, k_ref, v_ref, o_ref):
    lane = lax.broadcasted_iota(jnp.int32, (1, PW), 1)
    m0 = lane < HD
    zero = jnp.zeros((), BF16)
    q = q_ref[0]
    q2 = jnp.concatenate([jnp.where(m0, q, zero), jnp.where(m0, zero, q)], axis=0)
    kc = k_ref[0]
    o2 = _softmax_pv([_dot_nt(q2, kc)], [v_ref[0]])
    o_ref[0] = _bf(jnp.where(m0, o2[:TM], o2[TM:]))


def _ctx_attn(q, k, v):
    B, TT, _ = q.shape
    blk = pl.BlockSpec((1, TM, PW), lambda b, p: (b, 0, p))
    q_ctx = pl.BlockSpec((1, TM, PW), lambda b, p: (b, TT // TM - 1, p))
    return pl.pallas_call(
        _ctx_attn_kernel,
        grid=(B, NPAIR),
        in_specs=[q_ctx, blk, blk],
        out_specs=blk,
        out_shape=jax.ShapeDtypeStruct((B, TM, D), BF16),
        compiler_params=_params(("parallel", "parallel")),
        name="ctx_attn",
    )(q, k, v)


def _na_out_kernel(with_ctx, *refs):
    if with_ctx:
        ol_ref, oc_ref, g_ref, x_ref, mod_ref, vec_ref, w_ref, o_ref = refs
        o = jnp.where(pl.program_id(1) == 0, oc_ref[0], ol_ref[0])
    else:
        ol_ref, g_ref, x_ref, mod_ref, vec_ref, w_ref, o_ref = refs
        o = ol_ref[0]
    o = o.astype(F32) * g_ref[0].astype(F32)
    o_ref[0] = _out_tail(o, w_ref, vec_ref[1:2], mod_ref[0, 0, :, 2 * D:3 * D], x_ref[0])


def _na_out(o_lat, o_ctx, g, xc, mod, vec, w_out):
    B, TT, _ = xc.shape
    nt = TT // TM
    with_ctx = o_ctx is not None
    consts = [_const_spec((8, D)), _const_spec((D, D))]
    if with_ctx:
        tile = pl.BlockSpec((1, TM, D), lambda b, t: (b, t, 0))
        in_specs = [pl.BlockSpec((1, TM, D), lambda b, t: (b, jnp.maximum(t - 1, 0), 0)),
                    pl.BlockSpec((1, TM, D), lambda b, t: (b, 0, 0)),
                    tile, tile,
                    pl.BlockSpec((1, 1, 1, 3 * D), lambda b, t: (b, jnp.minimum(t, 1), 0, 0))] + consts
        args = (o_lat, o_ctx, g, xc, mod, vec, w_out)
        grid, out_spec, out_rows = (B, nt), tile, TT
    else:
        lat = pl.BlockSpec((1, TM, D), lambda b, t: (b, t + 1, 0))
        out_spec = pl.BlockSpec((1, TM, D), lambda b, t: (b, t, 0))
        in_specs = [out_spec, lat, lat,
                    pl.BlockSpec((1, 1, 1, 3 * D), lambda b, t: (b, 1, 0, 0))] + consts
        args = (o_lat, g, xc, mod, vec, w_out)
        grid, out_rows = (B, nt - 1), TT - TM
    return pl.pallas_call(
        functools.partial(_na_out_kernel, with_ctx),
        grid=grid,
        in_specs=in_specs,
        out_specs=out_spec,
        out_shape=jax.ShapeDtypeStruct((B, out_rows, D), F32),
        compiler_params=_params(("parallel", "parallel")),
        name="na_out",
    )(*args)


def _na_bias_table(rpb):
    ncol = 2 * WIN_W - 1
    j = np.arange(GRID_W)
    win_start = np.clip(j - WIN_W // 2, 0, GRID_W - WIN_W)
    kcol = np.arange(GRID_W)
    valid = (kcol[None, :] >= win_start[:, None]) & (kcol[None, :] < win_start[:, None] + WIN_W)
    dc = np.clip(kcol[None, :] - j[:, None], -(WIN_W - 1), WIN_W - 1) + (WIN_W - 1)
    onehot = jnp.asarray(dc[None] == np.arange(ncol)[:, None, None], F32)
    rp = rpb.astype(F32).reshape(rpb.shape[0], NPAIR, 2, 2 * WIN_H - 1, ncol)
    base = jnp.einsum('lphrc,cqk->lphqrk', rp, onehot, precision=lax.Precision.HIGHEST)
    base = jnp.where(jnp.asarray(valid)[None, None, None, :, None, :], base, NEG_INF)
    halves = []
    for hl in range(2):
        b = base[:, :, hl]
        per_off = [b[:, :, :, WIN_H - 1 - off:2 * WIN_H - 1 - off, :].reshape(b.shape[0], NPAIR, GRID_W, WIN_H * GRID_W)
                   for off in range(WIN_H)]
        halves.append(jnp.stack(per_off, axis=2))
    return jnp.concatenate(halves, axis=3)


def _pad_rows(m, rows):
    return jnp.pad(m, ((0, rows - m.shape[0]), (0, 0)))


def kernel(x, c, ctx, c_ctx, ada_w, ada_b, pre_g, post_g, rw_mu, rw_w_rkvg, rw_w0, rw_w1, rw_w2, rw_a0, rw_a1, rw_a2, rw_v0, rw_v1, rw_v2, rw_k_k, rw_k_a, rw_r_k, rw_lnx_w, rw_lnx_b, rw_w_out, na_w_in, na_b_in, na_rpb, na_w_out):
    B, T, _ = x.shape
    depth = ada_w.shape[0]
    assert ctx.shape[1] == TM and T % (RQ * GRID_W) == 0 and T % TM == 0 and T // GRID_W >= WIN_H

    seg = (np.arange(D)[:, None] // HD) == np.arange(PW)[None, :]
    e = jnp.asarray(seg, BF16)
    et = jnp.asarray(seg.T, BF16)

    cond = jnp.concatenate([c, c_ctx[None, :]], axis=0)
    nrow = -(-(B + 1) // 8) * 8
    cond = _pad_rows(cond * jax.nn.sigmoid(cond), nrow)
    mod_all = _adaln(cond, ada_w, ada_b)
    mod_ctx = jnp.broadcast_to(mod_all[:, B:B + 1], (depth, B, 3 * D))
    mod_all = jnp.stack([mod_ctx, mod_all[:, :B]], axis=2)[:, :, :, None, :]

    bias_all = _na_bias_table(na_rpb)
    xc = (ctx, x)
    v_first = None
    na_proj = None
    for i in range(depth):
        last = i == depth - 1
        j = i // 2
        mod = mod_all[i]
        if i % 2 == 0:
            zero = jnp.zeros((D,), F32)
            vec = jnp.stack([pre_g[i], *rw_mu[j], rw_k_k[j], rw_k_a[j], rw_r_k[j].reshape(D),
                             rw_w0[j, 0], rw_w0[j, 1], rw_a0[j, 0], rw_a0[j, 1],
                             rw_v0[j - 1] if j > 0 else zero, zero])
            lora = D // 16
            w1c = _bf(jnp.concatenate([rw_w1[j, 0], rw_w1[j, 1]], axis=1))
            a1c = _bf(jnp.concatenate([rw_a1[j, 0], rw_a1[j, 1]], axis=1))
            zl = jnp.zeros((lora, D), F32)
            w2z = _bf(jnp.stack([jnp.concatenate([rw_w2[j, 0], zl]), jnp.concatenate([zl, rw_w2[j, 1]])]))
            a2z = _bf(jnp.stack([jnp.concatenate([rw_a2[j, 0], zl]), jnp.concatenate([zl, rw_a2[j, 1]])]))
            vres = None
            if j > 0:
                v1p = _bf(jnp.pad(rw_v1[j - 1], ((0, 0), (0, PW - rw_v1.shape[-1]))))
                v2p = _bf(_pad_rows(rw_v2[j - 1], PW))
                vres = (v1p, v2p, v_first)
            wq = _bf(rw_w_rkvg[j])
            r, v, kk, g, bonus, kd0, kd1, b0, b1, lw0, lw1 = _rwkv_proj(
                xc, mod, vec, wq[0], wq[1], wq[2], wq[3], w1c, w2z, a1c, a2z, e, et, vres)
            if j == 0:
                v_first = v
            y0, y1 = _rwkv_scan(r, kk, v, kd0, b0, lw0, kd1, b1, lw1)
            vec_o = _pad_rows(jnp.stack([rw_lnx_w[j], rw_lnx_b[j], post_g[i]]), 8)
            if last:
                xc = _rwkv_out(y0, y1, bonus, g, xc, mod, vec_o, _bf(rw_w_out[j]), e, et)
            else:
                jn = (i + 1) // 2
                vec_n = _pad_rows(jnp.stack([pre_g[i + 1], post_g[i + 1]]), 8)
                xc, *na_proj = _rwkv_out(y0, y1, bonus, g, xc, mod, vec_o, _bf(rw_w_out[j]), e, et,
                                         na=(mod_all[i + 1], vec_n, _bf(na_w_in[jn]), na_b_in[jn][None, :]))
        else:
            vec = _pad_rows(jnp.stack([pre_g[i], post_g[i]]), 8)
            if na_proj is None:
                na_proj = _na_proj(xc, mod, vec, _bf(na_w_in[j]), na_b_in[j][None, :])
            q, k, v, g = na_proj
            na_proj = None
            o_lat = _na_attn(q, k, v, bias_all[j])
            o_ctx = None if last else _ctx_attn(q, k, v)
            xc = _na_out(o_lat, o_ctx, g, xc, mod, vec, _bf(na_w_out[j]))
    return xc if xc.shape[1] == T else xc[:, TM:]
```

```python
import functools
import math

import numpy as np
import jax
import jax.numpy as jnp
from jax import lax
from jax.experimental import pallas as pl
from jax.experimental.pallas import tpu as pltpu

F32 = jnp.float32
BF16 = jnp.bfloat16

D = 1024
HD = 64
HD_SHIFT = HD.bit_length() - 1
PW = 2 * HD
NPAIR = D // PW
TM = 256
CH = 64
assert CH == HD
GRID_W = 64
WIN_H = 8
WIN_W = 16
RMS_EPS = 1e-6
LNX_EPS = 64e-5
NEG_INF = -1e30
EXP_M05 = math.exp(-0.5)
RQ = 8
SCAN_PAIRS = 4
ATT_PAIRS = 2
ATT_GROUP = 4
OUT_SPLIT = 4
VMEM_LIMIT = 56 * 1024 * 1024


def _bf(x):
    return x.astype(BF16)


def _dot(a, b):
    return jnp.dot(a, b, preferred_element_type=F32)


def _dot_nt(a, b):
    return lax.dot_general(a, b, (((1,), (1,)), ((), ())), preferred_element_type=F32)


def _dot_tn(a, b):
    return lax.dot_general(a, b, (((0,), (0,)), ((), ())), preferred_element_type=F32)


def _dot_split(x, e):
    hi = _bf(x)
    lo = _bf(x - hi.astype(F32))
    return _dot(hi, e) + _dot(lo, e)


def _seg_reduce(x, e, two_pass=False):
    return _dot_split(x, e) if two_pass else _dot(_bf(x), e)


def _seg_expand(c, et):
    return _dot_split(c, et)


def _sigmoid(x):
    return 0.5 * jnp.tanh(0.5 * x) + 0.5


def _silu(x):
    half = 0.5 * x
    return half * jnp.tanh(half) + half


def _prenorm(x, g, scale, shift):
    ms = jnp.mean(x * x, axis=-1, keepdims=True)
    return x * lax.rsqrt(ms + RMS_EPS) * (g * (1.0 + scale)) + shift


def _const_spec(shape):
    nd = len(shape)
    return pl.BlockSpec(shape, lambda *_: (0,) * nd, pipeline_mode=pl.Buffered(1))


def _params(sem):
    return pltpu.CompilerParams(dimension_semantics=sem, vmem_limit_bytes=VMEM_LIMIT)


def _adaln_kernel(s_ref, w_ref, b_ref, o_ref):
    s = s_ref[...]
    w = w_ref[0]
    hi = _bf(s)
    lo = _bf(s - hi.astype(F32))
    whi = _bf(w)
    wlo = _bf(w - whi.astype(F32))
    o_ref[0] = _dot(hi, whi) + _dot(lo, whi) + _dot(hi, wlo) + b_ref[0]


def _adaln(silu_rows, ada_w, ada_b):
    depth = ada_w.shape[0]
    nrow = silu_rows.shape[0]
    return pl.pallas_call(
        _adaln_kernel,
        grid=(depth, 3),
        in_specs=[pl.BlockSpec((nrow, D), lambda i, j: (0, 0)),
                  pl.BlockSpec((1, D, D), lambda i, j: (i, 0, j)),
                  pl.BlockSpec((1, 1, D), lambda i, j: (i, 0, j))],
        out_specs=pl.BlockSpec((1, nrow, D), lambda i, j: (i, 0, j)),
        out_shape=jax.ShapeDtypeStruct((depth, nrow, 3 * D), F32),
        compiler_params=_params(("parallel", "parallel")),
        name="adaln",
    )(silu_rows, ada_w, ada_b.reshape(depth, 1, 3 * D))


_V_PRE_G, _V_MU, _V_KK, _V_KA, _V_RK, _V_W0, _V_A0, _V_V0 = 0, 1, 7, 8, 9, 10, 12, 14


def _rwkv_proj_kernel(has_vres, split, nt, *refs):
    if split:
        ctx_ref, refs = refs[0], refs[1:]
    if has_vres:
        (x_ref, xp_ref, xn_ref, mod_ref, vec_ref, wr_ref, wk_ref, wv_ref, wg_ref, w1_ref, w2_ref,
         a1_ref, a2_ref, e_ref, et_ref, v1_ref, v2_ref, vf_ref,
         r_ref, v_ref, kk_ref, g_ref, bonus_ref, kd0_ref, kd1_ref, b0_ref, b1_ref, lw0_ref, lw1_ref) = refs
    else:
        (x_ref, xp_ref, xn_ref, mod_ref, vec_ref, wr_ref, wk_ref, wv_ref, wg_ref, w1_ref, w2_ref,
         a1_ref, a2_ref, e_ref, et_ref,
         r_ref, v_ref, kk_ref, g_ref, bonus_ref, kd0_ref, kd1_ref, b0_ref, b1_ref, lw0_ref, lw1_ref) = refs
    t = pl.program_id(1)
    shift = mod_ref[0, 0, :, 0:D]
    scale = mod_ref[0, 0, :, D:2 * D]
    g_pre = vec_ref[_V_PRE_G:_V_PRE_G + 1]

    x = jnp.where(t == 0, ctx_ref[0], x_ref[0]) if split else x_ref[0]
    h = _prenorm(x, g_pre, scale, shift)
    hp = _prenorm(xp_ref[0], g_pre, scale, shift)[7:8]
    hn = _prenorm(xn_ref[0], g_pre, scale, shift)[0:1]
    hp = jnp.where(t >= 2, hp, 0.0)
    hn = jnp.where(jnp.logical_and(t >= 1, t < nt - 1), hn, 0.0)
    row = lax.broadcasted_iota(jnp.int32, (TM, 1), 0)
    h_m1 = jnp.where(row == 0, hp, pltpu.roll(h, 1, 0))
    h_p1 = jnp.where(row == TM - 1, hn, pltpu.roll(h, TM - 1, 0))
    xx = 0.5 * (h_m1 + h_p1) - h

    def lerp(n):
        return _bf(h + xx * vec_ref[_V_MU + n:_V_MU + n + 1])

    r = _dot(lerp(0), wr_ref[...])
    k = _dot(lerp(2), wk_ref[...])
    xv = lerp(3)
    v = _dot(xv, wv_ref[...])
    g = _silu(_dot(lerp(5), wg_ref[...]))
    if has_vres:
        lv = _dot(_bf(_dot(xv, v1_ref[...])), v2_ref[...])
        v = v + (vf_ref[0] - v) * _sigmoid(vec_ref[_V_V0:_V_V0 + 1] + lv)
    tw = _bf(jnp.tanh(_dot(lerp(1), w1_ref[...])))
    la = _bf(_dot(lerp(4), a1_ref[...]))

    e = e_ref[...]
    et = et_ref[...]
    kkr = k * vec_ref[_V_KK:_V_KK + 1]
    kk = kkr * _seg_expand(jnp.minimum(lax.rsqrt(_seg_reduce(kkr * kkr, e)), 1e12), et)
    half_ka = 0.5 * vec_ref[_V_KA:_V_KA + 1]

    ksum = None
    for d, (kd_ref, b_ref, lw_ref) in enumerate(((kd0_ref, b0_ref, lw0_ref), (kd1_ref, b1_ref, lw1_ref))):
        wl = vec_ref[_V_W0 + d:_V_W0 + d + 1] + _dot(tw, w2_ref[d])
        lw_ref[0] = (-0.5 * EXP_M05) * jnp.tanh(0.5 * wl) + (-0.5 * EXP_M05)
        ta = jnp.tanh(0.5 * (vec_ref[_V_A0 + d:_V_A0 + d + 1] + _dot(la, a2_ref[d])))
        a = 0.5 * ta + 0.5
        kd = k * ((1.0 - half_ka) + half_ka * ta)
        kd_ref[0] = kd
        b_ref[0] = kk * a
        ksum = kd if ksum is None else ksum + kd

    r_ref[0] = r
    v_ref[0] = v
    kk_ref[0] = kk
    g_ref[0] = _bf(g)
    bonus_ref[0] = _bf(_seg_expand(_seg_reduce(r * ksum * vec_ref[_V_RK:_V_RK + 1], e), et) * v)


def _stream_specs(stream):
    if isinstance(stream, tuple):
        ctx, x = stream
        B, T, _ = x.shape
        specs = [pl.BlockSpec((1, TM, D), lambda b, t: (b, 0, 0)),
                 pl.BlockSpec((1, TM, D), lambda b, t: (b, jnp.maximum(t - 1, 0), 0))]
        return specs, [ctx, x], B, T + TM, 1
    B, TT, _ = stream.shape
    return [pl.BlockSpec((1, TM, D), lambda b, t: (b, t, 0))], [stream], B, TT, 0


def _rwkv_proj(stream, mod, vec, wr, wk, wv, wg, w1c, w2z, a1c, a2z, e, et, vres):
    cur_specs, cur_args, B, TT, lat0 = _stream_specs(stream)
    split = lat0 == 1
    nt = TT // TM
    has_vres = vres is not None
    tile = pl.BlockSpec((1, TM, D), lambda b, t: (b, t, 0))
    rows8 = (TT - lat0 * TM) // 8
    in_specs = cur_specs + [
        pl.BlockSpec((1, 8, D), lambda b, t: (b, jnp.maximum((t - lat0) * (TM // 8) - 1, 0), 0)),
        pl.BlockSpec((1, 8, D), lambda b, t: (b, jnp.minimum((t + 1 - lat0) * (TM // 8), rows8 - 1), 0)),
        pl.BlockSpec((1, 1, 1, 3 * D), lambda b, t: (b, jnp.minimum(t, 1), 0, 0)),
        _const_spec((16, D)),
        _const_spec((D, D)), _const_spec((D, D)), _const_spec((D, D)), _const_spec((D, D)),
        _const_spec((D, PW)), _const_spec((2, PW, D)), _const_spec((D, PW)), _const_spec((2, PW, D)),
        _const_spec((D, PW)), _const_spec((PW, D)),
    ]
    args = cur_args + [cur_args[-1], cur_args[-1], mod, vec, wr, wk, wv, wg, w1c, w2z, a1c, a2z, e, et]
    if has_vres:
        v1p, v2p, v_first = vres
        in_specs += [_const_spec((D, PW)), _const_spec((PW, D)), tile]
        args += [v1p, v2p, v_first]
    out = jax.ShapeDtypeStruct((B, TT, D), F32)
    half = jax.ShapeDtypeStruct((B, TT, D), BF16)
    return pl.pallas_call(
        functools.partial(_rwkv_proj_kernel, has_vres, split, nt),
        grid=(B, nt),
        in_specs=in_specs,
        out_specs=[tile] * 11,
        out_shape=[out] * 3 + [half] * 2 + [out] * 6,
        compiler_params=_params(("parallel", "parallel")),
        name="rwkv_proj",
    )(*args)


def _cumsum_rows(x, rev):
    row = lax.broadcasted_iota(jnp.int32, (CH, 1), 0)
    s = 1
    while s < CH:
        if rev:
            x = x + jnp.where(row < CH - s, pltpu.roll(x, CH - s, 0), 0.0)
        else:
            x = x + jnp.where(row >= s, pltpu.roll(x, s, 0), 0.0)
        s *= 2
    return x


def _tri_masks(rev):
    ri = lax.broadcasted_iota(jnp.int32, (PW, PW), 0)
    ci = lax.broadcasted_iota(jnp.int32, (PW, PW), 1)
    same = (ri >> HD_SHIFT) == (ci >> HD_SHIFT)
    rl = ri & (CH - 1)
    cl = ci & (CH - 1)
    if rev:
        return jnp.logical_and(same, cl > rl), jnp.logical_and(same, cl >= rl)
    return jnp.logical_and(same, cl < rl), jnp.logical_and(same, cl <= rl)


def _scan_kernel(*refs):
    in_refs = (refs[0:6], refs[6:12])
    y_refs = refs[12:14]
    st_ref, q2_ref, g_ref, y0_ref, h_ref = refs[14:19]

    @pl.when(pl.program_id(2) == 0)
    def _():
        for ref in (st_ref, q2_ref, g_ref, y0_ref, h_ref):
            ref[...] = jnp.zeros_like(ref)

    nch = TM // CH
    chains = [(d, p, c) for d in (0, 1) for p in range(SCAN_PAIRS)
              for c in (range(nch - 1, -1, -1) if d else range(nch))]
    n = len(chains)
    nst = 2 * SCAN_PAIRS
    st = [st_ref[j] for j in range(nst)]

    def recurrence_round(k):
        for j in range(nst):
            i = j * nch + k
            d, p, c = chains[i]
            sb = _bf(st[j])
            y2 = _dot(q2_ref[i], sb) + y0_ref[i]
            st[j] = _dot(g_ref[i], sb) + h_ref[i]
            y_refs[d][0, pl.ds(c * CH, CH), p * PW:(p + 1) * PW] = _bf(y2[:CH] + y2[CH:])

    lane = lax.broadcasted_iota(jnp.int32, (1, PW), 1)
    m0 = (lane < HD).astype(F32)
    m1 = 1.0 - m0
    ri = lax.broadcasted_iota(jnp.int32, (PW, PW), 0)
    ci = lax.broadcasted_iota(jnp.int32, (PW, PW), 1)
    eye = (ri == ci).astype(F32)
    masks = (_tri_masks(False), _tri_masks(True))
    same = (ri >> HD_SHIFT) == (ci >> HD_SHIFT)

    def stack(x):
        return jnp.concatenate([x * m0, x * m1], axis=0)

    g = {}

    def prepare(g):
        g.update(a2=[], r2=[], v2=[], bk=[], wend=[], m=[])
        for d, p, c in chains:
            r_ref, kk_ref, v_ref, k_ref, b_ref, lw_ref = in_refs[d]
            sl = (0, pl.ds(c * CH, CH), slice(p * PW, (p + 1) * PW))
            lw = lw_ref[sl]
            cum = _cumsum_rows(lw, bool(d))
            last = cum[0:1] if d else cum[CH - 1:CH]
            wend = jnp.exp(last)
            e_n = jnp.exp(-cum)
            e_h = wend * e_n
            b = b_ref[sl]
            k = k_ref[sl]
            a2 = stack(-kk_ref[sl] * jnp.exp(cum - lw))
            r2 = stack(r_ref[sl] * jnp.exp(cum))
            g["a2"].append(a2)
            g["r2"].append(r2)
            g["v2"].append(stack(v_ref[sl]))
            bh = _bf(b * e_h)
            kh = _bf(k * e_h)
            g["bk"].append(jnp.concatenate([bh, bh, kh, kh], axis=0))
            g["wend"].append(wend)
            bt = _bf(b * e_n)
            kt = _bf(k * e_n)
            g["m"].append(_dot_nt(_bf(jnp.concatenate([a2, r2], axis=0)),
                                  jnp.concatenate([bt, bt, kt, kt], axis=0)))

    def mask(g):
        strict = [masks[d][0] for d, _, _ in chains]
        incl = [jnp.concatenate([masks[d][1]] * 2, axis=1) for d, _, _ in chains]
        m_ab = [jnp.where(s, m[:PW, :PW], 0.0) for s, m in zip(strict, g["m"])]
        g["m_ak"] = [_bf(jnp.where(s, m[:PW, PW:], 0.0)) for s, m in zip(strict, g["m"])]
        g["m_r"] = [_bf(jnp.where(s, m[PW:, :], 0.0)) for s, m in zip(incl, g["m"])]
        g["q"] = [eye + x for x in m_ab]
        lb = [_bf(x) for x in m_ab]
        g["l"] = [_dot(x, x) for x in lb]

    def double(g):
        lb = [_bf(x) for x in g["l"]]
        res = [_dot(x, jnp.concatenate([x, _bf(q)], axis=1)) for x, q in zip(lb, g["q"])]
        g["l"] = [x[:, :PW] for x in res]
        g["q"] = [q + x[:, PW:] for q, x in zip(g["q"], res)]

    def invert(g):
        g["tinv"] = [q + _dot(_bf(l), _bf(q)) for q, l in zip(g["q"], g["l"])]
        g["mv"] = [_dot(x, _bf(v)) for x, v in zip(g["m_ak"], g["v2"])]

    def solve(g):
        g["pu"] = [_dot(_bf(t), _bf(jnp.concatenate([a, mv], axis=1)))
                   for t, a, mv in zip(g["tinv"], g["a2"], g["mv"])]

    def combine(g):
        rhs = [_bf(jnp.concatenate([jnp.concatenate([pu[:, PW:], pu[:, :PW]], axis=1),
                                    jnp.concatenate([v, jnp.zeros_like(v)], axis=1)], axis=0))
               for pu, v in zip(g["pu"], g["v2"])]
        g["yp"] = [_dot(x, y) for x, y in zip(g["m_r"], rhs)]
        same2 = jnp.concatenate([same, same], axis=1)
        g["gh"] = [jnp.where(same2, _dot_tn(bk, y), 0.0) for bk, y in zip(g["bk"], rhs)]

    def publish(g):
        for i in range(n):
            g_ref[i] = _bf(g["gh"][i][:, PW:] + eye * g["wend"][i])
            q2_ref[i] = _bf(g["r2"][i] + g["yp"][i][:, PW:])
            y0_ref[i] = g["yp"][i][:, :PW]
            h_ref[i] = g["gh"][i][:, :PW]

    stages = [prepare, mask, double, double, double, double, invert, solve, combine, publish]
    for t, stage in enumerate(stages):
        if t < nch:
            recurrence_round(t)
        if t == nch:
            for j in range(nst):
                st_ref[j] = st[j]
        stage(g)


def _rwkv_scan(r, kk, v, kd0, b0, lw0, kd1, b1, lw1):
    B, TT, _ = r.shape
    nt = TT // TM
    width = SCAN_PAIRS * PW

    def bwd_tile(j):
        return jnp.where(j == 0, 0, nt - j)

    fwd_in = pl.BlockSpec((1, TM, width), lambda b, p, i: (b, jnp.minimum(i, nt - 1), p))
    bwd_in = pl.BlockSpec((1, TM, width), lambda b, p, i: (b, bwd_tile(jnp.minimum(i, nt - 1)), p))
    fwd_out = pl.BlockSpec((1, TM, width), lambda b, p, i: (b, jnp.maximum(i - 1, 0), p))
    bwd_out = pl.BlockSpec((1, TM, width), lambda b, p, i: (b, bwd_tile(jnp.maximum(i - 1, 0)), p))
    out = jax.ShapeDtypeStruct((B, TT, D), BF16)
    n = 2 * SCAN_PAIRS * (TM // CH)
    return pl.pallas_call(
        _scan_kernel,
        grid=(B, NPAIR // SCAN_PAIRS, nt + 1),
        in_specs=[fwd_in] * 6 + [bwd_in] * 6,
        out_specs=[fwd_out, bwd_out],
        out_shape=[out, out],
        scratch_shapes=[pltpu.VMEM((2 * SCAN_PAIRS, PW, PW), F32),
                        pltpu.VMEM((n, PW, PW), BF16), pltpu.VMEM((n, PW, PW), BF16),
                        pltpu.VMEM((n, PW, PW), F32), pltpu.VMEM((n, PW, PW), F32)],
        compiler_params=_params(("parallel", "parallel", "arbitrary")),
        name="rwkv_scan",
    )(r, kk, v, kd0, b0, lw0, r, kk, v, kd1, b1, lw1)


def _out_tail(o, w_ref, post_g, gate, x):
    out = _dot(_bf(o), w_ref[...])
    ms = jnp.mean(out * out, axis=-1, keepdims=True)
    return x + gate * (out * lax.rsqrt(ms + RMS_EPS) * post_g)


def _rwkv_out_kernel(split, y0_ref, y1_ref, bonus_ref, g_ref, *refs):
    if split:
        ctx_ref, refs = refs[0], refs[1:]
    x_ref, mod_ref, vec_ref, w_ref, e_ref, et_ref, o_ref = refs
    e = e_ref[...]
    et = et_ref[...]
    blocks = [pl.ds(i * (TM // OUT_SPLIT), TM // OUT_SPLIT) for i in range(OUT_SPLIT)]
    y = [y0_ref[0, b, :].astype(F32) + y1_ref[0, b, :].astype(F32) for b in blocks]
    mu = [_seg_reduce(v, e, two_pass=True) * (1.0 / HD) for v in y]
    yc = [v - _seg_expand(m, et) for v, m in zip(y, mu)]
    var = [_seg_reduce(v * v, e) * (1.0 / HD) for v in yc]
    yn = [v * _seg_expand(lax.rsqrt(s + LNX_EPS), et) * vec_ref[0:1] + vec_ref[1:2] for v, s in zip(yc, var)]
    o = jnp.concatenate([(v + bonus_ref[0, b, :].astype(F32)) * g_ref[0, b, :].astype(F32)
                         for v, b in zip(yn, blocks)], axis=0)
    x = jnp.where(pl.program_id(1) == 0, ctx_ref[0], x_ref[0]) if split else x_ref[0]
    o_ref[0] = _out_tail(o, w_ref, vec_ref[2:3], mod_ref[0, 0, :, 2 * D:3 * D], x)


def _rwkv_out(y0, y1, bonus, g, stream, mod, vec, w_out, e, et):
    cur_specs, cur_args, B, TT, lat0 = _stream_specs(stream)
    nt = TT // TM
    tile = pl.BlockSpec((1, TM, D), lambda b, t: (b, t, 0))
    return pl.pallas_call(
        functools.partial(_rwkv_out_kernel, lat0 == 1),
        grid=(B, nt),
        in_specs=[tile] * 4 + cur_specs + [
            pl.BlockSpec((1, 1, 1, 3 * D), lambda b, t: (b, jnp.minimum(t, 1), 0, 0)),
            _const_spec((8, D)), _const_spec((D, D)), _const_spec((D, PW)), _const_spec((PW, D))],
        out_specs=tile,
        out_shape=jax.ShapeDtypeStruct((B, TT, D), F32),
        compiler_params=_params(("parallel", "parallel")),
        name="rwkv_out",
    )(y0, y1, bonus, g, *cur_args, mod, vec, w_out, e, et)


def _na_proj_kernel(x_ref, mod_ref, vec_ref, w_ref, bias_ref, q_ref, k_ref, v_ref, g_ref):
    shift = mod_ref[0, 0, :, 0:D]
    scale = mod_ref[0, 0, :, D:2 * D]
    h = _bf(_prenorm(x_ref[0], vec_ref[0:1], scale, shift))
    q = _dot(h, w_ref[:, 0:D]) + bias_ref[:, 0:D]
    q_ref[0] = _bf(q * (HD ** -0.5))
    k_ref[0] = _bf(_dot(h, w_ref[:, D:2 * D]) + bias_ref[:, D:2 * D])
    v_ref[0] = _bf(_dot(h, w_ref[:, 2 * D:3 * D]) + bias_ref[:, 2 * D:3 * D])
    g = _dot(h, w_ref[:, 3 * D:4 * D]) + bias_ref[:, 3 * D:4 * D]
    g_ref[0] = _bf(_silu(g))


def _na_proj(xc, mod, vec, w_in, b_in):
    B, TT, _ = xc.shape
    nt = TT // TM
    tile = pl.BlockSpec((1, TM, D), lambda b, t: (b, t, 0))
    bf = jax.ShapeDtypeStruct((B, TT, D), BF16)
    return pl.pallas_call(
        _na_proj_kernel,
        grid=(B, nt),
        in_specs=[tile,
                  pl.BlockSpec((1, 1, 1, 3 * D), lambda b, t: (b, jnp.minimum(t, 1), 0, 0)),
                  _const_spec((8, D)), _const_spec((D, 4 * D)), _const_spec((1, 4 * D))],
        out_specs=[pl.BlockSpec((1, TM, D), lambda b, t: (b, jnp.where(t == 0, nt - 1, t - 1), 0))] + [tile] * 3,
        out_shape=[bf] * 4,
        compiler_params=_params(("parallel", "parallel")),
        name="na_proj",
    )(xc, mod, vec, w_in, b_in)


def _softmax_pv(s_list, v_list):
    mx = None
    for s in s_list:
        m = jnp.max(s, axis=-1, keepdims=True)
        mx = m if mx is None else jnp.maximum(mx, m)
    den = None
    acc = None
    for s, v in zip(s_list, v_list):
        p = jnp.exp(s - mx)
        d = jnp.sum(p, axis=-1, keepdims=True)
        o = _dot(_bf(p), v)
        den = d if den is None else den + d
        acc = o if acc is None else acc + o
    return acc / den


def _na_attn_kernel(rows, q_ref, k_ref, v_ref, bias_ref, o_ref):
    lane = lax.broadcasted_iota(jnp.int32, (1, PW), 1)
    m0 = lane < HD
    step = pl.program_id(2)
    nkeys = WIN_H * GRID_W
    zero = jnp.zeros((), BF16)
    off, start = [], []
    for rr in range(RQ):
        r = step * RQ + rr
        rs = jnp.clip(r - WIN_H // 2, 0, rows - WIN_H)
        off.append(r - rs)
        start.append(pl.multiple_of(TM + rs * GRID_W, GRID_W))
    groups = [dict(pp=pp, lanes=slice(pp * PW, (pp + 1) * PW), rows=list(range(r0, r0 + ATT_GROUP)))
              for pp in range(ATT_PAIRS) for r0 in range(0, RQ, ATT_GROUP)]

    def scores(g):
        q2 = []
        for rr in g["rows"]:
            q = q_ref[0, rr * GRID_W:(rr + 1) * GRID_W, g["lanes"]]
            q2.append(jnp.concatenate([jnp.where(m0, q, zero), jnp.where(m0, zero, q)], axis=0))
        g["s_ctx"] = _dot_nt(jnp.concatenate(q2, axis=0), k_ref[0, 0:TM, g["lanes"]])
        g["s_win"] = [_dot_nt(q2[j], k_ref[0, pl.ds(start[rr], nkeys), g["lanes"]]) + bias_ref[g["pp"], off[rr]]
                      for j, rr in enumerate(g["rows"])]

    def softmax(g):
        n = len(g["rows"])
        mx_ctx = jnp.max(g["s_ctx"], axis=-1, keepdims=True)
        mx = [jnp.maximum(jnp.max(g["s_win"][j], axis=-1, keepdims=True), mx_ctx[j * PW:(j + 1) * PW])
              for j in range(n)]
        g["p_win"] = [jnp.exp(g["s_win"][j] - mx[j]) for j in range(n)]
        g["p_ctx"] = jnp.exp(g["s_ctx"] - jnp.concatenate(mx, axis=0))

    def weighted_sum(g):
        den_ctx = jnp.sum(g["p_ctx"], axis=-1, keepdims=True)
        o_ctx = _dot(_bf(g["p_ctx"]), v_ref[0, 0:TM, g["lanes"]])
        for j, rr in enumerate(g["rows"]):
            den = jnp.sum(g["p_win"][j], axis=-1, keepdims=True) + den_ctx[j * PW:(j + 1) * PW]
            vw = v_ref[0, pl.ds(start[rr], nkeys), g["lanes"]]
            o2 = (_dot(_bf(g["p_win"][j]), vw) + o_ctx[j * PW:(j + 1) * PW]) * (1.0 / den)
            o_ref[0, rr * GRID_W:(rr + 1) * GRID_W, g["lanes"]] = _bf(jnp.where(m0, o2[:GRID_W], o2[GRID_W:]))

    stages = (scores, softmax, weighted_sum)
    for t in range(len(groups) + len(stages) - 1):
        for k, stage in enumerate(stages):
            if 0 <= t - k < len(groups):
                stage(groups[t - k])


def _na_attn(q, k, v, bias):
    B, TT, _ = q.shape
    T = TT - TM
    rows = T // GRID_W
    qb = RQ * GRID_W
    width = ATT_PAIRS * PW
    kv = pl.BlockSpec((1, TT, width), lambda b, p, s: (b, 0, p))
    return pl.pallas_call(
        functools.partial(_na_attn_kernel, rows),
        grid=(B, NPAIR // ATT_PAIRS, rows // RQ),
        in_specs=[pl.BlockSpec((1, qb, width), lambda b, p, s: (b, s, p)), kv, kv,
                  pl.BlockSpec((ATT_PAIRS, WIN_H, PW, WIN_H * GRID_W), lambda b, p, s: (p, 0, 0, 0))],
        out_specs=pl.BlockSpec((1, qb, width), lambda b, p, s: (b, s, p)),
        out_shape=jax.ShapeDtypeStruct((B, T, D), BF16),
        compiler_params=_params(("parallel", "parallel", "arbitrary")),
        name="na_attn",
    )(q, k, v, bias)


def _ctx_attn_kernel(q_ref, k_ref, v_ref, o_ref):
    lane = lax.broadcasted_iota(jnp.int32, (1, PW), 1)
    m0 = lane < HD
    zero = jnp.zeros((), BF16)
    q = q_ref[0]
    q2 = jnp.concatenate([jnp.where(m0, q, zero), jnp.where(m0, zero, q)], axis=0)
    kc = k_ref[0]
    o2 = _softmax_pv([_dot_nt(q2, kc)], [v_ref[0]])
    o_ref[0] = _bf(jnp.where(m0, o2[:TM], o2[TM:]))


def _ctx_attn(q, k, v):
    B, TT, _ = q.shape
    blk = pl.BlockSpec((1, TM, PW), lambda b, p: (b, 0, p))
    q_ctx = pl.BlockSpec((1, TM, PW), lambda b, p: (b, TT // TM - 1, p))
    return pl.pallas_call(
        _ctx_attn_kernel,
        grid=(B, NPAIR),
        in_specs=[q_ctx, blk, blk],
        out_specs=blk,
        out_shape=jax.ShapeDtypeStruct((B, TM, D), BF16),
        compiler_params=_params(("parallel", "parallel")),
        name="ctx_attn",
    )(q, k, v)


def _na_out_kernel(with_ctx, *refs):
    if with_ctx:
        ol_ref, oc_ref, g_ref, x_ref, mod_ref, vec_ref, w_ref, o_ref = refs
        o = jnp.where(pl.program_id(1) == 0, oc_ref[0], ol_ref[0])
    else:
        ol_ref, g_ref, x_ref, mod_ref, vec_ref, w_ref, o_ref = refs
        o = ol_ref[0]
    o = o.astype(F32) * g_ref[0].astype(F32)
    o_ref[0] = _out_tail(o, w_ref, vec_ref[1:2], mod_ref[0, 0, :, 2 * D:3 * D], x_ref[0])


def _na_out(o_lat, o_ctx, g, xc, mod, vec, w_out):
    B, TT, _ = xc.shape
    nt = TT // TM
    with_ctx = o_ctx is not None
    consts = [_const_spec((8, D)), _const_spec((D, D))]
    if with_ctx:
        tile = pl.BlockSpec((1, TM, D), lambda b, t: (b, t, 0))
        in_specs = [pl.BlockSpec((1, TM, D), lambda b, t: (b, jnp.maximum(t - 1, 0), 0)),
                    pl.BlockSpec((1, TM, D), lambda b, t: (b, 0, 0)),
                    tile, tile,
                    pl.BlockSpec((1, 1, 1, 3 * D), lambda b, t: (b, jnp.minimum(t, 1), 0, 0))] + consts
        args = (o_lat, o_ctx, g, xc, mod, vec, w_out)
        grid, out_spec, out_rows = (B, nt), tile, TT
    else:
        lat = pl.BlockSpec((1, TM, D), lambda b, t: (b, t + 1, 0))
        out_spec = pl.BlockSpec((1, TM, D), lambda b, t: (b, t, 0))
        in_specs = [out_spec, lat, lat,
                    pl.BlockSpec((1, 1, 1, 3 * D), lambda b, t: (b, 1, 0, 0))] + consts
        args = (o_lat, g, xc, mod, vec, w_out)
        grid, out_rows = (B, nt - 1), TT - TM
    return pl.pallas_call(
        functools.partial(_na_out_kernel, with_ctx),
        grid=grid,
        in_specs=in_specs,
        out_specs=out_spec,
        out_shape=jax.ShapeDtypeStruct((B, out_rows, D), F32),
        compiler_params=_params(("parallel", "parallel")),
        name="na_out",
    )(*args)


def _na_bias_table(rpb):
    ncol = 2 * WIN_W - 1
    j = np.arange(GRID_W)
    win_start = np.clip(j - WIN_W // 2, 0, GRID_W - WIN_W)
    kcol = np.arange(GRID_W)
    valid = (kcol[None, :] >= win_start[:, None]) & (kcol[None, :] < win_start[:, None] + WIN_W)
    dc = np.clip(kcol[None, :] - j[:, None], -(WIN_W - 1), WIN_W - 1) + (WIN_W - 1)
    onehot = jnp.asarray(dc[None] == np.arange(ncol)[:, None, None], F32)
    rp = rpb.astype(F32).reshape(rpb.shape[0], NPAIR, 2, 2 * WIN_H - 1, ncol)
    base = jnp.einsum('lphrc,cqk->lphqrk', rp, onehot, precision=lax.Precision.HIGHEST)
    base = jnp.where(jnp.asarray(valid)[None, None, None, :, None, :], base, NEG_INF)
    halves = []
    for hl in range(2):
        b = base[:, :, hl]
        per_off = [b[:, :, :, WIN_H - 1 - off:2 * WIN_H - 1 - off, :].reshape(b.shape[0], NPAIR, GRID_W, WIN_H * GRID_W)
                   for off in range(WIN_H)]
        halves.append(jnp.stack(per_off, axis=2))
    return jnp.concatenate(halves, axis=3)


def _pad_rows(m, rows):
    return jnp.pad(m, ((0, rows - m.shape[0]), (0, 0)))


def kernel(x, c, ctx, c_ctx, ada_w, ada_b, pre_g, post_g, rw_mu, rw_w_rkvg, rw_w0, rw_w1, rw_w2, rw_a0, rw_a1, rw_a2, rw_v0, rw_v1, rw_v2, rw_k_k, rw_k_a, rw_r_k, rw_lnx_w, rw_lnx_b, rw_w_out, na_w_in, na_b_in, na_rpb, na_w_out):
    B, T, _ = x.shape
    depth = ada_w.shape[0]
    assert ctx.shape[1] == TM and T % (RQ * GRID_W) == 0 and T % TM == 0 and T // GRID_W >= WIN_H

    seg = (np.arange(D)[:, None] // HD) == np.arange(PW)[None, :]
    e = jnp.asarray(seg, BF16)
    et = jnp.asarray(seg.T, BF16)

    cond = jnp.concatenate([c, c_ctx[None, :]], axis=0)
    nrow = -(-(B + 1) // 8) * 8
    cond = _pad_rows(cond * jax.nn.sigmoid(cond), nrow)
    mod_all = _adaln(cond, ada_w, ada_b)
    mod_ctx = jnp.broadcast_to(mod_all[:, B:B + 1], (depth, B, 3 * D))
    mod_all = jnp.stack([mod_ctx, mod_all[:, :B]], axis=2)[:, :, :, None, :]

    bias_all = _na_bias_table(na_rpb)
    xc = (ctx, x)
    v_first = None
    for i in range(depth):
        last = i == depth - 1
        j = i // 2
        mod = mod_all[i]
        if i % 2 == 0:
            zero = jnp.zeros((D,), F32)
            vec = jnp.stack([pre_g[i], *rw_mu[j], rw_k_k[j], rw_k_a[j], rw_r_k[j].reshape(D),
                             rw_w0[j, 0], rw_w0[j, 1], rw_a0[j, 0], rw_a0[j, 1],
                             rw_v0[j - 1] if j > 0 else zero, zero])
            lora = D // 16
            w1c = _bf(jnp.concatenate([rw_w1[j, 0], rw_w1[j, 1]], axis=1))
            a1c = _bf(jnp.concatenate([rw_a1[j, 0], rw_a1[j, 1]], axis=1))
            zl = jnp.zeros((lora, D), F32)
            w2z = _bf(jnp.stack([jnp.concatenate([rw_w2[j, 0], zl]), jnp.concatenate([zl, rw_w2[j, 1]])]))
            a2z = _bf(jnp.stack([jnp.concatenate([rw_a2[j, 0], zl]), jnp.concatenate([zl, rw_a2[j, 1]])]))
            vres = None
            if j > 0:
                v1p = _bf(jnp.pad(rw_v1[j - 1], ((0, 0), (0, PW - rw_v1.shape[-1]))))
                v2p = _bf(_pad_rows(rw_v2[j - 1], PW))
                vres = (v1p, v2p, v_first)
            wq = _bf(rw_w_rkvg[j])
            r, v, kk, g, bonus, kd0, kd1, b0, b1, lw0, lw1 = _rwkv_proj(
                xc, mod, vec, wq[0], wq[1], wq[2], wq[3], w1c, w2z, a1c, a2z, e, et, vres)
            if j == 0:
                v_first = v
            y0, y1 = _rwkv_scan(r, kk, v, kd0, b0, lw0, kd1, b1, lw1)
            vec_o = _pad_rows(jnp.stack([rw_lnx_w[j], rw_lnx_b[j], post_g[i]]), 8)
            xc = _rwkv_out(y0, y1, bonus, g, xc, mod, vec_o, _bf(rw_w_out[j]), e, et)
        else:
            vec = _pad_rows(jnp.stack([pre_g[i], post_g[i]]), 8)
            q, k, v, g = _na_proj(xc, mod, vec, _bf(na_w_in[j]), na_b_in[j][None, :])
            o_lat = _na_attn(q, k, v, bias_all[j])
            o_ctx = None if last else _ctx_attn(q, k, v)
            xc = _na_out(o_lat, o_ctx, g, xc, mod, vec, _bf(na_w_out[j]))
    return xc if xc.shape[1] == T else xc[:, TM:]
```

```python
import functools
import math

import numpy as np
import jax
import jax.numpy as jnp
from jax import lax
from jax.experimental import pallas as pl
from jax.experimental.pallas import tpu as pltpu

F32 = jnp.float32
BF16 = jnp.bfloat16

D = 1024
HD = 64
HD_SHIFT = HD.bit_length() - 1
NH = D // HD
PW = 2 * HD
NPAIR = D // PW
TM = 256
CH = 64
assert CH == HD
GRID_W = 64
WIN_H = 8
WIN_W = 16
RMS_EPS = 1e-6
LNX_EPS = 64e-5
NEG_INF = -1e30
EXP_M05 = math.exp(-0.5)
RQ = 8
SCAN_PAIRS = 4
ATT_PAIRS = 2
ATT_GROUP = 4
OUT_SPLIT = 4
VMEM_LIMIT = 56 * 1024 * 1024


def _bf(x):
    return x.astype(BF16)


def _dot(a, b):
    return jnp.dot(a, b, preferred_element_type=F32)


def _dot_nt(a, b):
    return lax.dot_general(a, b, (((1,), (1,)), ((), ())), preferred_element_type=F32)


def _dot_tn(a, b):
    return lax.dot_general(a, b, (((0,), (0,)), ((), ())), preferred_element_type=F32)


def _dot_split(x, e):
    hi = _bf(x)
    lo = _bf(x - hi.astype(F32))
    return _dot(hi, e) + _dot(lo, e)


def _seg_reduce(x, e, two_pass=False):
    return _dot_split(x, e) if two_pass else _dot(_bf(x), e)


def _seg_expand(c, et):
    return _dot_split(c, et)


def _sigmoid(x):
    return 0.5 * jnp.tanh(0.5 * x) + 0.5


def _silu(x):
    half = 0.5 * x
    return half * jnp.tanh(half) + half


def _prenorm(x, g, scale, shift):
    ms = jnp.mean(x * x, axis=-1, keepdims=True)
    return x * lax.rsqrt(ms + RMS_EPS) * (g * (1.0 + scale)) + shift


def _const_spec(shape):
    nd = len(shape)
    return pl.BlockSpec(shape, lambda *_: (0,) * nd, pipeline_mode=pl.Buffered(1))


def _params(sem, fuse_inputs=None):
    return pltpu.CompilerParams(dimension_semantics=sem, vmem_limit_bytes=VMEM_LIMIT, allow_input_fusion=fuse_inputs)


def _adaln_kernel(s_ref, w_ref, b_ref, o_ref):
    s = s_ref[...]
    w = w_ref[0]
    hi = _bf(s)
    lo = _bf(s - hi.astype(F32))
    whi = _bf(w)
    wlo = _bf(w - whi.astype(F32))
    o_ref[0] = _dot(hi, whi) + _dot(lo, whi) + _dot(hi, wlo) + b_ref[0]


def _adaln(silu_rows, ada_w, ada_b):
    depth = ada_w.shape[0]
    nrow = silu_rows.shape[0]
    return pl.pallas_call(
        _adaln_kernel,
        grid=(depth, 3),
        in_specs=[pl.BlockSpec((nrow, D), lambda i, j: (0, 0)),
                  pl.BlockSpec((1, D, D), lambda i, j: (i, 0, j)),
                  pl.BlockSpec((1, 1, D), lambda i, j: (i, 0, j))],
        out_specs=pl.BlockSpec((1, nrow, D), lambda i, j: (i, 0, j)),
        out_shape=jax.ShapeDtypeStruct((depth, nrow, 3 * D), F32),
        compiler_params=_params(("parallel", "parallel")),
        name="adaln",
    )(silu_rows, ada_w, ada_b.reshape(depth, 1, 3 * D))


_V_PRE_G, _V_MU, _V_KK, _V_KA, _V_RK, _V_W0, _V_A0, _V_V0 = 0, 1, 7, 8, 9, 10, 12, 14


def _rwkv_proj_kernel(has_vres, split, nt, *refs):
    if split:
        ctx_ref, refs = refs[0], refs[1:]
    if has_vres:
        (x_ref, xp_ref, xn_ref, mod_ref, vec_ref, wr_ref, wk_ref, wv_ref, wg_ref, w1_ref, w2_ref,
         a1_ref, a2_ref, e_ref, et_ref, v1_ref, v2_ref, vf_ref,
         r_ref, v_ref, kk_ref, g_ref, bonus_ref, kd0_ref, kd1_ref, b0_ref, b1_ref, lw0_ref, lw1_ref) = refs
    else:
        (x_ref, xp_ref, xn_ref, mod_ref, vec_ref, wr_ref, wk_ref, wv_ref, wg_ref, w1_ref, w2_ref,
         a1_ref, a2_ref, e_ref, et_ref,
         r_ref, v_ref, kk_ref, g_ref, bonus_ref, kd0_ref, kd1_ref, b0_ref, b1_ref, lw0_ref, lw1_ref) = refs
    t = pl.program_id(1)
    shift = mod_ref[0, 0, :, 0:D]
    scale = mod_ref[0, 0, :, D:2 * D]
    g_pre = vec_ref[_V_PRE_G:_V_PRE_G + 1]

    x = jnp.where(t == 0, ctx_ref[0], x_ref[0]) if split else x_ref[0]
    h = _prenorm(x, g_pre, scale, shift)
    hp = _prenorm(xp_ref[0], g_pre, scale, shift)[7:8]
    hn = _prenorm(xn_ref[0], g_pre, scale, shift)[0:1]
    hp = jnp.where(t >= 2, hp, 0.0)
    hn = jnp.where(jnp.logical_and(t >= 1, t < nt - 1), hn, 0.0)
    row = lax.broadcasted_iota(jnp.int32, (TM, 1), 0)
    h_m1 = jnp.where(row == 0, hp, pltpu.roll(h, 1, 0))
    h_p1 = jnp.where(row == TM - 1, hn, pltpu.roll(h, TM - 1, 0))
    xx = 0.5 * (h_m1 + h_p1) - h

    def lerp(n):
        return _bf(h + xx * vec_ref[_V_MU + n:_V_MU + n + 1])

    r = _dot(lerp(0), wr_ref[...])
    k = _dot(lerp(2), wk_ref[...])
    xv = lerp(3)
    v = _dot(xv, wv_ref[...])
    g = _silu(_dot(lerp(5), wg_ref[...]))
    if has_vres:
        lv = _dot(_bf(_dot(xv, v1_ref[...])), v2_ref[...])
        v = v + (vf_ref[0] - v) * _sigmoid(vec_ref[_V_V0:_V_V0 + 1] + lv)
    tw = _bf(jnp.tanh(_dot(lerp(1), w1_ref[...])))
    la = _bf(_dot(lerp(4), a1_ref[...]))

    e = e_ref[...]
    et = et_ref[...]
    kkr = k * vec_ref[_V_KK:_V_KK + 1]
    kk = kkr * _seg_expand(jnp.minimum(lax.rsqrt(_seg_reduce(kkr * kkr, e)), 1e12), et)
    half_ka = 0.5 * vec_ref[_V_KA:_V_KA + 1]

    ksum = None
    for d, (kd_ref, b_ref, lw_ref) in enumerate(((kd0_ref, b0_ref, lw0_ref), (kd1_ref, b1_ref, lw1_ref))):
        wl = vec_ref[_V_W0 + d:_V_W0 + d + 1] + _dot(tw, w2_ref[d])
        lw_ref[0] = (-0.5 * EXP_M05) * jnp.tanh(0.5 * wl) + (-0.5 * EXP_M05)
        ta = jnp.tanh(0.5 * (vec_ref[_V_A0 + d:_V_A0 + d + 1] + _dot(la, a2_ref[d])))
        a = 0.5 * ta + 0.5
        kd = k * ((1.0 - half_ka) + half_ka * ta)
        kd_ref[0] = kd
        b_ref[0] = kk * a
        ksum = kd if ksum is None else ksum + kd

    r_ref[0] = r
    v_ref[0] = v
    kk_ref[0] = kk
    g_ref[0] = _bf(g)
    bonus_ref[0] = _bf(_seg_expand(_seg_reduce(r * ksum * vec_ref[_V_RK:_V_RK + 1], e), et) * v)


def _stream_specs(stream):
    if isinstance(stream, tuple):
        ctx, x = stream
        B, T, _ = x.shape
        specs = [pl.BlockSpec((1, TM, D), lambda b, t: (b, 0, 0)),
                 pl.BlockSpec((1, TM, D), lambda b, t: (b, jnp.maximum(t - 1, 0), 0))]
        return specs, [ctx, x], B, T + TM, 1
    B, TT, _ = stream.shape
    return [pl.BlockSpec((1, TM, D), lambda b, t: (b, t, 0))], [stream], B, TT, 0


def _rwkv_proj(stream, mod, vec, wr, wk, wv, wg, w1c, w2z, a1c, a2z, e, et, vres):
    cur_specs, cur_args, B, TT, lat0 = _stream_specs(stream)
    split = lat0 == 1
    nt = TT // TM
    has_vres = vres is not None
    tile = pl.BlockSpec((1, TM, D), lambda b, t: (b, t, 0))
    rows8 = (TT - lat0 * TM) // 8
    in_specs = cur_specs + [
        pl.BlockSpec((1, 8, D), lambda b, t: (b, jnp.maximum((t - lat0) * (TM // 8) - 1, 0), 0)),
        pl.BlockSpec((1, 8, D), lambda b, t: (b, jnp.minimum((t + 1 - lat0) * (TM // 8), rows8 - 1), 0)),
        pl.BlockSpec((1, 1, 1, 3 * D), lambda b, t: (b, jnp.minimum(t, 1), 0, 0)),
        _const_spec((16, D)),
        _const_spec((D, D)), _const_spec((D, D)), _const_spec((D, D)), _const_spec((D, D)),
        _const_spec((D, PW)), _const_spec((2, PW, D)), _const_spec((D, PW)), _const_spec((2, PW, D)),
        _const_spec((D, PW)), _const_spec((PW, D)),
    ]
    args = cur_args + [cur_args[-1], cur_args[-1], mod, vec, wr, wk, wv, wg, w1c, w2z, a1c, a2z, e, et]
    if has_vres:
        v1p, v2p, v_first = vres
        in_specs += [_const_spec((D, PW)), _const_spec((PW, D)), tile]
        args += [v1p, v2p, v_first]
    out = jax.ShapeDtypeStruct((B, TT, D), F32)
    half = jax.ShapeDtypeStruct((B, TT, D), BF16)
    return pl.pallas_call(
        functools.partial(_rwkv_proj_kernel, has_vres, split, nt),
        grid=(B, nt),
        in_specs=in_specs,
        out_specs=[tile] * 11,
        out_shape=[out] * 3 + [half] * 2 + [out] * 6,
        compiler_params=_params(("parallel", "parallel")),
        name="rwkv_proj",
    )(*args)


def _cumsum_rows(x, rev):
    row = lax.broadcasted_iota(jnp.int32, (CH, 1), 0)
    s = 1
    while s < CH:
        if rev:
            x = x + jnp.where(row < CH - s, pltpu.roll(x, CH - s, 0), 0.0)
        else:
            x = x + jnp.where(row >= s, pltpu.roll(x, s, 0), 0.0)
        s *= 2
    return x


def _tri_masks(rev):
    ri = lax.broadcasted_iota(jnp.int32, (PW, PW), 0)
    ci = lax.broadcasted_iota(jnp.int32, (PW, PW), 1)
    same = (ri >> HD_SHIFT) == (ci >> HD_SHIFT)
    rl = ri & (CH - 1)
    cl = ci & (CH - 1)
    if rev:
        return jnp.logical_and(same, cl > rl), jnp.logical_and(same, cl >= rl)
    return jnp.logical_and(same, cl < rl), jnp.logical_and(same, cl <= rl)


def _scan_kernel(*refs):
    in_refs = (refs[0:6], refs[6:12])
    y_refs = refs[12:14]
    st_ref, q2_ref, g_ref, y0_ref, h_ref = refs[14:19]

    @pl.when(pl.program_id(2) == 0)
    def _():
        for ref in (st_ref, q2_ref, g_ref, y0_ref, h_ref):
            ref[...] = jnp.zeros_like(ref)

    nch = TM // CH
    chains = [(d, p, c) for d in (0, 1) for p in range(SCAN_PAIRS)
              for c in (range(nch - 1, -1, -1) if d else range(nch))]
    n = len(chains)
    nst = 2 * SCAN_PAIRS
    st = [st_ref[j] for j in range(nst)]

    def recurrence_round(k):
        for j in range(nst):
            i = j * nch + k
            d, p, c = chains[i]
            sb = _bf(st[j])
            y2 = _dot(q2_ref[i], sb) + y0_ref[i]
            st[j] = _dot(g_ref[i], sb) + h_ref[i]
            y_refs[d][0, pl.ds(c * CH, CH), p * PW:(p + 1) * PW] = _bf(y2[:CH] + y2[CH:])

    lane = lax.broadcasted_iota(jnp.int32, (1, PW), 1)
    m0 = (lane < HD).astype(F32)
    m1 = 1.0 - m0
    ri = lax.broadcasted_iota(jnp.int32, (PW, PW), 0)
    ci = lax.broadcasted_iota(jnp.int32, (PW, PW), 1)
    eye = (ri == ci).astype(F32)
    masks = (_tri_masks(False), _tri_masks(True))
    same = (ri >> HD_SHIFT) == (ci >> HD_SHIFT)

    def stack(x):
        return jnp.concatenate([x * m0, x * m1], axis=0)

    g = {}

    def prepare(g):
        g.update(a2=[], r2=[], v2=[], bk=[], wend=[], m=[])
        for d, p, c in chains:
            r_ref, kk_ref, v_ref, k_ref, b_ref, lw_ref = in_refs[d]
            sl = (0, pl.ds(c * CH, CH), slice(p * PW, (p + 1) * PW))
            lw = lw_ref[sl]
            cum = _cumsum_rows(lw, bool(d))
            last = cum[0:1] if d else cum[CH - 1:CH]
            wend = jnp.exp(last)
            e_n = jnp.exp(-cum)
            e_h = wend * e_n
            b = b_ref[sl]
            k = k_ref[sl]
            a2 = stack(-kk_ref[sl] * jnp.exp(cum - lw))
            r2 = stack(r_ref[sl] * jnp.exp(cum))
            g["a2"].append(a2)
            g["r2"].append(r2)
            g["v2"].append(stack(v_ref[sl]))
            bh = _bf(b * e_h)
            kh = _bf(k * e_h)
            g["bk"].append(jnp.concatenate([bh, bh, kh, kh], axis=0))
            g["wend"].append(wend)
            bt = _bf(b * e_n)
            kt = _bf(k * e_n)
            g["m"].append(_dot_nt(_bf(jnp.concatenate([a2, r2], axis=0)),
                                  jnp.concatenate([bt, bt, kt, kt], axis=0)))

    def mask(g):
        strict = [masks[d][0] for d, _, _ in chains]
        incl = [jnp.concatenate([masks[d][1]] * 2, axis=1) for d, _, _ in chains]
        m_ab = [jnp.where(s, m[:PW, :PW], 0.0) for s, m in zip(strict, g["m"])]
        g["m_ak"] = [_bf(jnp.where(s, m[:PW, PW:], 0.0)) for s, m in zip(strict, g["m"])]
        g["m_r"] = [_bf(jnp.where(s, m[PW:, :], 0.0)) for s, m in zip(incl, g["m"])]
        g["q"] = [eye + x for x in m_ab]
        lb = [_bf(x) for x in m_ab]
        g["l"] = [_dot(x, x) for x in lb]

    def double(g):
        lb = [_bf(x) for x in g["l"]]
        res = [_dot(x, jnp.concatenate([x, _bf(q)], axis=1)) for x, q in zip(lb, g["q"])]
        g["l"] = [x[:, :PW] for x in res]
        g["q"] = [q + x[:, PW:] for q, x in zip(g["q"], res)]

    def invert(g):
        g["tinv"] = [q + _dot(_bf(l), _bf(q)) for q, l in zip(g["q"], g["l"])]
        g["mv"] = [_dot(x, _bf(v)) for x, v in zip(g["m_ak"], g["v2"])]

    def solve(g):
        g["pu"] = [_dot(_bf(t), _bf(jnp.concatenate([a, mv], axis=1)))
                   for t, a, mv in zip(g["tinv"], g["a2"], g["mv"])]

    def combine(g):
        rhs = [_bf(jnp.concatenate([jnp.concatenate([pu[:, PW:], pu[:, :PW]], axis=1),
                                    jnp.concatenate([v, jnp.zeros_like(v)], axis=1)], axis=0))
               for pu, v in zip(g["pu"], g["v2"])]
        g["yp"] = [_dot(x, y) for x, y in zip(g["m_r"], rhs)]
        same2 = jnp.concatenate([same, same], axis=1)
        g["gh"] = [jnp.where(same2, _dot_tn(bk, y), 0.0) for bk, y in zip(g["bk"], rhs)]

    def publish(g):
        for i in range(n):
            g_ref[i] = _bf(g["gh"][i][:, PW:] + eye * g["wend"][i])
            q2_ref[i] = _bf(g["r2"][i] + g["yp"][i][:, PW:])
            y0_ref[i] = g["yp"][i][:, :PW]
            h_ref[i] = g["gh"][i][:, :PW]

    stages = [prepare, mask, double, double, double, double, invert, solve, combine, publish]
    for t, stage in enumerate(stages):
        if t < nch:
            recurrence_round(t)
        if t == nch:
            for j in range(nst):
                st_ref[j] = st[j]
        stage(g)


def _rwkv_scan(r, kk, v, kd0, b0, lw0, kd1, b1, lw1):
    B, TT, _ = r.shape
    nt = TT // TM
    width = SCAN_PAIRS * PW

    def bwd_tile(j):
        return jnp.where(j == 0, 0, nt - j)

    fwd_in = pl.BlockSpec((1, TM, width), lambda b, p, i: (b, jnp.minimum(i, nt - 1), p))
    bwd_in = pl.BlockSpec((1, TM, width), lambda b, p, i: (b, bwd_tile(jnp.minimum(i, nt - 1)), p))
    fwd_out = pl.BlockSpec((1, TM, width), lambda b, p, i: (b, jnp.maximum(i - 1, 0), p))
    bwd_out = pl.BlockSpec((1, TM, width), lambda b, p, i: (b, bwd_tile(jnp.maximum(i - 1, 0)), p))
    out = jax.ShapeDtypeStruct((B, TT, D), BF16)
    n = 2 * SCAN_PAIRS * (TM // CH)
    return pl.pallas_call(
        _scan_kernel,
        grid=(B, NPAIR // SCAN_PAIRS, nt + 1),
        in_specs=[fwd_in] * 6 + [bwd_in] * 6,
        out_specs=[fwd_out, bwd_out],
        out_shape=[out, out],
        scratch_shapes=[pltpu.VMEM((2 * SCAN_PAIRS, PW, PW), F32),
                        pltpu.VMEM((n, PW, PW), BF16), pltpu.VMEM((n, PW, PW), BF16),
                        pltpu.VMEM((n, PW, PW), F32), pltpu.VMEM((n, PW, PW), F32)],
        compiler_params=_params(("parallel", "parallel", "arbitrary")),
        name="rwkv_scan",
    )(r, kk, v, kd0, b0, lw0, r, kk, v, kd1, b1, lw1)


def _out_tail(o, w_ref, post_g, gate, x):
    out = _dot(_bf(o), w_ref[...])
    ms = jnp.mean(out * out, axis=-1, keepdims=True)
    return x + gate * (out * lax.rsqrt(ms + RMS_EPS) * post_g)


def _rwkv_out_kernel(split, y0_ref, y1_ref, bonus_ref, g_ref, *refs):
    if split:
        ctx_ref, refs = refs[0], refs[1:]
    x_ref, mod_ref, vec_ref, w_ref, e_ref, et_ref, o_ref = refs
    e = e_ref[...]
    et = et_ref[...]
    blocks = [pl.ds(i * (TM // OUT_SPLIT), TM // OUT_SPLIT) for i in range(OUT_SPLIT)]
    y = [y0_ref[0, b, :].astype(F32) + y1_ref[0, b, :].astype(F32) for b in blocks]
    mu = [_seg_reduce(v, e, two_pass=True) * (1.0 / HD) for v in y]
    yc = [v - _seg_expand(m, et) for v, m in zip(y, mu)]
    var = [_seg_reduce(v * v, e) * (1.0 / HD) for v in yc]
    yn = [v * _seg_expand(lax.rsqrt(s + LNX_EPS), et) * vec_ref[0:1] + vec_ref[1:2] for v, s in zip(yc, var)]
    o = jnp.concatenate([(v + bonus_ref[0, b, :].astype(F32)) * g_ref[0, b, :].astype(F32)
                         for v, b in zip(yn, blocks)], axis=0)
    x = jnp.where(pl.program_id(1) == 0, ctx_ref[0], x_ref[0]) if split else x_ref[0]
    o_ref[0] = _out_tail(o, w_ref, vec_ref[2:3], mod_ref[0, 0, :, 2 * D:3 * D], x)


def _rwkv_out(y0, y1, bonus, g, stream, mod, vec, w_out, e, et):
    cur_specs, cur_args, B, TT, lat0 = _stream_specs(stream)
    nt = TT // TM
    tile = pl.BlockSpec((1, TM, D), lambda b, t: (b, t, 0))
    return pl.pallas_call(
        functools.partial(_rwkv_out_kernel, lat0 == 1),
        grid=(B, nt),
        in_specs=[tile] * 4 + cur_specs + [
            pl.BlockSpec((1, 1, 1, 3 * D), lambda b, t: (b, jnp.minimum(t, 1), 0, 0)),
            _const_spec((8, D)), _const_spec((D, D)), _const_spec((D, PW)), _const_spec((PW, D))],
        out_specs=tile,
        out_shape=jax.ShapeDtypeStruct((B, TT, D), F32),
        compiler_params=_params(("parallel", "parallel")),
        name="rwkv_out",
    )(y0, y1, bonus, g, *cur_args, mod, vec, w_out, e, et)


def _na_proj_kernel(x_ref, mod_ref, vec_ref, w_ref, bias_ref, q_ref, k_ref, v_ref, g_ref):
    shift = mod_ref[0, 0, :, 0:D]
    scale = mod_ref[0, 0, :, D:2 * D]
    h = _bf(_prenorm(x_ref[0], vec_ref[0:1], scale, shift))
    q = _dot(h, w_ref[:, 0:D]) + bias_ref[:, 0:D]
    q_ref[0] = _bf(q * (HD ** -0.5))
    k_ref[0] = _bf(_dot(h, w_ref[:, D:2 * D]) + bias_ref[:, D:2 * D])
    v_ref[0] = _bf(_dot(h, w_ref[:, 2 * D:3 * D]) + bias_ref[:, 2 * D:3 * D])
    g = _dot(h, w_ref[:, 3 * D:4 * D]) + bias_ref[:, 3 * D:4 * D]
    g_ref[0] = _bf(_silu(g))


def _na_proj(xc, mod, vec, w_in, b_in):
    B, TT, _ = xc.shape
    nt = TT // TM
    tile = pl.BlockSpec((1, TM, D), lambda b, t: (b, t, 0))
    bf = jax.ShapeDtypeStruct((B, TT, D), BF16)
    return pl.pallas_call(
        _na_proj_kernel,
        grid=(B, nt),
        in_specs=[tile,
                  pl.BlockSpec((1, 1, 1, 3 * D), lambda b, t: (b, jnp.minimum(t, 1), 0, 0)),
                  _const_spec((8, D)), _const_spec((D, 4 * D)), _const_spec((1, 4 * D))],
        out_specs=[pl.BlockSpec((1, TM, D), lambda b, t: (b, jnp.where(t == 0, nt - 1, t - 1), 0))] + [tile] * 3,
        out_shape=[bf] * 4,
        compiler_params=_params(("parallel", "parallel"), fuse_inputs=[False, True, True, True, True]),
        name="na_proj",
    )(xc, mod, vec, w_in, b_in)


def _softmax_pv(s_list, v_list):
    mx = None
    for s in s_list:
        m = jnp.max(s, axis=-1, keepdims=True)
        mx = m if mx is None else jnp.maximum(mx, m)
    den = None
    acc = None
    for s, v in zip(s_list, v_list):
        p = jnp.exp(s - mx)
        d = jnp.sum(p, axis=-1, keepdims=True)
        o = _dot(_bf(p), v)
        den = d if den is None else den + d
        acc = o if acc is None else acc + o
    return acc / den


def _na_attn_kernel(rows, q_ref, k_ref, v_ref, bias_ref, o_ref):
    lane = lax.broadcasted_iota(jnp.int32, (1, PW), 1)
    m0 = lane < HD
    step = pl.program_id(2)
    nkeys = WIN_H * GRID_W
    zero = jnp.zeros((), BF16)
    off, start = [], []
    for rr in range(RQ):
        r = step * RQ + rr
        rs = jnp.clip(r - WIN_H // 2, 0, rows - WIN_H)
        off.append(r - rs)
        start.append(pl.multiple_of(TM + rs * GRID_W, GRID_W))
    groups = [dict(pp=pp, lanes=slice(pp * PW, (pp + 1) * PW), rows=list(range(r0, r0 + ATT_GROUP)))
              for pp in range(ATT_PAIRS) for r0 in range(0, RQ, ATT_GROUP)]

    def scores(g):
        q2 = []
        for rr in g["rows"]:
            q = q_ref[0, rr * GRID_W:(rr + 1) * GRID_W, g["lanes"]]
            q2.append(jnp.concatenate([jnp.where(m0, q, zero), jnp.where(m0, zero, q)], axis=0))
        g["s_ctx"] = _dot_nt(jnp.concatenate(q2, axis=0), k_ref[0, 0:TM, g["lanes"]])
        g["s_win"] = [_dot_nt(q2[j], k_ref[0, pl.ds(start[rr], nkeys), g["lanes"]]) + bias_ref[g["pp"], off[rr]]
                      for j, rr in enumerate(g["rows"])]

    def softmax(g):
        n = len(g["rows"])
        mx_ctx = jnp.max(g["s_ctx"], axis=-1, keepdims=True)
        mx = [jnp.maximum(jnp.max(g["s_win"][j], axis=-1, keepdims=True), mx_ctx[j * PW:(j + 1) * PW])
              for j in range(n)]
        g["p_win"] = [jnp.exp(g["s_win"][j] - mx[j]) for j in range(n)]
        g["p_ctx"] = jnp.exp(g["s_ctx"] - jnp.concatenate(mx, axis=0))

    def weighted_sum(g):
        den_ctx = jnp.sum(g["p_ctx"], axis=-1, keepdims=True)
        o_ctx = _dot(_bf(g["p_ctx"]), v_ref[0, 0:TM, g["lanes"]])
        for j, rr in enumerate(g["rows"]):
            den = jnp.sum(g["p_win"][j], axis=-1, keepdims=True) + den_ctx[j * PW:(j + 1) * PW]
            vw = v_ref[0, pl.ds(start[rr], nkeys), g["lanes"]]
            o2 = (_dot(_bf(g["p_win"][j]), vw) + o_ctx[j * PW:(j + 1) * PW]) * (1.0 / den)
            o_ref[0, rr * GRID_W:(rr + 1) * GRID_W, g["lanes"]] = _bf(jnp.where(m0, o2[:GRID_W], o2[GRID_W:]))

    stages = (scores, softmax, weighted_sum)
    for t in range(len(groups) + len(stages) - 1):
        for k, stage in enumerate(stages):
            if 0 <= t - k < len(groups):
                stage(groups[t - k])


def _na_attn(q, k, v, bias):
    B, TT, _ = q.shape
    T = TT - TM
    rows = T // GRID_W
    qb = RQ * GRID_W
    width = ATT_PAIRS * PW
    kv = pl.BlockSpec((1, TT, width), lambda b, p, s: (b, 0, p))
    return pl.pallas_call(
        functools.partial(_na_attn_kernel, rows),
        grid=(B, NPAIR // ATT_PAIRS, rows // RQ),
        in_specs=[pl.BlockSpec((1, qb, width), lambda b, p, s: (b, s, p)), kv, kv,
                  pl.BlockSpec((ATT_PAIRS, WIN_H, PW, WIN_H * GRID_W), lambda b, p, s: (p, 0, 0, 0))],
        out_specs=pl.BlockSpec((1, qb, width), lambda b, p, s: (b, s, p)),
        out_shape=jax.ShapeDtypeStruct((B, T, D), BF16),
        compiler_params=_params(("parallel", "parallel", "arbitrary")),
        name="na_attn",
    )(q, k, v, bias)


def _ctx_attn_kernel(q_ref, k_ref, v_ref, o_ref):
    lane = lax.broadcasted_iota(jnp.int32, (1, PW), 1)
    m0 = lane < HD
    zero = jnp.zeros((), BF16)
    q = q_ref[0]
    q2 = jnp.concatenate([jnp.where(m0, q, zero), jnp.where(m0, zero, q)], axis=0)
    kc = k_ref[0]
    o2 = _softmax_pv([_dot_nt(q2, kc)], [v_ref[0]])
    o_ref[0] = _bf(jnp.where(m0, o2[:TM], o2[TM:]))


def _ctx_attn(q, k, v):
    B, TT, _ = q.shape
    blk = pl.BlockSpec((1, TM, PW), lambda b, p: (b, 0, p))
    q_ctx = pl.BlockSpec((1, TM, PW), lambda b, p: (b, TT // TM - 1, p))
    return pl.pallas_call(
        _ctx_attn_kernel,
        grid=(B, NPAIR),
        in_specs=[q_ctx, blk, blk],
        out_specs=blk,
        out_shape=jax.ShapeDtypeStruct((B, TM, D), BF16),
        compiler_params=_params(("parallel", "parallel")),
        name="ctx_attn",
    )(q, k, v)


def _na_out_kernel(with_ctx, *refs):
    if with_ctx:
        ol_ref, oc_ref, g_ref, x_ref, mod_ref, vec_ref, w_ref, o_ref = refs
        o = jnp.where(pl.program_id(1) == 0, oc_ref[0], ol_ref[0])
    else:
        ol_ref, g_ref, x_ref, mod_ref, vec_ref, w_ref, o_ref = refs
        o = ol_ref[0]
    o = o.astype(F32) * g_ref[0].astype(F32)
    o_ref[0] = _out_tail(o, w_ref, vec_ref[1:2], mod_ref[0, 0, :, 2 * D:3 * D], x_ref[0])


def _na_out(o_lat, o_ctx, g, xc, mod, vec, w_out):
    B, TT, _ = xc.shape
    nt = TT // TM
    with_ctx = o_ctx is not None
    consts = [_const_spec((8, D)), _const_spec((D, D))]
    if with_ctx:
        tile = pl.BlockSpec((1, TM, D), lambda b, t: (b, t, 0))
        in_specs = [pl.BlockSpec((1, TM, D), lambda b, t: (b, jnp.maximum(t - 1, 0), 0)),
                    pl.BlockSpec((1, TM, D), lambda b, t: (b, 0, 0)),
                    tile, tile,
                    pl.BlockSpec((1, 1, 1, 3 * D), lambda b, t: (b, jnp.minimum(t, 1), 0, 0))] + consts
        args = (o_lat, o_ctx, g, xc, mod, vec, w_out)
        grid, out_spec, out_rows = (B, nt), tile, TT
    else:
        lat = pl.BlockSpec((1, TM, D), lambda b, t: (b, t + 1, 0))
        out_spec = pl.BlockSpec((1, TM, D), lambda b, t: (b, t, 0))
        in_specs = [out_spec, lat, lat,
                    pl.BlockSpec((1, 1, 1, 3 * D), lambda b, t: (b, 1, 0, 0))] + consts
        args = (o_lat, g, xc, mod, vec, w_out)
        grid, out_rows = (B, nt - 1), TT - TM
    return pl.pallas_call(
        functools.partial(_na_out_kernel, with_ctx),
        grid=grid,
        in_specs=in_specs,
        out_specs=out_spec,
        out_shape=jax.ShapeDtypeStruct((B, out_rows, D), F32),
        compiler_params=_params(("parallel", "parallel")),
        name="na_out",
    )(*args)


def _na_bias_table(rpb):
    ncol = 2 * WIN_W - 1
    j = np.arange(GRID_W)
    win_start = np.clip(j - WIN_W // 2, 0, GRID_W - WIN_W)
    kcol = np.arange(GRID_W)
    valid = (kcol[None, :] >= win_start[:, None]) & (kcol[None, :] < win_start[:, None] + WIN_W)
    dc = np.clip(kcol[None, :] - j[:, None], -(WIN_W - 1), WIN_W - 1) + (WIN_W - 1)
    onehot = jnp.asarray(dc[None] == np.arange(ncol)[:, None, None], F32)
    rp = rpb.astype(F32).reshape(rpb.shape[0], NPAIR, 2, 2 * WIN_H - 1, ncol)
    base = jnp.einsum('lphrc,cqk->lphqrk', rp, onehot, precision=lax.Precision.HIGHEST)
    base = jnp.where(jnp.asarray(valid)[None, None, None, :, None, :], base, NEG_INF)
    halves = []
    for hl in range(2):
        b = base[:, :, hl]
        per_off = [b[:, :, :, WIN_H - 1 - off:2 * WIN_H - 1 - off, :].reshape(b.shape[0], NPAIR, GRID_W, WIN_H * GRID_W)
                   for off in range(WIN_H)]
        halves.append(jnp.stack(per_off, axis=2))
    return jnp.concatenate(halves, axis=3)


def _pad_rows(m, rows):
    return jnp.pad(m, ((0, rows - m.shape[0]), (0, 0)))


def kernel(x, c, ctx, c_ctx, ada_w, ada_b, pre_g, post_g, rw_mu, rw_w_rkvg, rw_w0, rw_w1, rw_w2, rw_a0, rw_a1, rw_a2, rw_v0, rw_v1, rw_v2, rw_k_k, rw_k_a, rw_r_k, rw_lnx_w, rw_lnx_b, rw_w_out, na_w_in, na_b_in, na_rpb, na_w_out):
    B, T, _ = x.shape
    depth = ada_w.shape[0]
    assert ctx.shape[1] == TM and T % (RQ * GRID_W) == 0 and T % TM == 0 and T // GRID_W >= WIN_H

    seg = (np.arange(D)[:, None] // HD) == np.arange(PW)[None, :]
    e = jnp.asarray(seg, BF16)
    et = jnp.asarray(seg.T, BF16)

    cond = jnp.concatenate([c, c_ctx[None, :]], axis=0)
    nrow = -(-(B + 1) // 8) * 8
    cond = _pad_rows(cond * jax.nn.sigmoid(cond), nrow)
    mod_all = _adaln(cond, ada_w, ada_b)
    mod_ctx = jnp.broadcast_to(mod_all[:, B:B + 1], (depth, B, 3 * D))
    mod_all = jnp.stack([mod_ctx, mod_all[:, :B]], axis=2)[:, :, :, None, :]

    bias_all = _na_bias_table(na_rpb)
    xc = (ctx, x)
    v_first = None
    for i in range(depth):
        last = i == depth - 1
        j = i // 2
        mod = mod_all[i]
        if i % 2 == 0:
            zero = jnp.zeros((D,), F32)
            vec = jnp.stack([pre_g[i], *rw_mu[j], rw_k_k[j], rw_k_a[j], rw_r_k[j].reshape(D),
                             rw_w0[j, 0], rw_w0[j, 1], rw_a0[j, 0], rw_a0[j, 1],
                             rw_v0[j - 1] if j > 0 else zero, zero])
            lora = D // 16
            w1c = _bf(jnp.concatenate([rw_w1[j, 0], rw_w1[j, 1]], axis=1))
            a1c = _bf(jnp.concatenate([rw_a1[j, 0], rw_a1[j, 1]], axis=1))
            zl = jnp.zeros((lora, D), F32)
            w2z = _bf(jnp.stack([jnp.concatenate([rw_w2[j, 0], zl]), jnp.concatenate([zl, rw_w2[j, 1]])]))
            a2z = _bf(jnp.stack([jnp.concatenate([rw_a2[j, 0], zl]), jnp.concatenate([zl, rw_a2[j, 1]])]))
            vres = None
            if j > 0:
                v1p = _bf(jnp.pad(rw_v1[j - 1], ((0, 0), (0, PW - rw_v1.shape[-1]))))
                v2p = _bf(_pad_rows(rw_v2[j - 1], PW))
                vres = (v1p, v2p, v_first)
            wq = _bf(rw_w_rkvg[j])
            r, v, kk, g, bonus, kd0, kd1, b0, b1, lw0, lw1 = _rwkv_proj(
                xc, mod, vec, wq[0], wq[1], wq[2], wq[3], w1c, w2z, a1c, a2z, e, et, vres)
            if j == 0:
                v_first = v
            y0, y1 = _rwkv_scan(r, kk, v, kd0, b0, lw0, kd1, b1, lw1)
            vec_o = _pad_rows(jnp.stack([rw_lnx_w[j], rw_lnx_b[j], post_g[i]]), 8)
            xc = _rwkv_out(y0, y1, bonus, g, xc, mod, vec_o, _bf(rw_w_out[j]), e, et)
        else:
            vec = _pad_rows(jnp.stack([pre_g[i], post_g[i]]), 8)
            q, k, v, g = _na_proj(xc, mod, vec, _bf(na_w_in[j]), na_b_in[j][None, :])
            o_lat = _na_attn(q, k, v, bias_all[j])
            o_ctx = None if last else _ctx_attn(q, k, v)
            xc = _na_out(o_lat, o_ctx, g, xc, mod, vec, _bf(na_w_out[j]))
    return xc if xc.shape[1] == T else xc[:, TM:]
```
